```python
import math
import jax, jax.numpy as jnp
from jax import lax
import numpy as np

D_MODEL = 1024
BATCH = 8
SEQ = 8192
DEPTH = 2

GRID_W = 64
CTX_LEN = 256
BRANCH_W = D_MODEL
GLA_HEADS = 4
GLA_DK = D_MODEL // (2 * GLA_HEADS)
GLA_DV = BRANCH_W // GLA_HEADS
GLA_RANK = 16
GLA_TAU = 16.0
GLA_CHUNK = 64
DIFF_HEADS = 8
DIFF_DH = BRANCH_W // (2 * DIFF_HEADS)
MLA_HEADS = 8
MLA_Q_RANK = 256
MLA_KV_RANK = 128
MLA_NOPE = 128
MLA_ROPE = 64
MLA_DV = BRANCH_W // MLA_HEADS
MLA_SCALE = (MLA_NOPE + MLA_ROPE) ** -0.5
ROPE_DIM = 64
ROPE_BASE = 10000.0
Q_BLOCK = 128
FFN_HIDDEN = ((8 * D_MODEL // 3 + 255) // 256) * 256
DEEPNORM_ALPHA = (2 * DEPTH) ** 0.25
DEEPNORM_BETA = (8 * DEPTH) ** -0.25
EPS = 1e-6
IN_SIZES = (GLA_HEADS * GLA_DK, GLA_HEADS * GLA_DK, BRANCH_W, BRANCH_W, 2 * GLA_RANK,
            2 * DIFF_HEADS * DIFF_DH, 2 * DIFF_HEADS * DIFF_DH, BRANCH_W,
            MLA_Q_RANK, MLA_KV_RANK, MLA_ROPE, 3 * D_MODEL)
N_IN = sum(IN_SIZES)

kernel_name = "hybrid_gla_diff_mla_dit_block"


def layer_norm(x, g=None, b=None):
    xf = x.astype(jnp.float32)
    mu = jnp.mean(xf, axis=-1, keepdims=True)
    var = jnp.mean(jnp.square(xf - mu), axis=-1, keepdims=True)
    y = (xf - mu) * lax.rsqrt(var + EPS)
    if g is not None:
        y = y * g + b
    return y.astype(x.dtype)


def rms_norm(x, g):
    xf = x.astype(jnp.float32)
    y = xf * lax.rsqrt(jnp.mean(jnp.square(xf), axis=-1, keepdims=True) + EPS) * g
    return y.astype(x.dtype)


def modulate(x, shift, scale):
    return layer_norm(x) * (1.0 + scale) + shift


def axial_rope_tables(L, dtype):
    rows = L // GRID_W
    pos_row = jnp.broadcast_to(jnp.arange(rows, dtype=jnp.float32)[:, None], (rows, GRID_W)).reshape(L)
    pos_col = jnp.broadcast_to(jnp.arange(GRID_W, dtype=jnp.float32)[None, :], (rows, GRID_W)).reshape(L)
    d_axis = ROPE_DIM // 2
    inv = ROPE_BASE ** (-jnp.arange(0, d_axis, 2, dtype=jnp.float32) / d_axis)
    ang = jnp.concatenate([pos_row[:, None] * inv, pos_col[:, None] * inv], axis=-1)
    return jnp.cos(ang).astype(dtype), jnp.sin(ang).astype(dtype)


def apply_rope(x, cos, sin):
    half = ROPE_DIM // 2
    x1, x2 = x[..., :half], x[..., half:]
    return jnp.concatenate([x1 * cos - x2 * sin, x1 * sin + x2 * cos], axis=-1)


def split_columns(p):
    offsets = []
    acc = 0
    for size in IN_SIZES[:-1]:
        acc += size
        offsets.append(acc)
    return jnp.split(p, offsets, axis=-1)


def gla_chunked(q, k, v, log_a, s0, inclusive):
    B, H, L, dk = q.shape
    dv = v.shape[-1]
    n = L // GLA_CHUNK

    def chunks(t):
        return jnp.moveaxis(t.reshape(B, H, n, GLA_CHUNK, t.shape[-1]), 2, 0)

    mask = jnp.tril(jnp.ones((GLA_CHUNK, GLA_CHUNK), dtype=bool), k=0 if inclusive else -1)

    def step(s, inp):
        qc, kc, vc, gc = inp
        b = jnp.cumsum(gc, axis=-2)
        b_last = b[:, :, -1:, :]
        o_inter = jnp.einsum('bhcd,bhde->bhce', qc * jnp.exp(b), s)
        rel = jnp.where(mask[:, :, None], b[:, :, :, None, :] - b[:, :, None, :, :], -jnp.inf)
        a = jnp.einsum('bhid,bhjd,bhijd->bhij', qc, kc, jnp.exp(rel))
        o_intra = jnp.einsum('bhij,bhje->bhie', a, vc)
        s_new = jnp.exp(b_last[:, :, 0, :])[..., None] * s + jnp.einsum(
            'bhcd,bhce->bhde', kc * jnp.exp(b_last - b), vc)
        return s_new, o_inter + o_intra

    s_fin, o = lax.scan(step, s0, (chunks(q), chunks(k), chunks(v), chunks(log_a)))
    o = jnp.moveaxis(o, 0, 2).reshape(B, H, L, dv)
    return o.astype(v.dtype), s_fin


def gla_bidir(q, k, v, la_f, la_b, s0_f, s0_b):
    o_f, s_f = gla_chunked(q, k, v, la_f, s0_f, True)
    flip = lambda t: jnp.flip(t, axis=2)
    o_b, s_b = gla_chunked(flip(q), flip(k), flip(v), flip(la_b), s0_b, False)
    return o_f + flip(o_b), s_f, s_b


def diff_core(q, k, v, lam):
    s = jnp.einsum('bhmqd,bhmkd->bhmqk', q, k)
    p = jax.nn.softmax(s.astype(jnp.float32), axis=-1)
    w = p[:, :, 0] - lam * p[:, :, 1]
    return jnp.einsum('bhqk,bhkd->bhqd', w.astype(v.dtype), v)


def mla_core(q_nope, q_rope, k_nope, k_rope, v):
    s = jnp.einsum('bhqd,bhkd->bhqk', q_nope, k_nope) + jnp.einsum('bhqd,bkd->bhqk', q_rope, k_rope)
    p = jax.nn.softmax(s.astype(jnp.float32), axis=-1)
    return jnp.einsum('bhqk,bhkd->bhqd', p.astype(v.dtype), v)


def to_blocks(t):
    nb = t.shape[-2] // Q_BLOCK
    t = t.reshape(t.shape[:-2] + (nb, Q_BLOCK, t.shape[-1]))
    return jnp.moveaxis(t, -3, 0)


def from_blocks(o):
    o = jnp.moveaxis(o, 0, -3)
    return o.reshape(o.shape[:-3] + (-1, o.shape[-1]))


def sweep_query_blocks(core, qs):
    return from_blocks(lax.map(lambda qb: core(*qb), tuple(to_blocks(q) for q in qs)))


def merge_heads(o):
    B, H, L, d = o.shape
    return o.transpose(0, 2, 1, 3).reshape(B, L, H * d)


def mixer_features(h, rope, w_in, gla_w_a2, gla_b_a, mla_q_norm_g, mla_kv_norm_g, mla_w_uq, mla_w_ukv):
    B, L, _ = h.shape
    gq, gk, gv, gr, ga, dq, dk, dv, mq, mkv, mkr, gates = split_columns(h @ w_in)
    heads = lambda t, n: t.reshape(B, L, n, -1).transpose(0, 2, 1, 3)
    ga = ga.reshape(B, L, 2, GLA_RANK).astype(jnp.float32)
    log_a = jax.nn.log_sigmoid(jnp.einsum('blnr,nre->nble', ga, gla_w_a2)
                               + gla_b_a[:, None, None, :]) / GLA_TAU
    g_q = heads(gq, GLA_HEADS) * GLA_DK ** -0.5
    g_k = heads(gk, GLA_HEADS)
    g_v = heads(gv, GLA_HEADS)
    la_f = heads(log_a[0], GLA_HEADS)
    la_b = heads(log_a[1], GLA_HEADS)
    d_q = dq.reshape(B, L, DIFF_HEADS, 2, DIFF_DH).transpose(0, 2, 3, 1, 4) * DIFF_DH ** -0.5
    d_k = dk.reshape(B, L, DIFF_HEADS, 2, DIFF_DH).transpose(0, 2, 3, 1, 4)
    d_v = heads(dv, DIFF_HEADS)
    q_full = heads(rms_norm(mq, mla_q_norm_g) @ mla_w_uq, MLA_HEADS)
    q_nope, q_rope = q_full[..., :MLA_NOPE], q_full[..., MLA_NOPE:]
    kv = heads(rms_norm(mkv, mla_kv_norm_g) @ mla_w_ukv, MLA_HEADS)
    k_nope, m_v = kv[..., :MLA_NOPE], kv[..., MLA_NOPE:]
    k_rope = mkr
    if rope is not None:
        cos, sin = rope
        d_q = apply_rope(d_q, cos, sin)
        d_k = apply_rope(d_k, cos, sin)
        q_rope = apply_rope(q_rope, cos, sin)
        k_rope = apply_rope(k_rope, cos, sin)
    gla = (g_q, g_k, g_v, la_f, la_b, gr)
    diff = (d_q, d_k, d_v)
    mla = (q_nope * MLA_SCALE, q_rope * MLA_SCALE, k_nope, k_rope, m_v)
    return gla, diff, mla, gates


def merge_branches(o_gla, r, o_diff, o_mla, gates, lam_init, gla_norm_g, diff_norm_g, w_branch, w_out):
    y_a = merge_heads(rms_norm(o_gla, gla_norm_g)) * jax.nn.silu(r)
    y_b = merge_heads(rms_norm(o_diff, diff_norm_g) * (1.0 - lam_init))
    y_c = merge_heads(o_mla)
    g_a, g_b, g_c = jnp.split(gates, 3, axis=-1)
    y = (jax.nn.sigmoid(g_a) * (y_a @ w_branch[0]) + jax.nn.sigmoid(g_b) * (y_b @ w_branch[1])
         + jax.nn.sigmoid(g_c) * (y_c @ w_branch[2]))
    return y @ w_out


def token_mixer(h, hc, rope, lam, lam_init, w_in, gla_w_a2, gla_b_a, gla_norm_g, diff_norm_g,
                mla_q_norm_g, mla_kv_norm_g, mla_w_uq, mla_w_ukv, w_branch, w_out, ctx_out):
    feat_args = (w_in, gla_w_a2, gla_b_a, mla_q_norm_g, mla_kv_norm_g, mla_w_uq, mla_w_ukv)
    lat_gla, lat_diff, lat_mla, lat_gates = mixer_features(h, rope, *feat_args)
    ctx_gla, ctx_diff, ctx_mla, ctx_gates = mixer_features(hc, None, *feat_args)
    B = h.shape[0]
    s0 = jnp.zeros((B, GLA_HEADS, GLA_DK, GLA_DV), jnp.float32)
    o_gc, s_f, s_b = gla_bidir(*ctx_gla[:5], s0, s0)
    o_g, _, _ = gla_bidir(*lat_gla[:5], s_f, s_b)
    dq, dk, dv = lat_diff
    cq, ck, cv = ctx_diff
    dk_all = jnp.concatenate([dk, ck], axis=3)
    dv_all = jnp.concatenate([dv, cv], axis=2)
    o_d = sweep_query_blocks(lambda q: diff_core(q, dk_all, dv_all, lam), (dq,))
    qn, qr, kn, kr, mv = lat_mla
    cqn, cqr, ckn, ckr, cmv = ctx_mla
    kn_all = jnp.concatenate([kn, ckn], axis=2)
    kr_all = jnp.concatenate([kr, ckr], axis=1)
    mv_all = jnp.concatenate([mv, cmv], axis=2)
    o_m = sweep_query_blocks(lambda a, b: mla_core(a, b, kn_all, kr_all, mv_all), (qn, qr))
    merge_args = (lam_init, gla_norm_g, diff_norm_g, w_branch, w_out)
    y = merge_branches(o_g, lat_gla[5], o_d, o_m, lat_gates, *merge_args)
    if not ctx_out:
        return y, None
    o_dc = diff_core(cq, ck, cv, lam)
    o_mc = mla_core(cqn, cqr, ckn, ckr, cmv)
    yc = merge_branches(o_gc, ctx_gla[5], o_dc, o_mc, ctx_gates, *merge_args)
    return y, yc


def swiglu(h, w_in, w_out):
    gate, up = jnp.split(h @ w_in, 2, axis=-1)
    return (jax.nn.silu(gate) * up) @ w_out


def setup_inputs(seed: int = 0) -> dict:
    key = jax.random.key(seed)
    ks = jax.random.split(key, 24)
    D, L = D_MODEL, DEPTH

    def nrm(k, shape, scale):
        return jax.random.normal(k, shape, jnp.float32) * scale

    def gain(k, shape):
        return 1.0 + nrm(k, shape, 0.02)

    return {
        "x": nrm(ks[0], (BATCH, SEQ, D), 1.0),
        "c": nrm(ks[1], (BATCH, D), 1.0),
        "ctx": nrm(ks[2], (BATCH, CTX_LEN, D), 1.0),
        "c_ctx": nrm(ks[3], (D,), 1.0),
        "w_mod": nrm(ks[4], (L, D, 6 * D), D ** -0.5),
        "b_mod": nrm(ks[5], (L, 6 * D), 0.02),
        "w_in": nrm(ks[6], (L, D, N_IN), D ** -0.5),
        "gla_w_a2": nrm(ks[7], (L, 2, GLA_RANK, GLA_HEADS * GLA_DK), GLA_RANK ** -0.5),
        "gla_b_a": nrm(ks[8], (L, 2, GLA_HEADS * GLA_DK), 0.02),
        "gla_norm_g": gain(ks[9], (L, GLA_DV)),
        "diff_lam": nrm(ks[10], (L, 4, DIFF_DH), 0.1),
        "diff_norm_g": gain(ks[11], (L, 2 * DIFF_DH)),
        "mla_q_norm_g": gain(ks[12], (L, MLA_Q_RANK)),
        "mla_kv_norm_g": gain(ks[13], (L, MLA_KV_RANK)),
        "mla_w_uq": nrm(ks[14], (L, MLA_Q_RANK, MLA_HEADS * (MLA_NOPE + MLA_ROPE)), MLA_Q_RANK ** -0.5),
        "mla_w_ukv": nrm(ks[15], (L, MLA_KV_RANK, MLA_HEADS * (MLA_NOPE + MLA_DV)), MLA_KV_RANK ** -0.5),
        "w_branch": nrm(ks[16], (L, 3, BRANCH_W, D), BRANCH_W ** -0.5),
        "w_out": nrm(ks[17], (L, D, D), D ** -0.5 * DEEPNORM_BETA),
        "ln1_g": gain(ks[18], (L, D)),
        "ln1_b": nrm(ks[19], (L, D), 0.02),
        "ffn_w_in": nrm(ks[20], (L, D, 2 * FFN_HIDDEN), D ** -0.5),
        "ffn_w_out": nrm(ks[21], (L, FFN_HIDDEN, D), FFN_HIDDEN ** -0.5 * DEEPNORM_BETA),
        "ln2_g": gain(ks[22], (L, D)),
        "ln2_b": nrm(ks[23], (L, D), 0.02),
    }


def reference(x, c, ctx, c_ctx, w_mod, b_mod, w_in, gla_w_a2, gla_b_a, gla_norm_g, diff_lam,
              diff_norm_g, mla_q_norm_g, mla_kv_norm_g, mla_w_uq, mla_w_ukv, w_branch, w_out,
              ln1_g, ln1_b, ffn_w_in, ffn_w_out, ln2_g, ln2_b):
    rope = axial_rope_tables(x.shape[1], x.dtype)
    c_act = jax.nn.silu(c)
    cc_act = jax.nn.silu(c_ctx)
    xc = ctx
    for l in range(DEPTH):
        last = l == DEPTH - 1
        sh1, sc1, g1, sh2, sc2, g2 = jnp.split((c_act @ w_mod[l] + b_mod[l])[:, None, :], 6, axis=-1)
        csh1, csc1, cg1, csh2, csc2, cg2 = jnp.split(cc_act @ w_mod[l] + b_mod[l], 6, axis=-1)
        lam_init = 0.8 - 0.6 * math.exp(-0.3 * l)
        dl = diff_lam[l].astype(jnp.float32)
        lam = jnp.exp(jnp.sum(dl[0] * dl[1])) - jnp.exp(jnp.sum(dl[2] * dl[3])) + lam_init
        y, yc = token_mixer(modulate(x, sh1, sc1), modulate(xc, csh1, csc1), rope, lam, lam_init,
                            w_in[l], gla_w_a2[l], gla_b_a[l], gla_norm_g[l], diff_norm_g[l],
                            mla_q_norm_g[l], mla_kv_norm_g[l], mla_w_uq[l], mla_w_ukv[l],
                            w_branch[l], w_out[l], not last)
        x = layer_norm(DEEPNORM_ALPHA * x + g1 * y, ln1_g[l], ln1_b[l])
        x = layer_norm(DEEPNORM_ALPHA * x + g2 * swiglu(modulate(x, sh2, sc2), ffn_w_in[l], ffn_w_out[l]),
                       ln2_g[l], ln2_b[l])
        if not last:
            xc = layer_norm(DEEPNORM_ALPHA * xc + cg1 * yc, ln1_g[l], ln1_b[l])
            xc = layer_norm(DEEPNORM_ALPHA * xc + cg2 * swiglu(modulate(xc, csh2, csc2), ffn_w_in[l], ffn_w_out[l]),
                            ln2_g[l], ln2_b[l])
    return x
```

```python
import functools
import math

import jax
import jax.numpy as jnp
from jax import lax
from jax.experimental import pallas as pl
from jax.experimental.pallas import tpu as pltpu

BF = jnp.bfloat16
F32 = jnp.float32

D_MODEL = 1024
N_LAYERS = 2
GRID_W = 64
TILE = 256
GLA_H, GLA_DK, GLA_DV, GLA_RANK, GLA_TAU = 4, 128, 256, 16, 16.0
GLA_CHUNK, GLA_SUB = 64, 16
DIFF_H, DIFF_DH = 8, 64
MLA_H, MLA_QR, MLA_KVR, MLA_NOPE, MLA_ROPE, MLA_DV = 8, 256, 128, 128, 64, 128
MLA_SCALE = (MLA_NOPE + MLA_ROPE) ** -0.5
ROPE_DIM, ROPE_BASE = 64, 10000.0
FFN_H = 2816
FFN_CHUNK = 704
DN_ALPHA = (2 * N_LAYERS) ** 0.25
EPS = 1e-6
ONES_ROWS = 16
VMEM_LIMIT = 56 * 1024 * 1024

_SIZES = (512, 512, 1024, 1024, 32, 1024, 1024, 1024, 256, 128, 64, 3072)
_OFF = [0]
for _s in _SIZES:
    _OFF.append(_OFF[-1] + _s)
(O_GQ, O_GK, O_GV, O_GR, O_GA, O_DQ, O_DK, O_DV, O_MQ, O_MKV, O_MKR, O_GATES, O_END) = _OFF

T_GQ, T_GK, T_GV, T_GR, T_DK, T_DKS, T_MQ, T_MKV, T_SMA, T_SMB, T_END = (
    0, 512, 1024, 2048, 3072, 4096, 5120, 5376, 5504, 5632, 5760)
F_DQ, F_DV, F_GATES, F_END = 0, 1024, 2048, 5120


def _cparams(sem):
    return pltpu.CompilerParams(dimension_semantics=sem, vmem_limit_bytes=VMEM_LIMIT)


def _resident(shape):
    nd = len(shape)
    return pl.BlockSpec(shape, lambda *_: (0,) * nd, pipeline_mode=pl.Buffered(1))


def _dot(a, b):
    return jnp.dot(a, b, preferred_element_type=F32)


def _dot_nt(a, b):
    return lax.dot_general(a, b, (((1,), (1,)), ((), ())), preferred_element_type=F32)


def _dot_tn(a, b):
    return lax.dot_general(a, b, (((0,), (0,)), ((), ())), preferred_element_type=F32)


def _ln_rows(x):
    mu = jnp.mean(x, axis=-1, keepdims=True)
    xc = x - mu
    var = jnp.mean(xc * xc, axis=-1, keepdims=True)
    return xc * lax.rsqrt(var + EPS)


def _sigmoid(x):
    return 1.0 / (1.0 + jnp.exp(-x))


def _silu(x):
    return x * _sigmoid(x)


def _mod_kernel(c_ref, w_ref, b_ref, o_ref):
    ca = _silu(c_ref[...]).astype(BF)
    o_ref[0] = _dot(ca, w_ref[0].astype(BF)) + b_ref[0]


def _modulation(c_all, w_mod, b_mod):
    nl, d, n6 = w_mod.shape
    r = c_all.shape[0]
    nblk = 1536
    return pl.pallas_call(
        _mod_kernel,
        grid=(nl, n6 // nblk),
        in_specs=[pl.BlockSpec((r, d), lambda l, j: (0, 0)),
                  pl.BlockSpec((1, d, nblk), lambda l, j: (l, 0, j)),
                  pl.BlockSpec((1, 1, nblk), lambda l, j: (l, 0, j))],
        out_specs=pl.BlockSpec((1, r, nblk), lambda l, j: (l, 0, j)),
        out_shape=jax.ShapeDtypeStruct((nl, r, n6), F32),
        compiler_params=_cparams(("arbitrary", "arbitrary")),
        name="modulation",
    )(c_all, w_mod, b_mod.reshape(nl, 1, n6))


def _proj_tok_kernel(x_ref, sh_ref, sc_ref, ct_ref, st_ref, w_ref, wa2_ref, ba_ref, qg_ref, kvg_ref, wk_ref,
                     gq_ref, gk_ref, gv_ref, gr_ref, dk_ref, la_ref, mk_ref, cq_ref, ckv_ref):
    h = (_ln_rows(x_ref[0]) * (1.0 + sc_ref[0]) + sh_ref[0]).astype(BF)

    def proj(a, b):
        return _dot(h, w_ref[:, a:b])

    gq_ref[0] = (proj(T_GQ, T_GK) * GLA_DK ** -0.5).astype(BF)
    gk_ref[0] = proj(T_GK, T_GV).astype(BF)
    gv_ref[0] = proj(T_GV, T_GR).astype(BF)
    gr_ref[0] = proj(T_GR, T_DK).astype(BF)
    ct = ct_ref[...]
    st = st_ref[...]
    for hh in range(DIFF_H):
        a = T_DK + 128 * hh
        dk_ref[0, :, 128 * hh:128 * (hh + 1)] = (proj(a, a + 128) * ct + proj(a + 1024, a + 1152) * st).astype(BF)
    mq = proj(T_MQ, T_MKV)
    cq = mq * lax.rsqrt(jnp.mean(mq * mq, axis=-1, keepdims=True) + EPS) * qg_ref[...]
    cq_ref[0] = cq.astype(BF)
    mkv = proj(T_MKV, T_SMA)
    ckv = (mkv * lax.rsqrt(jnp.mean(mkv * mkv, axis=-1, keepdims=True) + EPS) * kvg_ref[...]).astype(BF)
    ckv_ref[0] = ckv
    kn = _dot(ckv, wk_ref[...]).astype(BF)
    sma = proj(T_SMA, T_SMB)
    smb = proj(T_SMB, T_END)
    kr = (sma[:, 0:64] * ct[:, 0:64] + smb[:, 0:64] * st[:, 0:64]).astype(BF)
    for hh in range(MLA_H):
        mk_ref[0, hh, :, 0:128] = kn[:, 128 * hh:128 * (hh + 1)]
        mk_ref[0, hh, :, 128:192] = kr
    z = _dot(sma.astype(BF), wa2_ref[...]) + ba_ref[...]
    la_ref[0] = (jnp.minimum(z, 0.0) - jnp.log(1.0 + jnp.exp(-jnp.abs(z)))) * (1.0 / GLA_TAU)


def _proj_tok(xall, mod_l, ctok, stok, p, n_b, n_t, ltot):
    tile_spec = lambda c: pl.BlockSpec((1, TILE, c), lambda b, t: (b, t, 0))
    mrow = lambda b, t: jnp.where(t == n_t - 1, n_b, b)
    bf_out = lambda c: jax.ShapeDtypeStruct((n_b, ltot, c), BF)
    return pl.pallas_call(
        _proj_tok_kernel,
        grid=(n_b, n_t),
        in_specs=[tile_spec(D_MODEL),
                  pl.BlockSpec((1, 1, D_MODEL), lambda b, t: (mrow(b, t), 0, 0)),
                  pl.BlockSpec((1, 1, D_MODEL), lambda b, t: (mrow(b, t), 0, 1)),
                  pl.BlockSpec((TILE, 128), lambda b, t: (t, 0)),
                  pl.BlockSpec((TILE, 128), lambda b, t: (t, 0)),
                  _resident((D_MODEL, T_END)), _resident((128, 1024)), _resident((1, 1024)),
                  _resident((1, MLA_QR)), _resident((1, MLA_KVR)), _resident((MLA_KVR, 1024))],
        out_specs=[tile_spec(512), tile_spec(512), tile_spec(1024), tile_spec(1024), tile_spec(1024),
                   tile_spec(1024),
                   pl.BlockSpec((1, MLA_H, TILE, 192), lambda b, t: (b, 0, t, 0)),
                   tile_spec(MLA_QR), tile_spec(MLA_KVR)],
        out_shape=[bf_out(512), bf_out(512), bf_out(1024), bf_out(1024), bf_out(1024),
                   jax.ShapeDtypeStruct((n_b, ltot, 1024), F32),
                   jax.ShapeDtypeStruct((n_b, MLA_H, ltot, 192), BF),
                   bf_out(MLA_QR), bf_out(MLA_KVR)],
        compiler_params=_cparams(("arbitrary", "arbitrary")),
        name="proj_tok",
    )(xall, mod_l, mod_l, ctok, stok, p["w_tok"], p["wa2"], p["ba"], p["qg"], p["kvg"], p["wukv_k"])


def _rope_rows(x1, x2, cos, sin):
    return x1 * cos - x2 * sin, x1 * sin + x2 * cos


def _proj_feat_kernel(x_ref, sh_ref, sc_ref, cos_ref, sin_ref, cq_ref, ckv_ref, w_ref, wuq_ref, wv_ref,
                      dq_ref, dv_ref, g_ref, mq_ref, mv_ref):
    h = (_ln_rows(x_ref[0]) * (1.0 + sc_ref[0]) + sh_ref[0]).astype(BF)
    cos = cos_ref[0]
    sin = sin_ref[0]
    for g in range(2 * DIFF_H):
        r = _dot_nt(w_ref[F_DQ + 64 * g:F_DQ + 64 * (g + 1), :], h)
        o1, o2 = _rope_rows(r[0:32], r[32:64], cos, sin)
        dq_ref[0, 0, 64 * g:64 * g + 32, :] = o1.astype(BF)
        dq_ref[0, 0, 64 * g + 32:64 * (g + 1), :] = o2.astype(BF)
    dv_ref[0, 0] = _dot_nt(w_ref[F_DV:F_GATES, :], h).astype(BF)
    for j in range(3):
        a = F_GATES + 1024 * j
        g_ref[0, 0, 1024 * j:1024 * (j + 1), :] = _dot_nt(w_ref[a:a + 1024, :], h).astype(BF)
    cq = cq_ref[0]
    for hh in range(MLA_H):
        r = _dot_nt(wuq_ref[192 * hh:192 * (hh + 1), :], cq) * MLA_SCALE
        o1, o2 = _rope_rows(r[128:160], r[160:192], cos, sin)
        mq_ref[0, 0, 192 * hh:192 * hh + 128, :] = r[0:128].astype(BF)
        mq_ref[0, 0, 192 * hh + 128:192 * hh + 160, :] = o1.astype(BF)
        mq_ref[0, 0, 192 * hh + 160:192 * (hh + 1), :] = o2.astype(BF)
    mv_ref[0, 0] = _dot_nt(wv_ref[...], ckv_ref[0]).astype(BF)


def _proj_feat(xall, mod_l, cosT, sinT, cq, ckv, p, n_b, n_t):
    tile_spec = lambda c: pl.BlockSpec((1, TILE, c), lambda b, t: (b, t, 0))
    mrow = lambda b, t: jnp.where(t == n_t - 1, n_b, b)
    fm_spec = lambda c: pl.BlockSpec((1, 1, c, TILE), lambda b, t: (b, t, 0, 0))
    fm_out = lambda c: jax.ShapeDtypeStruct((n_b, n_t, c, TILE), BF)
    return pl.pallas_call(
        _proj_feat_kernel,
        grid=(n_b, n_t),
        in_specs=[tile_spec(D_MODEL),
                  pl.BlockSpec((1, 1, D_MODEL), lambda b, t: (mrow(b, t), 0, 0)),
                  pl.BlockSpec((1, 1, D_MODEL), lambda b, t: (mrow(b, t), 0, 1)),
                  pl.BlockSpec((1, 32, TILE), lambda b, t: (t, 0, 0)),
                  pl.BlockSpec((1, 32, TILE), lambda b, t: (t, 0, 0)),
                  tile_spec(MLA_QR), tile_spec(MLA_KVR),
                  _resident((F_END, D_MODEL)), _resident((MLA_H * 192, MLA_QR)), _resident((1024, MLA_KVR))],
        out_specs=[fm_spec(1024), fm_spec(1024), fm_spec(3072), fm_spec(MLA_H * 192), fm_spec(1024)],
        out_shape=[fm_out(1024), fm_out(1024), fm_out(3072), fm_out(MLA_H * 192), fm_out(1024)],
        compiler_params=_cparams(("arbitrary", "arbitrary")),
        name="proj_feat",
    )(xall, mod_l, mod_l, cosT, sinT, cq, ckv, p["w_featT"], p["wuqT"], p["wukv_vT"])


def _flash_T(k_ref, v_ref, q2, m_ref, acc_ref, j0, n_t):
    dv = v_ref.shape[2]
    m_ref[...] = jnp.full(m_ref.shape, -jnp.inf, F32)
    acc_ref[...] = jnp.zeros(acc_ref.shape, F32)
    ones = jnp.ones((ONES_ROWS, TILE), BF)

    def step(j, carry):
        k = k_ref[0, pl.ds(pl.multiple_of(j * TILE, TILE), TILE), :]
        s = _dot(k, q2)
        m_old = m_ref[...]
        m_new = jnp.maximum(m_old, jnp.max(s, axis=0, keepdims=True))
        alpha = jnp.exp(m_old - m_new)
        p = jnp.exp(s - m_new).astype(BF)
        vext = jnp.concatenate([v_ref[0, j], ones], axis=0)
        acc_ref[...] = acc_ref[...] * alpha + _dot(vext, p)
        m_ref[...] = m_new
        return carry

    lax.fori_loop(j0, n_t, step, 0)
    acc = acc_ref[...]
    return acc[0:dv] / acc[dv:dv + 1]


def _diff_attn_kernel(q_ref, k_ref, v_ref, lam_ref, g_ref, o_ref, q2_ref, m_ref, acc_ref, *, n_t, lam_init, ctx_q):
    qi = pl.program_id(2)
    q = q_ref[0, 0]
    zeros = jnp.zeros((DIFF_DH, TILE), BF)
    q2_ref[0:64, 0:TILE] = q[0:64]
    q2_ref[64:128, 0:TILE] = zeros
    q2_ref[0:64, TILE:2 * TILE] = zeros
    q2_ref[64:128, TILE:2 * TILE] = q[64:128]
    j0 = jnp.where(qi == n_t - 1, n_t - 1, 0) if ctx_q else 0
    o = _flash_T(k_ref, v_ref, q2_ref[...], m_ref, acc_ref, j0, n_t)
    dl = lam_ref[...]
    lam = (jnp.exp(jnp.sum(dl[0:1] * dl[1:2], axis=1, keepdims=True))
           - jnp.exp(jnp.sum(dl[2:3] * dl[3:4], axis=1, keepdims=True)) + lam_init)
    od = o[:, 0:TILE] - lam * o[:, TILE:2 * TILE]
    y = od * lax.rsqrt(jnp.mean(od * od, axis=0, keepdims=True) + EPS) * g_ref[...] * (1.0 - lam_init)
    o_ref[0, 0] = y.astype(BF)


def _diff_attn(dqT, dk, dvT, dlam, gcol, n_b, n_t, n_q, lam_init):
    ltot = n_t * TILE
    kern = functools.partial(_diff_attn_kernel, n_t=n_t, lam_init=lam_init, ctx_q=(n_q == n_t))
    return pl.pallas_call(
        kern,
        grid=(n_b, DIFF_H, n_q),
        in_specs=[pl.BlockSpec((1, 1, 128, TILE), lambda b, h, q: (b, q, h, 0)),
                  pl.BlockSpec((1, ltot, 128), lambda b, h, q: (b, 0, h)),
                  pl.BlockSpec((1, n_t, 128, TILE), lambda b, h, q: (b, 0, h, 0)),
                  pl.BlockSpec((4, DIFF_DH), lambda b, h, q: (0, 0)),
                  pl.BlockSpec((128, TILE), lambda b, h, q: (0, 0))],
        out_specs=pl.BlockSpec((1, 1, 128, TILE), lambda b, h, q: (b, q, h, 0)),
        out_shape=jax.ShapeDtypeStruct((n_b, n_q, 1024, TILE), BF),
        scratch_shapes=[pltpu.VMEM((128, 2 * TILE), BF), pltpu.VMEM((1, 2 * TILE), F32),
                        pltpu.VMEM((128 + ONES_ROWS, 2 * TILE), F32)],
        compiler_params=_cparams(("arbitrary", "arbitrary", "arbitrary")),
        name="diff_attn",
    )(dqT, dk, dvT, dlam, gcol)


def _mla_attn_kernel(q_ref, k_ref, v_ref, o_ref, m_ref, acc_ref, *, n_t, ctx_q):
    qi = pl.program_id(2)
    j0 = jnp.where(qi == n_t - 1, n_t - 1, 0) if ctx_q else 0
    o = _flash_T(k_ref.at[0], v_ref, q_ref[0, 0], m_ref, acc_ref, j0, n_t)
    o_ref[0, 0] = o.astype(BF)


def _mla_attn(mqT, mk, mvT, n_b, n_t, n_q):
    ltot = n_t * TILE
    kern = functools.partial(_mla_attn_kernel, n_t=n_t, ctx_q=(n_q == n_t))
    return pl.pallas_call(
        kern,
        grid=(n_b, MLA_H, n_q),
        in_specs=[pl.BlockSpec((1, 1, 192, TILE), lambda b, h, q: (b, q, h, 0)),
                  pl.BlockSpec((1, 1, ltot, 192), lambda b, h, q: (b, h, 0, 0)),
                  pl.BlockSpec((1, n_t, 128, TILE), lambda b, h, q: (b, 0, h, 0))],
        out_specs=pl.BlockSpec((1, 1, 128, TILE), lambda b, h, q: (b, q, h, 0)),
        out_shape=jax.ShapeDtypeStruct((n_b, n_q, 1024, TILE), BF),
        scratch_shapes=[pltpu.VMEM((1, TILE), F32), pltpu.VMEM((128 + ONES_ROWS, TILE), F32)],
        compiler_params=_cparams(("arbitrary", "arbitrary", "arbitrary")),
        name="mla_attn",
    )(mqT, mk, mvT)


def _gla_tile(q_ref, k_ref, v_ref, la_ref, tri_ref, st_ref, reverse):
    g = la_ref[0]
    g1 = g.astype(BF)
    r1 = g - g1.astype(F32)
    g2 = r1.astype(BF)
    g3 = (r1 - g2.astype(F32)).astype(BF)
    tri = tri_ref[...]
    bcum = _dot(tri, g1) + _dot(tri, g2) + _dot(tri, g3)
    n_chunk = TILE // GLA_CHUNK
    n_sub = GLA_CHUNK // GLA_SUB
    rows64 = lax.broadcasted_iota(jnp.int32, (GLA_CHUNK, GLA_DK), 0)
    r16 = lax.broadcasted_iota(jnp.int32, (GLA_SUB, GLA_CHUNK), 0)
    c16 = lax.broadcasted_iota(jnp.int32, (GLA_SUB, GLA_CHUNK), 1)
    outs = []
    for hh in range(GLA_H):
        kc0, kc1 = GLA_DK * hh, GLA_DK * (hh + 1)
        o_chunks = [None] * n_chunk
        for c in (range(n_chunk - 1, -1, -1) if reverse else range(n_chunk)):
            r0 = GLA_CHUNK * c
            bc = bcum[r0:r0 + GLA_CHUNK, kc0:kc1]
            qc = q_ref[0, r0:r0 + GLA_CHUNK, kc0:kc1].astype(F32)
            kc = k_ref[0, r0:r0 + GLA_CHUNK, kc0:kc1].astype(F32)
            vc = v_ref[0, r0:r0 + GLA_CHUNK, GLA_DV * hh:GLA_DV * (hh + 1)]
            b_tot = bc[0:1] if reverse else bc[GLA_CHUNK - 1:GLA_CHUNK]
            a_rows = []
            for i in range(n_sub):
                s0 = GLA_SUB * i
                ref_row = bc[s0 + GLA_SUB - 1:s0 + GLA_SUB] if reverse else bc[s0:s0 + 1]
                qs = (qc[s0:s0 + GLA_SUB] * jnp.exp(bc[s0:s0 + GLA_SUB] - ref_row)).astype(BF)
                valid = (rows64 >= s0) if reverse else (rows64 < s0 + GLA_SUB)
                ks = (kc * jnp.exp(jnp.where(valid, ref_row - bc, -jnp.inf))).astype(BF)
                a = _dot_nt(qs, ks)
                keep = (c16 > r16 + s0) if reverse else (c16 <= r16 + s0)
                a_rows.append(jnp.where(keep, a, 0.0))
            a_mat = jnp.concatenate(a_rows, axis=0).astype(BF)
            st = st_ref[hh]
            q_in = (qc * jnp.exp(bc)).astype(BF)
            o_chunks[c] = _dot(a_mat, vc) + _dot_nt(q_in, st.astype(BF))
            k_d = (kc * jnp.exp(b_tot - bc)).astype(BF)
            st_ref[hh] = st * jnp.exp(b_tot) + _dot_tn(vc, k_d)
        outs.append(jnp.concatenate(o_chunks, axis=0))
    return jnp.concatenate(outs, axis=1)


def _gla_fwd_kernel(q_ref, k_ref, v_ref, la_ref, tri_ref, o_ref, st_ref):
    @pl.when(pl.program_id(1) == 0)
    def _():
        st_ref[...] = jnp.zeros(st_ref.shape, F32)
    o_ref[0] = _gla_tile(q_ref, k_ref, v_ref, la_ref, tri_ref, st_ref, False)


def _gla_bwd_kernel(q_ref, k_ref, v_ref, la_ref, tri_ref, of_ref, r_ref, g_ref, y_ref, st_ref):
    @pl.when(pl.program_id(1) == 0)
    def _():
        st_ref[...] = jnp.zeros(st_ref.shape, F32)
    o = of_ref[0] + _gla_tile(q_ref, k_ref, v_ref, la_ref, tri_ref, st_ref, True)
    gn = g_ref[...]
    for hh in range(GLA_H):
        sl = slice(GLA_DV * hh, GLA_DV * (hh + 1))
        oh = o[:, sl]
        yh = oh * lax.rsqrt(jnp.mean(oh * oh, axis=-1, keepdims=True) + EPS) * gn
        y_ref[0, :, sl] = (yh * _silu(r_ref[0, :, sl].astype(F32))).astype(BF)


def _gla(gq, gk, gv, la, gr, tri_lo, tri_up, gnorm, n_b, n_t):
    ltot = n_t * TILE
    t_fwd = lambda s: (s + n_t - 1) % n_t
    t_bwd = lambda s: jnp.where(s == 0, n_t - 1, n_t - 1 - s)

    def specs(tmap, dirn):
        ts = lambda c, cb=0: pl.BlockSpec((1, TILE, c), lambda b, s: (b, tmap(s), cb))
        return [ts(512), ts(512), ts(1024), ts(512, dirn), pl.BlockSpec((TILE, TILE), lambda b, s: (0, 0))]

    o_f = pl.pallas_call(
        _gla_fwd_kernel,
        grid=(n_b, n_t),
        in_specs=specs(t_fwd, 0),
        out_specs=pl.BlockSpec((1, TILE, 1024), lambda b, s: (b, t_fwd(s), 0)),
        out_shape=jax.ShapeDtypeStruct((n_b, ltot, 1024), F32),
        scratch_shapes=[pltpu.VMEM((GLA_H, GLA_DV, GLA_DK), F32)],
        compiler_params=_cparams(("arbitrary", "arbitrary")),
        name="gla_fwd",
    )(gq, gk, gv, la, tri_lo)
    tsb = lambda c: pl.BlockSpec((1, TILE, c), lambda b, s: (b, t_bwd(s), 0))
    return pl.pallas_call(
        _gla_bwd_kernel,
        grid=(n_b, n_t),
        in_specs=specs(t_bwd, 1) + [tsb(1024), tsb(1024), pl.BlockSpec((1, GLA_DV), lambda b, s: (0, 0))],
        out_specs=tsb(1024),
        out_shape=jax.ShapeDtypeStruct((n_b, ltot, 1024), BF),
        scratch_shapes=[pltpu.VMEM((GLA_H, GLA_DV, GLA_DK), F32)],
        compiler_params=_cparams(("arbitrary", "arbitrary")),
        name="gla_bwd",
    )(gq, gk, gv, la, tri_up, o_f, gr, gnorm)


def _merge_kernel(x_ref, ya_ref, yb_ref, yc_ref, g_ref, g1_ref, wb_ref, wo_ref, lg_ref, lb_ref, o_ref):
    zt = (_sigmoid(g_ref[0, 0, 0:1024, :].astype(F32)) * _dot_nt(wb_ref[0], ya_ref[0])
          + _sigmoid(g_ref[0, 0, 1024:2048, :].astype(F32)) * _dot(wb_ref[1], yb_ref[0, 0])
          + _sigmoid(g_ref[0, 0, 2048:3072, :].astype(F32)) * _dot(wb_ref[2], yc_ref[0, 0]))
    u = _dot(wo_ref[...], zt.astype(BF)).T
    y = _ln_rows(DN_ALPHA * x_ref[0] + g1_ref[0] * u)
    o_ref[0] = y * lg_ref[...] + lb_ref[...]


def _merge(xall, ya, ybT, ycT, gatesT, mod_l, p, n_b, n_t, n_q):
    ltot = n_t * TILE
    tile_spec = lambda c: pl.BlockSpec((1, TILE, c), lambda b, t: (b, t, 0))
    fm_spec = lambda c: pl.BlockSpec((1, 1, c, TILE), lambda b, t: (b, t, 0, 0))
    mrow = lambda b, t: jnp.where(t == n_t - 1, n_b, b)
    return pl.pallas_call(
        _merge_kernel,
        grid=(n_b, n_q),
        in_specs=[tile_spec(D_MODEL), tile_spec(1024), fm_spec(1024), fm_spec(1024), fm_spec(3072),
                  pl.BlockSpec((1, 1, D_MODEL), lambda b, t: (mrow(b, t), 0, 2)),
                  _resident((3, D_MODEL, D_MODEL)), _resident((D_MODEL, D_MODEL)),
                  _resident((1, D_MODEL)), _resident((1, D_MODEL))],
        out_specs=tile_spec(D_MODEL),
        out_shape=jax.ShapeDtypeStruct((n_b, n_q * TILE, D_MODEL), F32),
        compiler_params=_cparams(("arbitrary", "arbitrary")),
        name="merge",
    )(xall, ya, ybT, ycT, gatesT, mod_l, p["wbT"], p["woT"], p["ln1_g"], p["ln1_b"])


def _ffn_kernel(x_ref, sh_ref, sc_ref, g2_ref, wi_ref, wo_ref, lg_ref, lb_ref, o_ref):
    x = x_ref[0]
    h = (_ln_rows(x) * (1.0 + sc_ref[0]) + sh_ref[0]).astype(BF)
    acc = jnp.zeros((TILE, D_MODEL), F32)
    for j in range(FFN_H // FFN_CHUNK):
        a = FFN_CHUNK * j
        gate = _dot(h, wi_ref[:, a:a + FFN_CHUNK])
        up = _dot(h, wi_ref[:, FFN_H + a:FFN_H + a + FFN_CHUNK])
        acc = acc + _dot((_silu(gate) * up).astype(BF), wo_ref[a:a + FFN_CHUNK, :])
    y = _ln_rows(DN_ALPHA * x + g2_ref[0] * acc)
    o_ref[0] = y * lg_ref[...] + lb_ref[...]


def _ffn(x1, mod_l, p, n_b, n_t, n_q):
    tile_spec = lambda c: pl.BlockSpec((1, TILE, c), lambda b, t: (b, t, 0))
    mrow = lambda b, t: jnp.where(t == n_t - 1, n_b, b)
    mspec = lambda j: pl.BlockSpec((1, 1, D_MODEL), lambda b, t: (mrow(b, t), 0, j))
    return pl.pallas_call(
        _ffn_kernel,
        grid=(n_b, n_q),
        in_specs=[tile_spec(D_MODEL), mspec(3), mspec(4), mspec(5),
                  _resident((D_MODEL, 2 * FFN_H)), _resident((FFN_H, D_MODEL)),
                  _resident((1, D_MODEL)), _resident((1, D_MODEL))],
        out_specs=tile_spec(D_MODEL),
        out_shape=jax.ShapeDtypeStruct((n_b, n_q * TILE, D_MODEL), F32),
        compiler_params=_cparams(("arbitrary", "arbitrary")),
        name="ffn",
    )(x1, mod_l, mod_l, mod_l, p["ffn_wi"], p["ffn_wo"], p["ln2_g"], p["ln2_b"])


def _swap32(w):
    k, n = w.shape
    return w.reshape(k, n // 64, 2, 32)[:, :, ::-1, :].reshape(k, n)


def _prep_layer(l, w_in, gla_w_a2, gla_b_a, mla_q_norm_g, mla_kv_norm_g, mla_w_uq, mla_w_ukv, w_branch, w_out,
                ln1_g, ln1_b, ffn_w_in, ffn_w_out, ln2_g, ln2_b):
    w = w_in[l]
    seg = lambda a, b: w[:, a:b]
    zc = lambda n: jnp.zeros((D_MODEL, n), F32)
    dk_w, mkr_w = seg(O_DK, O_DV), seg(O_MKR, O_GATES)
    w_tok = jnp.concatenate([
        seg(O_GQ, O_GK), seg(O_GK, O_GV), seg(O_GV, O_GR), seg(O_GR, O_GA), dk_w, _swap32(dk_w),
        seg(O_MQ, O_MKV), seg(O_MKV, O_MKR),
        mkr_w, seg(O_GA, O_DQ), zc(32), _swap32(mkr_w), zc(64)], axis=1).astype(BF)
    w_featT = jnp.concatenate([seg(O_DQ, O_DK) * DIFF_DH ** -0.5, seg(O_DV, O_MQ), seg(O_GATES, O_END)],
                              axis=1).T.astype(BF)
    wa2 = jnp.zeros((128, 1024), F32)
    wa2 = wa2.at[64:80, 0:512].set(gla_w_a2[l, 0]).at[80:96, 512:1024].set(gla_w_a2[l, 1]).astype(BF)
    ukv = mla_w_ukv[l].reshape(MLA_KVR, MLA_H, MLA_NOPE + MLA_DV)
    return dict(
        w_tok=w_tok, w_featT=w_featT, wa2=wa2, ba=gla_b_a[l].reshape(1, 1024),
        qg=mla_q_norm_g[l].reshape(1, MLA_QR), kvg=mla_kv_norm_g[l].reshape(1, MLA_KVR),
        wukv_k=ukv[:, :, :MLA_NOPE].reshape(MLA_KVR, 1024).astype(BF),
        wukv_vT=ukv[:, :, MLA_NOPE:].reshape(MLA_KVR, 1024).T.astype(BF),
        wuqT=mla_w_uq[l].T.astype(BF),
        wbT=jnp.swapaxes(w_branch[l], 1, 2).astype(BF), woT=w_out[l].T.astype(BF),
        ln1_g=ln1_g[l].reshape(1, D_MODEL), ln1_b=ln1_b[l].reshape(1, D_MODEL),
        ffn_wi=ffn_w_in[l].astype(BF), ffn_wo=ffn_w_out[l].astype(BF),
        ln2_g=ln2_g[l].reshape(1, D_MODEL), ln2_b=ln2_b[l].reshape(1, D_MODEL))


def _rope_tables(l_lat, l_ctx):
    rows = l_lat // GRID_W
    pos_row = jnp.broadcast_to(jnp.arange(rows, dtype=F32)[:, None], (rows, GRID_W)).reshape(l_lat)
    pos_col = jnp.broadcast_to(jnp.arange(GRID_W, dtype=F32)[None, :], (rows, GRID_W)).reshape(l_lat)
    d_axis = ROPE_DIM // 2
    inv = ROPE_BASE ** (-jnp.arange(0, d_axis, 2, dtype=F32) / d_axis)
    ang = jnp.concatenate([pos_row[:, None] * inv, pos_col[:, None] * inv], axis=-1)
    cos = jnp.concatenate([jnp.cos(ang), jnp.ones((l_ctx, 32), F32)], axis=0)
    sin = jnp.concatenate([jnp.sin(ang), jnp.zeros((l_ctx, 32), F32)], axis=0)
    ctok = jnp.tile(cos, (1, 4))
    stok = jnp.tile(jnp.concatenate([-sin, sin], axis=1), (1, 2))
    n_t = (l_lat + l_ctx) // TILE
    to_fm = lambda a: a.T.reshape(32, n_t, TILE).transpose(1, 0, 2)
    return ctok, stok, to_fm(cos), to_fm(sin)


def kernel(x, c, ctx, c_ctx, w_mod, b_mod, w_in, gla_w_a2, gla_b_a, gla_norm_g, diff_lam, diff_norm_g,
           mla_q_norm_g, mla_kv_norm_g, mla_w_uq, mla_w_ukv, w_branch, w_out, ln1_g, ln1_b, ffn_w_in, ffn_w_out,
           ln2_g, ln2_b):
    n_b, l_lat, _ = x.shape
    l_ctx = ctx.shape[1]
    assert l_lat % TILE == 0 and l_ctx == TILE and n_b + 1 <= 16
    ltot = l_lat + l_ctx
    n_t = ltot // TILE
    ctok, stok, cosT, sinT = _rope_tables(l_lat, l_ctx)
    c_all = jnp.zeros((16, D_MODEL), F32).at[:n_b].set(c).at[n_b].set(c_ctx)
    mod = _modulation(c_all, w_mod, b_mod)
    ii = lax.broadcasted_iota(jnp.int32, (TILE, TILE), 0)
    jj = lax.broadcasted_iota(jnp.int32, (TILE, TILE), 1)
    same = (ii // GLA_CHUNK) == (jj // GLA_CHUNK)
    tri_lo = (same & (jj <= ii)).astype(BF)
    tri_up = (same & (jj >= ii)).astype(BF)
    xall = jnp.concatenate([x, ctx], axis=1)
    for l in range(N_LAYERS):
        last = l == N_LAYERS - 1
        n_q = n_t - 1 if last else n_t
        lam_init = 0.8 - 0.6 * math.exp(-0.3 * l)
        p = _prep_layer(l, w_in, gla_w_a2, gla_b_a, mla_q_norm_g, mla_kv_norm_g, mla_w_uq, mla_w_ukv, w_branch,
                        w_out, ln1_g, ln1_b, ffn_w_in, ffn_w_out, ln2_g, ln2_b)
        mod_l = mod[l].reshape(16, 1, 6 * D_MODEL)
        gq, gk, gv, gr, dk, la, mk, cq, ckv = _proj_tok(xall, mod_l, ctok, stok, p, n_b, n_t, ltot)
        dqT, dvT, gatesT, mqT, mvT = _proj_feat(xall, mod_l, cosT, sinT, cq, ckv, p, n_b, n_t)
        ya = _gla(gq, gk, gv, la, gr, tri_lo, tri_up, gla_norm_g[l].reshape(1, GLA_DV), n_b, n_t)
        gcol = jnp.broadcast_to(diff_norm_g[l].reshape(128, 1), (128, TILE))
        ybT = _diff_attn(dqT, dk, dvT, diff_lam[l], gcol, n_b, n_t, n_q, lam_init)
        ycT = _mla_attn(mqT, mk, mvT, n_b, n_t, n_q)
        x1 = _merge(xall, ya, ybT, ycT, gatesT, mod_l, p, n_b, n_t, n_q)
        xall = _ffn(x1, mod_l, p, n_b, n_t, n_q)
    return xall
```

```python
import functools
import math

import jax
import jax.numpy as jnp
from jax import lax
from jax.experimental import pallas as pl
from jax.experimental.pallas import tpu as pltpu

BF = jnp.bfloat16
F32 = jnp.float32

D_MODEL = 1024
N_LAYERS = 2
GRID_W = 64
TILE = 256
GLA_H, GLA_DK, GLA_DV, GLA_RANK, GLA_TAU = 4, 128, 256, 16, 16.0
GLA_CHUNK, GLA_SUB = 64, 16
DIFF_H, DIFF_DH = 8, 64
MLA_H, MLA_QR, MLA_KVR, MLA_NOPE, MLA_ROPE, MLA_DV = 8, 256, 128, 128, 64, 128
MLA_SCALE = (MLA_NOPE + MLA_ROPE) ** -0.5
ROPE_DIM, ROPE_BASE = 64, 10000.0
FFN_H = 2816
FFN_CHUNK = 704
DN_ALPHA = (2 * N_LAYERS) ** 0.25
EPS = 1e-6
DIFF_QS, MLA_QS = 2, 4
KEY_TILES = 2
ONES_ROWS = 16
VMEM_LIMIT = 56 * 1024 * 1024

_SIZES = (512, 512, 1024, 1024, 32, 1024, 1024, 1024, 256, 128, 64, 3072)
_OFF = [0]
for _s in _SIZES:
    _OFF.append(_OFF[-1] + _s)
(O_GQ, O_GK, O_GV, O_GR, O_GA, O_DQ, O_DK, O_DV, O_MQ, O_MKV, O_MKR, O_GATES, O_END) = _OFF

T_GQ, T_GK, T_GV, T_GR, T_DK, T_DKS, T_MQ, T_MKV, T_SMA, T_SMB, T_END = (
    0, 512, 1024, 2048, 3072, 4096, 5120, 5376, 5504, 5632, 5760)
F_DQ, F_DV, F_GATES, F_END = 0, 1024, 2048, 5120


def _cparams(sem):
    return pltpu.CompilerParams(dimension_semantics=sem, vmem_limit_bytes=VMEM_LIMIT)


def _resident(shape):
    nd = len(shape)
    return pl.BlockSpec(shape, lambda *_: (0,) * nd, pipeline_mode=pl.Buffered(1))


def _dot(a, b):
    return jnp.dot(a, b, preferred_element_type=F32)


def _dot_nt(a, b):
    return lax.dot_general(a, b, (((1,), (1,)), ((), ())), preferred_element_type=F32)


def _dot_tn(a, b):
    return lax.dot_general(a, b, (((0,), (0,)), ((), ())), preferred_element_type=F32)


def _ln_rows(x):
    mu = jnp.mean(x, axis=-1, keepdims=True)
    xc = x - mu
    var = jnp.mean(xc * xc, axis=-1, keepdims=True)
    return xc * lax.rsqrt(var + EPS)


def _sigmoid(x):
    return 1.0 / (1.0 + jnp.exp(-x))


def _silu(x):
    return x * _sigmoid(x)


def _mod_kernel(c_ref, w_ref, b_ref, o_ref):
    ca = _silu(c_ref[...]).astype(BF)
    o_ref[0] = _dot(ca, w_ref[0].astype(BF)) + b_ref[0]


def _modulation(c_all, w_mod, b_mod):
    nl, d, n6 = w_mod.shape
    r = c_all.shape[0]
    nblk = 1536
    return pl.pallas_call(
        _mod_kernel,
        grid=(nl, n6 // nblk),
        in_specs=[pl.BlockSpec((r, d), lambda l, j: (0, 0)),
                  pl.BlockSpec((1, d, nblk), lambda l, j: (l, 0, j)),
                  pl.BlockSpec((1, 1, nblk), lambda l, j: (l, 0, j))],
        out_specs=pl.BlockSpec((1, r, nblk), lambda l, j: (l, 0, j)),
        out_shape=jax.ShapeDtypeStruct((nl, r, n6), F32),
        compiler_params=_cparams(("arbitrary", "arbitrary")),
        name="modulation",
    )(c_all, w_mod, b_mod.reshape(nl, 1, n6))


def _proj_tok_kernel(x_ref, sh_ref, sc_ref, ct_ref, st_ref, w_ref, wa2_ref, ba_ref, qg_ref, kvg_ref, wk_ref,
                     gq_ref, gk_ref, gv_ref, gr_ref, dk_ref, la_ref, mk_ref, cq_ref, ckv_ref):
    h = (_ln_rows(x_ref[0]) * (1.0 + sc_ref[0]) + sh_ref[0]).astype(BF)

    def proj(a, b):
        return _dot(h, w_ref[:, a:b])

    gq_ref[0] = (proj(T_GQ, T_GK) * GLA_DK ** -0.5).astype(BF)
    gk_ref[0] = proj(T_GK, T_GV).astype(BF)
    gv_ref[0] = proj(T_GV, T_GR).astype(BF)
    gr_ref[0] = proj(T_GR, T_DK).astype(BF)
    ct = ct_ref[...]
    st = st_ref[...]
    for hh in range(DIFF_H):
        a = T_DK + 128 * hh
        dk_ref[0, :, 128 * hh:128 * (hh + 1)] = (proj(a, a + 128) * ct + proj(a + 1024, a + 1152) * st).astype(BF)
    mq = proj(T_MQ, T_MKV)
    cq = mq * lax.rsqrt(jnp.mean(mq * mq, axis=-1, keepdims=True) + EPS) * qg_ref[...]
    cq_ref[0] = cq.astype(BF)
    mkv = proj(T_MKV, T_SMA)
    ckv = (mkv * lax.rsqrt(jnp.mean(mkv * mkv, axis=-1, keepdims=True) + EPS) * kvg_ref[...]).astype(BF)
    ckv_ref[0] = ckv
    kn = _dot(ckv, wk_ref[...]).astype(BF)
    sma = proj(T_SMA, T_SMB)
    smb = proj(T_SMB, T_END)
    kr = (sma[:, 0:64] * ct[:, 0:64] + smb[:, 0:64] * st[:, 0:64]).astype(BF)
    for hh in range(MLA_H):
        mk_ref[0, hh, :, 0:128] = kn[:, 128 * hh:128 * (hh + 1)]
        mk_ref[0, hh, :, 128:192] = kr
    z = _dot(sma.astype(BF), wa2_ref[...]) + ba_ref[...]
    la_ref[0] = (jnp.minimum(z, 0.0) - jnp.log(1.0 + jnp.exp(-jnp.abs(z)))) * (1.0 / GLA_TAU)


def _proj_tok(xall, mod_l, ctok, stok, p, n_b, n_t, ltot):
    tile_spec = lambda c: pl.BlockSpec((1, TILE, c), lambda b, t: (b, t, 0))
    mrow = lambda b, t: jnp.where(t == n_t - 1, n_b, b)
    bf_out = lambda c: jax.ShapeDtypeStruct((n_b, ltot, c), BF)
    return pl.pallas_call(
        _proj_tok_kernel,
        grid=(n_b, n_t),
        in_specs=[tile_spec(D_MODEL),
                  pl.BlockSpec((1, 1, D_MODEL), lambda b, t: (mrow(b, t), 0, 0)),
                  pl.BlockSpec((1, 1, D_MODEL), lambda b, t: (mrow(b, t), 0, 1)),
                  pl.BlockSpec((TILE, 128), lambda b, t: (t, 0)),
                  pl.BlockSpec((TILE, 128), lambda b, t: (t, 0)),
                  _resident((D_MODEL, T_END)), _resident((128, 1024)), _resident((1, 1024)),
                  _resident((1, MLA_QR)), _resident((1, MLA_KVR)), _resident((MLA_KVR, 1024))],
        out_specs=[tile_spec(512), tile_spec(512), tile_spec(1024), tile_spec(1024), tile_spec(1024),
                   tile_spec(1024),
                   pl.BlockSpec((1, MLA_H, TILE, 192), lambda b, t: (b, 0, t, 0)),
                   tile_spec(MLA_QR), tile_spec(MLA_KVR)],
        out_shape=[bf_out(512), bf_out(512), bf_out(1024), bf_out(1024), bf_out(1024),
                   jax.ShapeDtypeStruct((n_b, ltot, 1024), F32),
                   jax.ShapeDtypeStruct((n_b, MLA_H, ltot, 192), BF),
                   bf_out(MLA_QR), bf_out(MLA_KVR)],
        compiler_params=_cparams(("arbitrary", "arbitrary")),
        name="proj_tok",
    )(xall, mod_l, mod_l, ctok, stok, p["w_tok"], p["wa2"], p["ba"], p["qg"], p["kvg"], p["wukv_k"])


def _rope_rows(x1, x2, cos, sin):
    return x1 * cos - x2 * sin, x1 * sin + x2 * cos


def _proj_feat_kernel(x_ref, sh_ref, sc_ref, cos_ref, sin_ref, cq_ref, ckv_ref, w_ref, wuq_ref, wv_ref,
                      dq_ref, dv_ref, g_ref, mq_ref, mv_ref):
    h = (_ln_rows(x_ref[0]) * (1.0 + sc_ref[0]) + sh_ref[0]).astype(BF)
    cos = cos_ref[0]
    sin = sin_ref[0]
    for g in range(2 * DIFF_H):
        r = _dot_nt(w_ref[F_DQ + 64 * g:F_DQ + 64 * (g + 1), :], h)
        o1, o2 = _rope_rows(r[0:32], r[32:64], cos, sin)
        dq_ref[0, 0, 64 * g:64 * g + 32, :] = o1.astype(BF)
        dq_ref[0, 0, 64 * g + 32:64 * (g + 1), :] = o2.astype(BF)
    dv_ref[0, 0] = _dot_nt(w_ref[F_DV:F_GATES, :], h).astype(BF)
    for j in range(3):
        a = F_GATES + 1024 * j
        g_ref[0, 0, 1024 * j:1024 * (j + 1), :] = _dot_nt(w_ref[a:a + 1024, :], h).astype(BF)
    cq = cq_ref[0]
    for hh in range(MLA_H):
        r = _dot_nt(wuq_ref[192 * hh:192 * (hh + 1), :], cq) * MLA_SCALE
        o1, o2 = _rope_rows(r[128:160], r[160:192], cos, sin)
        mq_ref[0, 0, 192 * hh:192 * hh + 128, :] = r[0:128].astype(BF)
        mq_ref[0, 0, 192 * hh + 128:192 * hh + 160, :] = o1.astype(BF)
        mq_ref[0, 0, 192 * hh + 160:192 * (hh + 1), :] = o2.astype(BF)
    mv_ref[0, 0] = _dot_nt(wv_ref[...], ckv_ref[0]).astype(BF)


def _proj_feat(xall, mod_l, cosT, sinT, cq, ckv, p, n_b, n_t):
    tile_spec = lambda c: pl.BlockSpec((1, TILE, c), lambda b, t: (b, t, 0))
    mrow = lambda b, t: jnp.where(t == n_t - 1, n_b, b)
    fm_spec = lambda c: pl.BlockSpec((1, 1, c, TILE), lambda b, t: (b, t, 0, 0))
    fm_out = lambda c: jax.ShapeDtypeStruct((n_b, n_t, c, TILE), BF)
    return pl.pallas_call(
        _proj_feat_kernel,
        grid=(n_b, n_t),
        in_specs=[tile_spec(D_MODEL),
                  pl.BlockSpec((1, 1, D_MODEL), lambda b, t: (mrow(b, t), 0, 0)),
                  pl.BlockSpec((1, 1, D_MODEL), lambda b, t: (mrow(b, t), 0, 1)),
                  pl.BlockSpec((1, 32, TILE), lambda b, t: (t, 0, 0)),
                  pl.BlockSpec((1, 32, TILE), lambda b, t: (t, 0, 0)),
                  tile_spec(MLA_QR), tile_spec(MLA_KVR),
                  _resident((F_END, D_MODEL)), _resident((MLA_H * 192, MLA_QR)), _resident((1024, MLA_KVR))],
        out_specs=[fm_spec(1024), fm_spec(1024), fm_spec(3072), fm_spec(MLA_H * 192), fm_spec(1024)],
        out_shape=[fm_out(1024), fm_out(1024), fm_out(3072), fm_out(MLA_H * 192), fm_out(1024)],
        compiler_params=_cparams(("arbitrary", "arbitrary")),
        name="proj_feat",
    )(xall, mod_l, mod_l, cosT, sinT, cq, ckv, p["w_featT"], p["wuqT"], p["wukv_vT"])


def _flash_T(k_ref, v_ref, q2, m_ref, acc_ref, n_lat, latent_keys):
    dv = v_ref.shape[2]
    m_ref[...] = jnp.full(m_ref.shape, -jnp.inf, F32)
    acc_ref[...] = jnp.zeros(acc_ref.shape, F32)

    def step(tile0, ntiles):
        rows = ntiles * TILE
        k = k_ref[0, pl.ds(pl.multiple_of(tile0 * TILE, TILE), rows), :]
        s = _dot(k, q2)
        m_old = m_ref[...]
        m_new = jnp.maximum(m_old, jnp.max(s, axis=0, keepdims=True))
        alpha = jnp.exp(m_old - m_new)
        p = jnp.exp(s - m_new).astype(BF)
        v = jnp.concatenate([v_ref[0, tile0 + i] for i in range(ntiles)], axis=1)
        vext = jnp.concatenate([v, jnp.ones((ONES_ROWS, rows), BF)], axis=0)
        acc_ref[...] = acc_ref[...] * alpha + _dot(vext, p)
        m_ref[...] = m_new

    if latent_keys:
        def body(j, carry):
            step(j * KEY_TILES, KEY_TILES)
            return carry
        lax.fori_loop(0, n_lat // KEY_TILES, body, 0, unroll=2)
    step(n_lat if latent_keys else 0, 1)
    acc = acc_ref[...]
    return acc[0:dv] / acc[dv:dv + 1]


def _attn_call(kern, name, q_rows, k_spec_fn, n_b, n_heads, n_t, qs, scratch, qT, k, vT, extra, extra_specs,
               ctx_into=None):
    n_lat = n_t - 1
    if ctx_into is None:
        grid = (n_b, n_heads, n_lat // qs)
        q_spec = pl.BlockSpec((1, qs, q_rows, TILE), lambda b, h, q: (b, q, h, 0))
        k_spec = k_spec_fn(n_t * TILE, 0)
        v_spec = pl.BlockSpec((1, n_t, 128, TILE), lambda b, h, q: (b, 0, h, 0))
        o_spec = pl.BlockSpec((1, qs, 128, TILE), lambda b, h, q: (b, q, h, 0))
        alias_in, alias_specs, aliases = [], [], {}
    else:
        grid = (n_b, n_heads, 1)
        q_spec = pl.BlockSpec((1, 1, q_rows, TILE), lambda b, h, q: (b, n_lat, h, 0))
        k_spec = k_spec_fn(TILE, n_lat)
        v_spec = pl.BlockSpec((1, 1, 128, TILE), lambda b, h, q: (b, n_lat, h, 0))
        o_spec = pl.BlockSpec((1, 1, 128, TILE), lambda b, h, q: (b, n_lat, h, 0))
        alias_in, alias_specs = [ctx_into], [pl.BlockSpec(memory_space=pl.ANY)]
        aliases = {3 + len(extra): 0}
    return pl.pallas_call(
        kern,
        grid=grid,
        in_specs=[q_spec, k_spec, v_spec] + extra_specs + alias_specs,
        out_specs=o_spec,
        out_shape=jax.ShapeDtypeStruct((n_b, n_t, 1024, TILE), BF),
        scratch_shapes=scratch,
        input_output_aliases=aliases,
        compiler_params=_cparams(("arbitrary", "arbitrary", "arbitrary")),
        name=name,
    )(qT, k, vT, *extra, *alias_in)


def _diff_attn_kernel(q_ref, k_ref, v_ref, lam_ref, g_ref, *rest, n_lat, lam_init, qs, latent):
    o_ref, q2_ref, m_ref, acc_ref = rest[-4:]
    nq = qs * TILE
    zeros = jnp.zeros((DIFF_DH, TILE), BF)
    for i in range(qs):
        q = q_ref[0, i]
        q2_ref[0:64, i * TILE:(i + 1) * TILE] = q[0:64]
        q2_ref[64:128, i * TILE:(i + 1) * TILE] = zeros
        q2_ref[0:64, nq + i * TILE:nq + (i + 1) * TILE] = zeros
        q2_ref[64:128, nq + i * TILE:nq + (i + 1) * TILE] = q[64:128]
    o = _flash_T(k_ref, v_ref, q2_ref[...], m_ref, acc_ref, n_lat, latent)
    dl = lam_ref[...]
    lam = (jnp.exp(jnp.sum(dl[0:1] * dl[1:2], axis=1, keepdims=True))
           - jnp.exp(jnp.sum(dl[2:3] * dl[3:4], axis=1, keepdims=True)) + lam_init)
    od = o[:, 0:nq] - lam * o[:, nq:2 * nq]
    y = od * lax.rsqrt(jnp.mean(od * od, axis=0, keepdims=True) + EPS) * (1.0 - lam_init)
    for i in range(qs):
        o_ref[0, i] = (y[:, i * TILE:(i + 1) * TILE] * g_ref[...]).astype(BF)


def _diff_attn(dqT, dk, dvT, dlam, gcol, n_b, n_t, lam_init, ctx_into=None):
    qs = DIFF_QS if ctx_into is None else 1
    kern = functools.partial(_diff_attn_kernel, n_lat=n_t - 1, lam_init=lam_init, qs=qs, latent=ctx_into is None)
    k_spec_fn = lambda rows, blk: pl.BlockSpec((1, rows, 128), lambda b, h, q: (b, blk, h))
    scratch = [pltpu.VMEM((128, 2 * qs * TILE), BF), pltpu.VMEM((1, 2 * qs * TILE), F32),
               pltpu.VMEM((128 + ONES_ROWS, 2 * qs * TILE), F32)]
    extra_specs = [pl.BlockSpec((4, DIFF_DH), lambda b, h, q: (0, 0)),
                   pl.BlockSpec((128, TILE), lambda b, h, q: (0, 0))]
    return _attn_call(kern, "diff_attn" if ctx_into is None else "diff_attn_ctx", 128, k_spec_fn, n_b, DIFF_H, n_t,
                      qs, scratch, dqT, dk, dvT, [dlam, gcol], extra_specs, ctx_into)


def _mla_attn_kernel(q_ref, k_ref, v_ref, *rest, n_lat, qs, latent):
    o_ref, q2_ref, m_ref, acc_ref = rest[-4:]
    for i in range(qs):
        q2_ref[:, i * TILE:(i + 1) * TILE] = q_ref[0, i]
    o = _flash_T(k_ref.at[0], v_ref, q2_ref[...], m_ref, acc_ref, n_lat, latent)
    for i in range(qs):
        o_ref[0, i] = o[:, i * TILE:(i + 1) * TILE].astype(BF)


def _mla_attn(mqT, mk, mvT, n_b, n_t, ctx_into=None):
    qs = MLA_QS if ctx_into is None else 1
    kern = functools.partial(_mla_attn_kernel, n_lat=n_t - 1, qs=qs, latent=ctx_into is None)
    k_spec_fn = lambda rows, blk: pl.BlockSpec((1, 1, rows, 192), lambda b, h, q: (b, h, blk, 0))
    scratch = [pltpu.VMEM((192, qs * TILE), BF), pltpu.VMEM((1, qs * TILE), F32),
               pltpu.VMEM((128 + ONES_ROWS, qs * TILE), F32)]
    return _attn_call(kern, "mla_attn" if ctx_into is None else "mla_attn_ctx", 192, k_spec_fn, n_b, MLA_H, n_t,
                      qs, scratch, mqT, mk, mvT, [], [], ctx_into)


def _gla_tile(q_ref, k_ref, v_ref, la_ref, tri_ref, st_ref, reverse):
    g = la_ref[0]
    g1 = g.astype(BF)
    r1 = g - g1.astype(F32)
    g2 = r1.astype(BF)
    g3 = (r1 - g2.astype(F32)).astype(BF)
    tri = tri_ref[...]
    bcum = _dot(tri, g1) + _dot(tri, g2) + _dot(tri, g3)
    n_chunk = TILE // GLA_CHUNK
    n_sub = GLA_CHUNK // GLA_SUB
    rows64 = lax.broadcasted_iota(jnp.int32, (GLA_CHUNK, GLA_DK), 0)
    r16 = lax.broadcasted_iota(jnp.int32, (GLA_SUB, GLA_CHUNK), 0)
    c16 = lax.broadcasted_iota(jnp.int32, (GLA_SUB, GLA_CHUNK), 1)
    outs = []
    for hh in range(GLA_H):
        kc0, kc1 = GLA_DK * hh, GLA_DK * (hh + 1)
        o_chunks = [None] * n_chunk
        for c in (range(n_chunk - 1, -1, -1) if reverse else range(n_chunk)):
            r0 = GLA_CHUNK * c
            bc = bcum[r0:r0 + GLA_CHUNK, kc0:kc1]
            qc = q_ref[0, r0:r0 + GLA_CHUNK, kc0:kc1].astype(F32)
            kc = k_ref[0, r0:r0 + GLA_CHUNK, kc0:kc1].astype(F32)
            vc = v_ref[0, r0:r0 + GLA_CHUNK, GLA_DV * hh:GLA_DV * (hh + 1)]
            b_tot = bc[0:1] if reverse else bc[GLA_CHUNK - 1:GLA_CHUNK]
            a_rows = []
            for i in range(n_sub):
                s0 = GLA_SUB * i
                ref_row = bc[s0 + GLA_SUB - 1:s0 + GLA_SUB] if reverse else bc[s0:s0 + 1]
                qs = (qc[s0:s0 + GLA_SUB] * jnp.exp(bc[s0:s0 + GLA_SUB] - ref_row)).astype(BF)
                valid = (rows64 >= s0) if reverse else (rows64 < s0 + GLA_SUB)
                ks = (kc * jnp.exp(jnp.where(valid, ref_row - bc, -jnp.inf))).astype(BF)
                a = _dot_nt(qs, ks)
                keep = (c16 > r16 + s0) if reverse else (c16 <= r16 + s0)
                a_rows.append(jnp.where(keep, a, 0.0))
            a_mat = jnp.concatenate(a_rows, axis=0).astype(BF)
            st = st_ref[hh]
            q_in = (qc * jnp.exp(bc)).astype(BF)
            o_chunks[c] = _dot(a_mat, vc) + _dot_nt(q_in, st.astype(BF))
            k_d = (kc * jnp.exp(b_tot - bc)).astype(BF)
            st_ref[hh] = st * jnp.exp(b_tot) + _dot_tn(vc, k_d)
        outs.append(jnp.concatenate(o_chunks, axis=0))
    return jnp.concatenate(outs, axis=1)


def _gla_fwd_kernel(q_ref, k_ref, v_ref, la_ref, tri_ref, o_ref, st_ref):
    @pl.when(pl.program_id(1) == 0)
    def _():
        st_ref[...] = jnp.zeros(st_ref.shape, F32)
    o_ref[0] = _gla_tile(q_ref, k_ref, v_ref, la_ref, tri_ref, st_ref, False)


def _gla_bwd_kernel(q_ref, k_ref, v_ref, la_ref, tri_ref, of_ref, r_ref, g_ref, y_ref, st_ref):
    @pl.when(pl.program_id(1) == 0)
    def _():
        st_ref[...] = jnp.zeros(st_ref.shape, F32)
    o = of_ref[0] + _gla_tile(q_ref, k_ref, v_ref, la_ref, tri_ref, st_ref, True)
    gn = g_ref[...]
    for hh in range(GLA_H):
        sl = slice(GLA_DV * hh, GLA_DV * (hh + 1))
        oh = o[:, sl]
        yh = oh * lax.rsqrt(jnp.mean(oh * oh, axis=-1, keepdims=True) + EPS) * gn
        y_ref[0, :, sl] = (yh * _silu(r_ref[0, :, sl].astype(F32))).astype(BF)


def _gla(gq, gk, gv, la, gr, tri_lo, tri_up, gnorm, n_b, n_t):
    ltot = n_t * TILE
    t_fwd = lambda s: (s + n_t - 1) % n_t
    t_bwd = lambda s: jnp.where(s == 0, n_t - 1, n_t - 1 - s)

    def specs(tmap, dirn):
        ts = lambda c, cb=0: pl.BlockSpec((1, TILE, c), lambda b, s: (b, tmap(s), cb))
        return [ts(512), ts(512), ts(1024), ts(512, dirn), pl.BlockSpec((TILE, TILE), lambda b, s: (0, 0))]

    o_f = pl.pallas_call(
        _gla_fwd_kernel,
        grid=(n_b, n_t),
        in_specs=specs(t_fwd, 0),
        out_specs=pl.BlockSpec((1, TILE, 1024), lambda b, s: (b, t_fwd(s), 0)),
        out_shape=jax.ShapeDtypeStruct((n_b, ltot, 1024), F32),
        scratch_shapes=[pltpu.VMEM((GLA_H, GLA_DV, GLA_DK), F32)],
        compiler_params=_cparams(("arbitrary", "arbitrary")),
        name="gla_fwd",
    )(gq, gk, gv, la, tri_lo)
    tsb = lambda c: pl.BlockSpec((1, TILE, c), lambda b, s: (b, t_bwd(s), 0))
    return pl.pallas_call(
        _gla_bwd_kernel,
        grid=(n_b, n_t),
        in_specs=specs(t_bwd, 1) + [tsb(1024), tsb(1024), pl.BlockSpec((1, GLA_DV), lambda b, s: (0, 0))],
        out_specs=tsb(1024),
        out_shape=jax.ShapeDtypeStruct((n_b, ltot, 1024), BF),
        scratch_shapes=[pltpu.VMEM((GLA_H, GLA_DV, GLA_DK), F32)],
        compiler_params=_cparams(("arbitrary", "arbitrary")),
        name="gla_bwd",
    )(gq, gk, gv, la, tri_up, o_f, gr, gnorm)


def _merge_kernel(x_ref, ya_ref, yb_ref, yc_ref, g_ref, g1_ref, wb_ref, wo_ref, lg_ref, lb_ref, o_ref):
    zt = (_sigmoid(g_ref[0, 0, 0:1024, :].astype(F32)) * _dot_nt(wb_ref[0], ya_ref[0])
          + _sigmoid(g_ref[0, 0, 1024:2048, :].astype(F32)) * _dot(wb_ref[1], yb_ref[0, 0])
          + _sigmoid(g_ref[0, 0, 2048:3072, :].astype(F32)) * _dot(wb_ref[2], yc_ref[0, 0]))
    u = _dot(wo_ref[...], zt.astype(BF)).T
    y = _ln_rows(DN_ALPHA * x_ref[0] + g1_ref[0] * u)
    o_ref[0] = y * lg_ref[...] + lb_ref[...]


def _merge(xall, ya, ybT, ycT, gatesT, mod_l, p, n_b, n_t, n_q):
    ltot = n_t * TILE
    tile_spec = lambda c: pl.BlockSpec((1, TILE, c), lambda b, t: (b, t, 0))
    fm_spec = lambda c: pl.BlockSpec((1, 1, c, TILE), lambda b, t: (b, t, 0, 0))
    mrow = lambda b, t: jnp.where(t == n_t - 1, n_b, b)
    return pl.pallas_call(
        _merge_kernel,
        grid=(n_b, n_q),
        in_specs=[tile_spec(D_MODEL), tile_spec(1024), fm_spec(1024), fm_spec(1024), fm_spec(3072),
                  pl.BlockSpec((1, 1, D_MODEL), lambda b, t: (mrow(b, t), 0, 2)),
                  _resident((3, D_MODEL, D_MODEL)), _resident((D_MODEL, D_MODEL)),
                  _resident((1, D_MODEL)), _resident((1, D_MODEL))],
        out_specs=tile_spec(D_MODEL),
        out_shape=jax.ShapeDtypeStruct((n_b, n_q * TILE, D_MODEL), F32),
        compiler_params=_cparams(("arbitrary", "arbitrary")),
        name="merge",
    )(xall, ya, ybT, ycT, gatesT, mod_l, p["wbT"], p["woT"], p["ln1_g"], p["ln1_b"])


def _ffn_kernel(x_ref, sh_ref, sc_ref, g2_ref, wi_ref, wo_ref, lg_ref, lb_ref, o_ref):
    x = x_ref[0]
    h = (_ln_rows(x) * (1.0 + sc_ref[0]) + sh_ref[0]).astype(BF)
    acc = jnp.zeros((TILE, D_MODEL), F32)
    for j in range(FFN_H // FFN_CHUNK):
        a = FFN_CHUNK * j
        gate = _dot(h, wi_ref[:, a:a + FFN_CHUNK])
        up = _dot(h, wi_ref[:, FFN_H + a:FFN_H + a + FFN_CHUNK])
        acc = acc + _dot((_silu(gate) * up).astype(BF), wo_ref[a:a + FFN_CHUNK, :])
    y = _ln_rows(DN_ALPHA * x + g2_ref[0] * acc)
    o_ref[0] = y * lg_ref[...] + lb_ref[...]


def _ffn(x1, mod_l, p, n_b, n_t, n_q):
    tile_spec = lambda c: pl.BlockSpec((1, TILE, c), lambda b, t: (b, t, 0))
    mrow = lambda b, t: jnp.where(t == n_t - 1, n_b, b)
    mspec = lambda j: pl.BlockSpec((1, 1, D_MODEL), lambda b, t: (mrow(b, t), 0, j))
    return pl.pallas_call(
        _ffn_kernel,
        grid=(n_b, n_q),
        in_specs=[tile_spec(D_MODEL), mspec(3), mspec(4), mspec(5),
                  _resident((D_MODEL, 2 * FFN_H)), _resident((FFN_H, D_MODEL)),
                  _resident((1, D_MODEL)), _resident((1, D_MODEL))],
        out_specs=tile_spec(D_MODEL),
        out_shape=jax.ShapeDtypeStruct((n_b, n_q * TILE, D_MODEL), F32),
        compiler_params=_cparams(("arbitrary", "arbitrary")),
        name="ffn",
    )(x1, mod_l, mod_l, mod_l, p["ffn_wi"], p["ffn_wo"], p["ln2_g"], p["ln2_b"])


def _swap32(w):
    k, n = w.shape
    return w.reshape(k, n // 64, 2, 32)[:, :, ::-1, :].reshape(k, n)


def _prep_layer(l, w_in, gla_w_a2, gla_b_a, mla_q_norm_g, mla_kv_norm_g, mla_w_uq, mla_w_ukv, w_branch, w_out,
                ln1_g, ln1_b, ffn_w_in, ffn_w_out, ln2_g, ln2_b):
    w = w_in[l]
    seg = lambda a, b: w[:, a:b]
    zc = lambda n: jnp.zeros((D_MODEL, n), F32)
    dk_w, mkr_w = seg(O_DK, O_DV), seg(O_MKR, O_GATES)
    w_tok = jnp.concatenate([
        seg(O_GQ, O_GK), seg(O_GK, O_GV), seg(O_GV, O_GR), seg(O_GR, O_GA), dk_w, _swap32(dk_w),
        seg(O_MQ, O_MKV), seg(O_MKV, O_MKR),
        mkr_w, seg(O_GA, O_DQ), zc(32), _swap32(mkr_w), zc(64)], axis=1).astype(BF)
    w_featT = jnp.concatenate([seg(O_DQ, O_DK) * DIFF_DH ** -0.5, seg(O_DV, O_MQ), seg(O_GATES, O_END)],
                              axis=1).T.astype(BF)
    wa2 = jnp.zeros((128, 1024), F32)
    wa2 = wa2.at[64:80, 0:512].set(gla_w_a2[l, 0]).at[80:96, 512:1024].set(gla_w_a2[l, 1]).astype(BF)
    ukv = mla_w_ukv[l].reshape(MLA_KVR, MLA_H, MLA_NOPE + MLA_DV)
    return dict(
        w_tok=w_tok, w_featT=w_featT, wa2=wa2, ba=gla_b_a[l].reshape(1, 1024),
        qg=mla_q_norm_g[l].reshape(1, MLA_QR), kvg=mla_kv_norm_g[l].reshape(1, MLA_KVR),
        wukv_k=ukv[:, :, :MLA_NOPE].reshape(MLA_KVR, 1024).astype(BF),
        wukv_vT=ukv[:, :, MLA_NOPE:].reshape(MLA_KVR, 1024).T.astype(BF),
        wuqT=mla_w_uq[l].T.astype(BF),
        wbT=jnp.swapaxes(w_branch[l], 1, 2).astype(BF), woT=w_out[l].T.astype(BF),
        ln1_g=ln1_g[l].reshape(1, D_MODEL), ln1_b=ln1_b[l].reshape(1, D_MODEL),
        ffn_wi=ffn_w_in[l].astype(BF), ffn_wo=ffn_w_out[l].astype(BF),
        ln2_g=ln2_g[l].reshape(1, D_MODEL), ln2_b=ln2_b[l].reshape(1, D_MODEL))


def _rope_tables(l_lat, l_ctx):
    rows = l_lat // GRID_W
    pos_row = jnp.broadcast_to(jnp.arange(rows, dtype=F32)[:, None], (rows, GRID_W)).reshape(l_lat)
    pos_col = jnp.broadcast_to(jnp.arange(GRID_W, dtype=F32)[None, :], (rows, GRID_W)).reshape(l_lat)
    d_axis = ROPE_DIM // 2
    inv = ROPE_BASE ** (-jnp.arange(0, d_axis, 2, dtype=F32) / d_axis)
    ang = jnp.concatenate([pos_row[:, None] * inv, pos_col[:, None] * inv], axis=-1)
    cos = jnp.concatenate([jnp.cos(ang), jnp.ones((l_ctx, 32), F32)], axis=0)
    sin = jnp.concatenate([jnp.sin(ang), jnp.zeros((l_ctx, 32), F32)], axis=0)
    ctok = jnp.tile(cos, (1, 4))
    stok = jnp.tile(jnp.concatenate([-sin, sin], axis=1), (1, 2))
    n_t = (l_lat + l_ctx) // TILE
    to_fm = lambda a: a.T.reshape(32, n_t, TILE).transpose(1, 0, 2)
    return ctok, stok, to_fm(cos), to_fm(sin)


def kernel(x, c, ctx, c_ctx, w_mod, b_mod, w_in, gla_w_a2, gla_b_a, gla_norm_g, diff_lam, diff_norm_g,
           mla_q_norm_g, mla_kv_norm_g, mla_w_uq, mla_w_ukv, w_branch, w_out, ln1_g, ln1_b, ffn_w_in, ffn_w_out,
           ln2_g, ln2_b):
    n_b, l_lat, _ = x.shape
    l_ctx = ctx.shape[1]
    assert l_ctx == TILE and n_b + 1 <= 16
    assert l_lat % (TILE * max(DIFF_QS, MLA_QS, KEY_TILES)) == 0
    ltot = l_lat + l_ctx
    n_t = ltot // TILE
    ctok, stok, cosT, sinT = _rope_tables(l_lat, l_ctx)
    c_all = jnp.zeros((16, D_MODEL), F32).at[:n_b].set(c).at[n_b].set(c_ctx)
    mod = _modulation(c_all, w_mod, b_mod)
    ii = lax.broadcasted_iota(jnp.int32, (TILE, TILE), 0)
    jj = lax.broadcasted_iota(jnp.int32, (TILE, TILE), 1)
    same = (ii // GLA_CHUNK) == (jj // GLA_CHUNK)
    tri_lo = (same & (jj <= ii)).astype(BF)
    tri_up = (same & (jj >= ii)).astype(BF)
    xall = jnp.concatenate([x, ctx], axis=1)
    for l in range(N_LAYERS):
        last = l == N_LAYERS - 1
        n_q = n_t - 1 if last else n_t
        lam_init = 0.8 - 0.6 * math.exp(-0.3 * l)
        p = _prep_layer(l, w_in, gla_w_a2, gla_b_a, mla_q_norm_g, mla_kv_norm_g, mla_w_uq, mla_w_ukv, w_branch,
                        w_out, ln1_g, ln1_b, ffn_w_in, ffn_w_out, ln2_g, ln2_b)
        mod_l = mod[l].reshape(16, 1, 6 * D_MODEL)
        gq, gk, gv, gr, dk, la, mk, cq, ckv = _proj_tok(xall, mod_l, ctok, stok, p, n_b, n_t, ltot)
        dqT, dvT, gatesT, mqT, mvT = _proj_feat(xall, mod_l, cosT, sinT, cq, ckv, p, n_b, n_t)
        ya = _gla(gq, gk, gv, la, gr, tri_lo, tri_up, gla_norm_g[l].reshape(1, GLA_DV), n_b, n_t)
        gcol = jnp.broadcast_to(diff_norm_g[l].reshape(128, 1), (128, TILE))
        ybT = _diff_attn(dqT, dk, dvT, diff_lam[l], gcol, n_b, n_t, lam_init)
        ycT = _mla_attn(mqT, mk, mvT, n_b, n_t)
        if not last:
            ybT = _diff_attn(dqT, dk, dvT, diff_lam[l], gcol, n_b, n_t, lam_init, ctx_into=ybT)
            ycT = _mla_attn(mqT, mk, mvT, n_b, n_t, ctx_into=ycT)
        x1 = _merge(xall, ya, ybT, ycT, gatesT, mod_l, p, n_b, n_t, n_q)
        xall = _ffn(x1, mod_l, p, n_b, n_t, n_q)
    return xall
```

```python
import functools
import math

import jax
import jax.numpy as jnp
from jax import lax
from jax.experimental import pallas as pl
from jax.experimental.pallas import tpu as pltpu

BF = jnp.bfloat16
F32 = jnp.float32

D_MODEL = 1024
N_LAYERS = 2
GRID_W = 64
TILE = 256
GLA_H, GLA_DK, GLA_DV, GLA_RANK, GLA_TAU = 4, 128, 256, 16, 16.0
GLA_CHUNK, GLA_SUB = 64, 16
DIFF_H, DIFF_DH = 8, 64
MLA_H, MLA_QR, MLA_KVR, MLA_NOPE, MLA_ROPE, MLA_DV = 8, 256, 128, 128, 64, 128
MLA_SCALE = (MLA_NOPE + MLA_ROPE) ** -0.5
ROPE_DIM, ROPE_BASE = 64, 10000.0
FFN_H = 2816
FFN_CHUNK = 704
DN_ALPHA = (2 * N_LAYERS) ** 0.25
EPS = 1e-6
DIFF_QS, MLA_QS = 2, 4
KEY_TILES = 2
KEY_UNROLL = 4
LOG2E = math.log2(math.e)
ONES_ROWS = 16
VMEM_LIMIT = 56 * 1024 * 1024

_SIZES = (512, 512, 1024, 1024, 32, 1024, 1024, 1024, 256, 128, 64, 3072)
_OFF = [0]
for _s in _SIZES:
    _OFF.append(_OFF[-1] + _s)
(O_GQ, O_GK, O_GV, O_GR, O_GA, O_DQ, O_DK, O_DV, O_MQ, O_MKV, O_MKR, O_GATES, O_END) = _OFF

T_GQ, T_GK, T_GV, T_GR, T_DK, T_DKS, T_MQ, T_MKV, T_SMA, T_SMB, T_END = (
    0, 512, 1024, 2048, 3072, 4096, 5120, 5376, 5504, 5632, 5760)
F_DQ, F_DV, F_GATES, F_END = 0, 1024, 2048, 5120


def _cparams(sem):
    return pltpu.CompilerParams(dimension_semantics=sem, vmem_limit_bytes=VMEM_LIMIT)


def _resident(shape):
    nd = len(shape)
    return pl.BlockSpec(shape, lambda *_: (0,) * nd, pipeline_mode=pl.Buffered(1))


def _dot(a, b):
    return jnp.dot(a, b, preferred_element_type=F32)


def _dot_nt(a, b):
    return lax.dot_general(a, b, (((1,), (1,)), ((), ())), preferred_element_type=F32)


def _dot_tn(a, b):
    return lax.dot_general(a, b, (((0,), (0,)), ((), ())), preferred_element_type=F32)


def _ln_rows(x):
    mu = jnp.mean(x, axis=-1, keepdims=True)
    xc = x - mu
    var = jnp.mean(xc * xc, axis=-1, keepdims=True)
    return xc * lax.rsqrt(var + EPS)


def _sigmoid(x):
    return 1.0 / (1.0 + jnp.exp(-x))


def _silu(x):
    return x * _sigmoid(x)


def _mod_kernel(c_ref, w_ref, b_ref, o_ref):
    ca = _silu(c_ref[...]).astype(BF)
    o_ref[0] = _dot(ca, w_ref[0].astype(BF)) + b_ref[0]


def _modulation(c_all, w_mod, b_mod):
    nl, d, n6 = w_mod.shape
    r = c_all.shape[0]
    nblk = 1536
    return pl.pallas_call(
        _mod_kernel,
        grid=(nl, n6 // nblk),
        in_specs=[pl.BlockSpec((r, d), lambda l, j: (0, 0)),
                  pl.BlockSpec((1, d, nblk), lambda l, j: (l, 0, j)),
                  pl.BlockSpec((1, 1, nblk), lambda l, j: (l, 0, j))],
        out_specs=pl.BlockSpec((1, r, nblk), lambda l, j: (l, 0, j)),
        out_shape=jax.ShapeDtypeStruct((nl, r, n6), F32),
        compiler_params=_cparams(("arbitrary", "arbitrary")),
        name="modulation",
    )(c_all, w_mod, b_mod.reshape(nl, 1, n6))


def _proj_tok_kernel(x_ref, sh_ref, sc_ref, ct_ref, st_ref, w_ref, wa2_ref, ba_ref, qg_ref, kvg_ref, wk_ref,
                     gq_ref, gk_ref, gv_ref, gr_ref, dk_ref, la_ref, mk_ref, cq_ref, ckv_ref):
    h = (_ln_rows(x_ref[0]) * (1.0 + sc_ref[0]) + sh_ref[0]).astype(BF)

    def proj(a, b):
        return _dot(h, w_ref[:, a:b])

    gq_ref[0] = (proj(T_GQ, T_GK) * GLA_DK ** -0.5).astype(BF)
    gk_ref[0] = proj(T_GK, T_GV).astype(BF)
    gv_ref[0] = proj(T_GV, T_GR).astype(BF)
    gr_ref[0] = proj(T_GR, T_DK).astype(BF)
    ct = ct_ref[...]
    st = st_ref[...]
    for hh in range(DIFF_H):
        a = T_DK + 128 * hh
        dk_ref[0, :, 128 * hh:128 * (hh + 1)] = (proj(a, a + 128) * ct + proj(a + 1024, a + 1152) * st).astype(BF)
    mq = proj(T_MQ, T_MKV)
    cq = mq * lax.rsqrt(jnp.mean(mq * mq, axis=-1, keepdims=True) + EPS) * qg_ref[...]
    cq_ref[0] = cq.astype(BF)
    mkv = proj(T_MKV, T_SMA)
    ckv = (mkv * lax.rsqrt(jnp.mean(mkv * mkv, axis=-1, keepdims=True) + EPS) * kvg_ref[...]).astype(BF)
    ckv_ref[0] = ckv
    kn = _dot(ckv, wk_ref[...]).astype(BF)
    sma = proj(T_SMA, T_SMB)
    smb = proj(T_SMB, T_END)
    kr = (sma[:, 0:64] * ct[:, 0:64] + smb[:, 0:64] * st[:, 0:64]).astype(BF)
    for hh in range(MLA_H):
        mk_ref[0, hh, :, 0:128] = kn[:, 128 * hh:128 * (hh + 1)]
        mk_ref[0, hh, :, 128:192] = kr
    z = _dot(sma.astype(BF), wa2_ref[...]) + ba_ref[...]
    la_ref[0] = (jnp.minimum(z, 0.0) - jnp.log(1.0 + jnp.exp(-jnp.abs(z)))) * (1.0 / GLA_TAU)


def _proj_tok(xall, mod_l, ctok, stok, p, n_b, n_t, ltot):
    tile_spec = lambda c: pl.BlockSpec((1, TILE, c), lambda b, t: (b, t, 0))
    mrow = lambda b, t: jnp.where(t == n_t - 1, n_b, b)
    bf_out = lambda c: jax.ShapeDtypeStruct((n_b, ltot, c), BF)
    return pl.pallas_call(
        _proj_tok_kernel,
        grid=(n_b, n_t),
        in_specs=[tile_spec(D_MODEL),
                  pl.BlockSpec((1, 1, D_MODEL), lambda b, t: (mrow(b, t), 0, 0)),
                  pl.BlockSpec((1, 1, D_MODEL), lambda b, t: (mrow(b, t), 0, 1)),
                  pl.BlockSpec((TILE, 128), lambda b, t: (t, 0)),
                  pl.BlockSpec((TILE, 128), lambda b, t: (t, 0)),
                  _resident((D_MODEL, T_END)), _resident((128, 1024)), _resident((1, 1024)),
                  _resident((1, MLA_QR)), _resident((1, MLA_KVR)), _resident((MLA_KVR, 1024))],
        out_specs=[tile_spec(512), tile_spec(512), tile_spec(1024), tile_spec(1024), tile_spec(1024),
                   tile_spec(1024),
                   pl.BlockSpec((1, MLA_H, TILE, 192), lambda b, t: (b, 0, t, 0)),
                   tile_spec(MLA_QR), tile_spec(MLA_KVR)],
        out_shape=[bf_out(512), bf_out(512), bf_out(1024), bf_out(1024), bf_out(1024),
                   jax.ShapeDtypeStruct((n_b, ltot, 1024), F32),
                   jax.ShapeDtypeStruct((n_b, MLA_H, ltot, 192), BF),
                   bf_out(MLA_QR), bf_out(MLA_KVR)],
        compiler_params=_cparams(("arbitrary", "arbitrary")),
        name="proj_tok",
    )(xall, mod_l, mod_l, ctok, stok, p["w_tok"], p["wa2"], p["ba"], p["qg"], p["kvg"], p["wukv_k"])


def _rope_rows(x1, x2, cos, sin):
    return x1 * cos - x2 * sin, x1 * sin + x2 * cos


def _proj_feat_kernel(x_ref, sh_ref, sc_ref, cos_ref, sin_ref, cq_ref, ckv_ref, w_ref, wuq_ref, wv_ref,
                      dq_ref, dv_ref, g_ref, mq_ref, mv_ref):
    h = (_ln_rows(x_ref[0]) * (1.0 + sc_ref[0]) + sh_ref[0]).astype(BF)
    cos = cos_ref[0]
    sin = sin_ref[0]
    for g in range(2 * DIFF_H):
        r = _dot_nt(w_ref[F_DQ + 64 * g:F_DQ + 64 * (g + 1), :], h) * LOG2E
        o1, o2 = _rope_rows(r[0:32], r[32:64], cos, sin)
        dq_ref[0, 0, 64 * g:64 * g + 32, :] = o1.astype(BF)
        dq_ref[0, 0, 64 * g + 32:64 * (g + 1), :] = o2.astype(BF)
    dv_ref[0, 0] = _dot_nt(w_ref[F_DV:F_GATES, :], h).astype(BF)
    for j in range(3):
        a = F_GATES + 1024 * j
        g_ref[0, 0, 1024 * j:1024 * (j + 1), :] = _dot_nt(w_ref[a:a + 1024, :], h).astype(BF)
    cq = cq_ref[0]
    for hh in range(MLA_H):
        r = _dot_nt(wuq_ref[192 * hh:192 * (hh + 1), :], cq) * (MLA_SCALE * LOG2E)
        o1, o2 = _rope_rows(r[128:160], r[160:192], cos, sin)
        mq_ref[0, 0, 192 * hh:192 * hh + 128, :] = r[0:128].astype(BF)
        mq_ref[0, 0, 192 * hh + 128:192 * hh + 160, :] = o1.astype(BF)
        mq_ref[0, 0, 192 * hh + 160:192 * (hh + 1), :] = o2.astype(BF)
    mv_ref[0, 0] = _dot_nt(wv_ref[...], ckv_ref[0]).astype(BF)


def _proj_feat(xall, mod_l, cosT, sinT, cq, ckv, p, n_b, n_t):
    tile_spec = lambda c: pl.BlockSpec((1, TILE, c), lambda b, t: (b, t, 0))
    mrow = lambda b, t: jnp.where(t == n_t - 1, n_b, b)
    fm_spec = lambda c: pl.BlockSpec((1, 1, c, TILE), lambda b, t: (b, t, 0, 0))
    fm_out = lambda c: jax.ShapeDtypeStruct((n_b, n_t, c, TILE), BF)
    return pl.pallas_call(
        _proj_feat_kernel,
        grid=(n_b, n_t),
        in_specs=[tile_spec(D_MODEL),
                  pl.BlockSpec((1, 1, D_MODEL), lambda b, t: (mrow(b, t), 0, 0)),
                  pl.BlockSpec((1, 1, D_MODEL), lambda b, t: (mrow(b, t), 0, 1)),
                  pl.BlockSpec((1, 32, TILE), lambda b, t: (t, 0, 0)),
                  pl.BlockSpec((1, 32, TILE), lambda b, t: (t, 0, 0)),
                  tile_spec(MLA_QR), tile_spec(MLA_KVR),
                  _resident((F_END, D_MODEL)), _resident((MLA_H * 192, MLA_QR)), _resident((1024, MLA_KVR))],
        out_specs=[fm_spec(1024), fm_spec(1024), fm_spec(3072), fm_spec(MLA_H * 192), fm_spec(1024)],
        out_shape=[fm_out(1024), fm_out(1024), fm_out(3072), fm_out(MLA_H * 192), fm_out(1024)],
        compiler_params=_cparams(("arbitrary", "arbitrary")),
        name="proj_feat",
    )(xall, mod_l, mod_l, cosT, sinT, cq, ckv, p["w_featT"], p["wuqT"], p["wukv_vT"])


def _flash_T(k_ref, v_ref, q2, m_ref, acc_ref, s_ref, n_lat, latent_keys):
    dv = v_ref.shape[2]
    m_ref[...] = jnp.full(m_ref.shape, -jnp.inf, F32)
    acc_ref[...] = jnp.zeros(acc_ref.shape, F32)
    rows = KEY_TILES * TILE

    def scores(slot, tile0):
        s_ref[slot] = _dot(k_ref[0, pl.ds(pl.multiple_of(tile0 * TILE, TILE), rows), :], q2)

    def consume(s, tile0, ntiles):
        m_old = m_ref[...]
        m_new = jnp.maximum(m_old, jnp.max(s, axis=0, keepdims=True))
        alpha = jnp.exp2(m_old - m_new)
        p = jnp.exp2(s - m_new).astype(BF)
        v = jnp.concatenate([v_ref[0, tile0 + i] for i in range(ntiles)], axis=1)
        vext = jnp.concatenate([v, jnp.ones((ONES_ROWS, ntiles * TILE), BF)], axis=0)
        acc_ref[...] = acc_ref[...] * alpha + _dot(vext, p)
        m_ref[...] = m_new

    if latent_keys:
        n_steps = n_lat // KEY_TILES
        scores(0, 0)

        def body(jj, carry):
            for u in range(KEY_UNROLL):
                j = KEY_UNROLL * jj + u
                scores((u + 1) % 2, (j + 1) * KEY_TILES)
                consume(s_ref[u % 2], j * KEY_TILES, KEY_TILES)
            return carry
        lax.fori_loop(0, n_steps // KEY_UNROLL - 1, body, 0)
        for u in range(KEY_UNROLL):
            j = n_steps - KEY_UNROLL + u
            if u < KEY_UNROLL - 1:
                scores((u + 1) % 2, (j + 1) * KEY_TILES)
            else:
                s_ctx = _dot(k_ref[0, pl.ds(n_lat * TILE, TILE), :], q2)
            consume(s_ref[u % 2], j * KEY_TILES, KEY_TILES)
        consume(s_ctx, n_lat, 1)
    else:
        consume(_dot(k_ref[0, pl.ds(0, TILE), :], q2), 0, 1)
    acc = acc_ref[...]
    return acc[0:dv] / acc[dv:dv + 1]


def _attn_call(kern, name, q_rows, k_spec_fn, n_b, n_heads, n_t, qs, scratch, qT, k, vT, extra, extra_specs,
               ctx_into=None):
    n_lat = n_t - 1
    if ctx_into is None:
        grid = (n_b, n_heads, n_lat // qs)
        q_spec = pl.BlockSpec((1, qs, q_rows, TILE), lambda b, h, q: (b, q, h, 0))
        k_spec = k_spec_fn(n_t * TILE, 0)
        v_spec = pl.BlockSpec((1, n_t, 128, TILE), lambda b, h, q: (b, 0, h, 0))
        o_spec = pl.BlockSpec((1, qs, 128, TILE), lambda b, h, q: (b, q, h, 0))
        alias_in, alias_specs, aliases = [], [], {}
    else:
        grid = (n_b, n_heads, 1)
        q_spec = pl.BlockSpec((1, 1, q_rows, TILE), lambda b, h, q: (b, n_lat, h, 0))
        k_spec = k_spec_fn(TILE, n_lat)
        v_spec = pl.BlockSpec((1, 1, 128, TILE), lambda b, h, q: (b, n_lat, h, 0))
        o_spec = pl.BlockSpec((1, 1, 128, TILE), lambda b, h, q: (b, n_lat, h, 0))
        alias_in, alias_specs = [ctx_into], [pl.BlockSpec(memory_space=pl.ANY)]
        aliases = {3 + len(extra): 0}
    return pl.pallas_call(
        kern,
        grid=grid,
        in_specs=[q_spec, k_spec, v_spec] + extra_specs + alias_specs,
        out_specs=o_spec,
        out_shape=jax.ShapeDtypeStruct((n_b, n_t, 1024, TILE), BF),
        scratch_shapes=scratch,
        input_output_aliases=aliases,
        compiler_params=_cparams(("arbitrary", "arbitrary", "arbitrary")),
        name=name,
    )(qT, k, vT, *extra, *alias_in)


def _diff_attn_kernel(q_ref, k_ref, v_ref, lam_ref, g_ref, *rest, n_lat, lam_init, qs, latent):
    o_ref, q2_ref, m_ref, acc_ref, s_ref = rest[-5:]
    nq = qs * TILE
    zeros = jnp.zeros((DIFF_DH, TILE), BF)
    for i in range(qs):
        q = q_ref[0, i]
        q2_ref[0:64, i * TILE:(i + 1) * TILE] = q[0:64]
        q2_ref[64:128, i * TILE:(i + 1) * TILE] = zeros
        q2_ref[0:64, nq + i * TILE:nq + (i + 1) * TILE] = zeros
        q2_ref[64:128, nq + i * TILE:nq + (i + 1) * TILE] = q[64:128]
    o = _flash_T(k_ref, v_ref, q2_ref[...], m_ref, acc_ref, s_ref, n_lat, latent)
    dl = lam_ref[...]
    lam = (jnp.exp(jnp.sum(dl[0:1] * dl[1:2], axis=1, keepdims=True))
           - jnp.exp(jnp.sum(dl[2:3] * dl[3:4], axis=1, keepdims=True)) + lam_init)
    od = o[:, 0:nq] - lam * o[:, nq:2 * nq]
    y = od * lax.rsqrt(jnp.mean(od * od, axis=0, keepdims=True) + EPS) * (1.0 - lam_init)
    for i in range(qs):
        o_ref[0, i] = (y[:, i * TILE:(i + 1) * TILE] * g_ref[...]).astype(BF)


def _diff_attn(dqT, dk, dvT, dlam, gcol, n_b, n_t, lam_init, ctx_into=None):
    qs = DIFF_QS if ctx_into is None else 1
    kern = functools.partial(_diff_attn_kernel, n_lat=n_t - 1, lam_init=lam_init, qs=qs, latent=ctx_into is None)
    k_spec_fn = lambda rows, blk: pl.BlockSpec((1, rows, 128), lambda b, h, q: (b, blk, h))
    scratch = [pltpu.VMEM((128, 2 * qs * TILE), BF), pltpu.VMEM((1, 2 * qs * TILE), F32),
               pltpu.VMEM((128 + ONES_ROWS, 2 * qs * TILE), F32),
               pltpu.VMEM((2, KEY_TILES * TILE, 2 * qs * TILE), F32)]
    extra_specs = [pl.BlockSpec((4, DIFF_DH), lambda b, h, q: (0, 0)),
                   pl.BlockSpec((128, TILE), lambda b, h, q: (0, 0))]
    return _attn_call(kern, "diff_attn" if ctx_into is None else "diff_attn_ctx", 128, k_spec_fn, n_b, DIFF_H, n_t,
                      qs, scratch, dqT, dk, dvT, [dlam, gcol], extra_specs, ctx_into)


def _mla_attn_kernel(q_ref, k_ref, v_ref, *rest, n_lat, qs, latent):
    o_ref, q2_ref, m_ref, acc_ref, s_ref = rest[-5:]
    for i in range(qs):
        q2_ref[:, i * TILE:(i + 1) * TILE] = q_ref[0, i]
    o = _flash_T(k_ref.at[0], v_ref, q2_ref[...], m_ref, acc_ref, s_ref, n_lat, latent)
    for i in range(qs):
        o_ref[0, i] = o[:, i * TILE:(i + 1) * TILE].astype(BF)


def _mla_attn(mqT, mk, mvT, n_b, n_t, ctx_into=None):
    qs = MLA_QS if ctx_into is None else 1
    kern = functools.partial(_mla_attn_kernel, n_lat=n_t - 1, qs=qs, latent=ctx_into is None)
    k_spec_fn = lambda rows, blk: pl.BlockSpec((1, 1, rows, 192), lambda b, h, q: (b, h, blk, 0))
    scratch = [pltpu.VMEM((192, qs * TILE), BF), pltpu.VMEM((1, qs * TILE), F32),
               pltpu.VMEM((128 + ONES_ROWS, qs * TILE), F32),
               pltpu.VMEM((2, KEY_TILES * TILE, qs * TILE), F32)]
    return _attn_call(kern, "mla_attn" if ctx_into is None else "mla_attn_ctx", 192, k_spec_fn, n_b, MLA_H, n_t,
                      qs, scratch, mqT, mk, mvT, [], [], ctx_into)


def _gla_tile(q_ref, k_ref, v_ref, la_ref, tri_ref, st_ref, reverse):
    g = la_ref[0]
    g1 = g.astype(BF)
    r1 = g - g1.astype(F32)
    g2 = r1.astype(BF)
    g3 = (r1 - g2.astype(F32)).astype(BF)
    tri = tri_ref[...]
    bcum = _dot(tri, g1) + _dot(tri, g2) + _dot(tri, g3)
    n_chunk = TILE // GLA_CHUNK
    n_sub = GLA_CHUNK // GLA_SUB
    rows64 = lax.broadcasted_iota(jnp.int32, (GLA_CHUNK, GLA_DK), 0)
    r16 = lax.broadcasted_iota(jnp.int32, (GLA_SUB, GLA_CHUNK), 0)
    c16 = lax.broadcasted_iota(jnp.int32, (GLA_SUB, GLA_CHUNK), 1)
    outs = []
    for hh in range(GLA_H):
        kc0, kc1 = GLA_DK * hh, GLA_DK * (hh + 1)
        o_chunks = [None] * n_chunk
        for c in (range(n_chunk - 1, -1, -1) if reverse else range(n_chunk)):
            r0 = GLA_CHUNK * c
            bc = bcum[r0:r0 + GLA_CHUNK, kc0:kc1]
            qc = q_ref[0, r0:r0 + GLA_CHUNK, kc0:kc1].astype(F32)
            kc = k_ref[0, r0:r0 + GLA_CHUNK, kc0:kc1].astype(F32)
            vc = v_ref[0, r0:r0 + GLA_CHUNK, GLA_DV * hh:GLA_DV * (hh + 1)]
            b_tot = bc[0:1] if reverse else bc[GLA_CHUNK - 1:GLA_CHUNK]
            a_rows = []
            for i in range(n_sub):
                s0 = GLA_SUB * i
                ref_row = bc[s0 + GLA_SUB - 1:s0 + GLA_SUB] if reverse else bc[s0:s0 + 1]
                qs = (qc[s0:s0 + GLA_SUB] * jnp.exp(bc[s0:s0 + GLA_SUB] - ref_row)).astype(BF)
                valid = (rows64 >= s0) if reverse else (rows64 < s0 + GLA_SUB)
                ks = (kc * jnp.exp(jnp.where(valid, ref_row - bc, -jnp.inf))).astype(BF)
                a = _dot_nt(qs, ks)
                keep = (c16 > r16 + s0) if reverse else (c16 <= r16 + s0)
                a_rows.append(jnp.where(keep, a, 0.0))
            a_mat = jnp.concatenate(a_rows, axis=0).astype(BF)
            st = st_ref[hh]
            q_in = (qc * jnp.exp(bc)).astype(BF)
            o_chunks[c] = _dot(a_mat, vc) + _dot_nt(q_in, st.astype(BF))
            k_d = (kc * jnp.exp(b_tot - bc)).astype(BF)
            st_ref[hh] = st * jnp.exp(b_tot) + _dot_tn(vc, k_d)
        outs.append(jnp.concatenate(o_chunks, axis=0))
    return jnp.concatenate(outs, axis=1)


def _gla_fwd_kernel(q_ref, k_ref, v_ref, la_ref, tri_ref, o_ref, st_ref):
    @pl.when(pl.program_id(1) == 0)
    def _():
        st_ref[...] = jnp.zeros(st_ref.shape, F32)
    o_ref[0] = _gla_tile(q_ref, k_ref, v_ref, la_ref, tri_ref, st_ref, False)


def _gla_bwd_kernel(q_ref, k_ref, v_ref, la_ref, tri_ref, of_ref, r_ref, g_ref, y_ref, st_ref):
    @pl.when(pl.program_id(1) == 0)
    def _():
        st_ref[...] = jnp.zeros(st_ref.shape, F32)
    o = of_ref[0] + _gla_tile(q_ref, k_ref, v_ref, la_ref, tri_ref, st_ref, True)
    gn = g_ref[...]
    for hh in range(GLA_H):
        sl = slice(GLA_DV * hh, GLA_DV * (hh + 1))
        oh = o[:, sl]
        yh = oh * lax.rsqrt(jnp.mean(oh * oh, axis=-1, keepdims=True) + EPS) * gn
        y_ref[0, :, sl] = (yh * _silu(r_ref[0, :, sl].astype(F32))).astype(BF)


def _gla(gq, gk, gv, la, gr, tri_lo, tri_up, gnorm, n_b, n_t):
    ltot = n_t * TILE
    t_fwd = lambda s: (s + n_t - 1) % n_t
    t_bwd = lambda s: jnp.where(s == 0, n_t - 1, n_t - 1 - s)

    def specs(tmap, dirn):
        ts = lambda c, cb=0: pl.BlockSpec((1, TILE, c), lambda b, s: (b, tmap(s), cb))
        return [ts(512), ts(512), ts(1024), ts(512, dirn), pl.BlockSpec((TILE, TILE), lambda b, s: (0, 0))]

    o_f = pl.pallas_call(
        _gla_fwd_kernel,
        grid=(n_b, n_t),
        in_specs=specs(t_fwd, 0),
        out_specs=pl.BlockSpec((1, TILE, 1024), lambda b, s: (b, t_fwd(s), 0)),
        out_shape=jax.ShapeDtypeStruct((n_b, ltot, 1024), F32),
        scratch_shapes=[pltpu.VMEM((GLA_H, GLA_DV, GLA_DK), F32)],
        compiler_params=_cparams(("arbitrary", "arbitrary")),
        name="gla_fwd",
    )(gq, gk, gv, la, tri_lo)
    tsb = lambda c: pl.BlockSpec((1, TILE, c), lambda b, s: (b, t_bwd(s), 0))
    return pl.pallas_call(
        _gla_bwd_kernel,
        grid=(n_b, n_t),
        in_specs=specs(t_bwd, 1) + [tsb(1024), tsb(1024), pl.BlockSpec((1, GLA_DV), lambda b, s: (0, 0))],
        out_specs=tsb(1024),
        out_shape=jax.ShapeDtypeStruct((n_b, ltot, 1024), BF),
        scratch_shapes=[pltpu.VMEM((GLA_H, GLA_DV, GLA_DK), F32)],
        compiler_params=_cparams(("arbitrary", "arbitrary")),
        name="gla_bwd",
    )(gq, gk, gv, la, tri_up, o_f, gr, gnorm)


def _merge_kernel(x_ref, ya_ref, yb_ref, yc_ref, g_ref, g1_ref, wb_ref, wo_ref, lg_ref, lb_ref, o_ref):
    zt = (_sigmoid(g_ref[0, 0, 0:1024, :].astype(F32)) * _dot_nt(wb_ref[0], ya_ref[0])
          + _sigmoid(g_ref[0, 0, 1024:2048, :].astype(F32)) * _dot(wb_ref[1], yb_ref[0, 0])
          + _sigmoid(g_ref[0, 0, 2048:3072, :].astype(F32)) * _dot(wb_ref[2], yc_ref[0, 0]))
    u = _dot(wo_ref[...], zt.astype(BF)).T
    y = _ln_rows(DN_ALPHA * x_ref[0] + g1_ref[0] * u)
    o_ref[0] = y * lg_ref[...] + lb_ref[...]


def _merge(xall, ya, ybT, ycT, gatesT, mod_l, p, n_b, n_t, n_q):
    ltot = n_t * TILE
    tile_spec = lambda c: pl.BlockSpec((1, TILE, c), lambda b, t: (b, t, 0))
    fm_spec = lambda c: pl.BlockSpec((1, 1, c, TILE), lambda b, t: (b, t, 0, 0))
    mrow = lambda b, t: jnp.where(t == n_t - 1, n_b, b)
    return pl.pallas_call(
        _merge_kernel,
        grid=(n_b, n_q),
        in_specs=[tile_spec(D_MODEL), tile_spec(1024), fm_spec(1024), fm_spec(1024), fm_spec(3072),
                  pl.BlockSpec((1, 1, D_MODEL), lambda b, t: (mrow(b, t), 0, 2)),
                  _resident((3, D_MODEL, D_MODEL)), _resident((D_MODEL, D_MODEL)),
                  _resident((1, D_MODEL)), _resident((1, D_MODEL))],
        out_specs=tile_spec(D_MODEL),
        out_shape=jax.ShapeDtypeStruct((n_b, n_q * TILE, D_MODEL), F32),
        compiler_params=_cparams(("arbitrary", "arbitrary")),
        name="merge",
    )(xall, ya, ybT, ycT, gatesT, mod_l, p["wbT"], p["woT"], p["ln1_g"], p["ln1_b"])


def _ffn_kernel(x_ref, sh_ref, sc_ref, g2_ref, wi_ref, wo_ref, lg_ref, lb_ref, o_ref):
    x = x_ref[0]
    h = (_ln_rows(x) * (1.0 + sc_ref[0]) + sh_ref[0]).astype(BF)
    acc = jnp.zeros((TILE, D_MODEL), F32)
    for j in range(FFN_H // FFN_CHUNK):
        a = FFN_CHUNK * j
        gate = _dot(h, wi_ref[:, a:a + FFN_CHUNK])
        up = _dot(h, wi_ref[:, FFN_H + a:FFN_H + a + FFN_CHUNK])
        acc = acc + _dot((_silu(gate) * up).astype(BF), wo_ref[a:a + FFN_CHUNK, :])
    y = _ln_rows(DN_ALPHA * x + g2_ref[0] * acc)
    o_ref[0] = y * lg_ref[...] + lb_ref[...]


def _ffn(x1, mod_l, p, n_b, n_t, n_q):
    tile_spec = lambda c: pl.BlockSpec((1, TILE, c), lambda b, t: (b, t, 0))
    mrow = lambda b, t: jnp.where(t == n_t - 1, n_b, b)
    mspec = lambda j: pl.BlockSpec((1, 1, D_MODEL), lambda b, t: (mrow(b, t), 0, j))
    return pl.pallas_call(
        _ffn_kernel,
        grid=(n_b, n_q),
        in_specs=[tile_spec(D_MODEL), mspec(3), mspec(4), mspec(5),
                  _resident((D_MODEL, 2 * FFN_H)), _resident((FFN_H, D_MODEL)),
                  _resident((1, D_MODEL)), _resident((1, D_MODEL))],
        out_specs=tile_spec(D_MODEL),
        out_shape=jax.ShapeDtypeStruct((n_b, n_q * TILE, D_MODEL), F32),
        compiler_params=_cparams(("arbitrary", "arbitrary")),
        name="ffn",
    )(x1, mod_l, mod_l, mod_l, p["ffn_wi"], p["ffn_wo"], p["ln2_g"], p["ln2_b"])


def _swap32(w):
    k, n = w.shape
    return w.reshape(k, n // 64, 2, 32)[:, :, ::-1, :].reshape(k, n)


def _prep_layer(l, w_in, gla_w_a2, gla_b_a, mla_q_norm_g, mla_kv_norm_g, mla_w_uq, mla_w_ukv, w_branch, w_out,
                ln1_g, ln1_b, ffn_w_in, ffn_w_out, ln2_g, ln2_b):
    w = w_in[l]
    seg = lambda a, b: w[:, a:b]
    zc = lambda n: jnp.zeros((D_MODEL, n), F32)
    dk_w, mkr_w = seg(O_DK, O_DV), seg(O_MKR, O_GATES)
    w_tok = jnp.concatenate([
        seg(O_GQ, O_GK), seg(O_GK, O_GV), seg(O_GV, O_GR), seg(O_GR, O_GA), dk_w, _swap32(dk_w),
        seg(O_MQ, O_MKV), seg(O_MKV, O_MKR),
        mkr_w, seg(O_GA, O_DQ), zc(32), _swap32(mkr_w), zc(64)], axis=1).astype(BF)
    w_featT = jnp.concatenate([seg(O_DQ, O_DK) * DIFF_DH ** -0.5, seg(O_DV, O_MQ), seg(O_GATES, O_END)],
                              axis=1).T.astype(BF)
    wa2 = jnp.zeros((128, 1024), F32)
    wa2 = wa2.at[64:80, 0:512].set(gla_w_a2[l, 0]).at[80:96, 512:1024].set(gla_w_a2[l, 1]).astype(BF)
    ukv = mla_w_ukv[l].reshape(MLA_KVR, MLA_H, MLA_NOPE + MLA_DV)
    return dict(
        w_tok=w_tok, w_featT=w_featT, wa2=wa2, ba=gla_b_a[l].reshape(1, 1024),
        qg=mla_q_norm_g[l].reshape(1, MLA_QR), kvg=mla_kv_norm_g[l].reshape(1, MLA_KVR),
        wukv_k=ukv[:, :, :MLA_NOPE].reshape(MLA_KVR, 1024).astype(BF),
        wukv_vT=ukv[:, :, MLA_NOPE:].reshape(MLA_KVR, 1024).T.astype(BF),
        wuqT=mla_w_uq[l].T.astype(BF),
        wbT=jnp.swapaxes(w_branch[l], 1, 2).astype(BF), woT=w_out[l].T.astype(BF),
        ln1_g=ln1_g[l].reshape(1, D_MODEL), ln1_b=ln1_b[l].reshape(1, D_MODEL),
        ffn_wi=ffn_w_in[l].astype(BF), ffn_wo=ffn_w_out[l].astype(BF),
        ln2_g=ln2_g[l].reshape(1, D_MODEL), ln2_b=ln2_b[l].reshape(1, D_MODEL))


def _rope_tables(l_lat, l_ctx):
    rows = l_lat // GRID_W
    pos_row = jnp.broadcast_to(jnp.arange(rows, dtype=F32)[:, None], (rows, GRID_W)).reshape(l_lat)
    pos_col = jnp.broadcast_to(jnp.arange(GRID_W, dtype=F32)[None, :], (rows, GRID_W)).reshape(l_lat)
    d_axis = ROPE_DIM // 2
    inv = ROPE_BASE ** (-jnp.arange(0, d_axis, 2, dtype=F32) / d_axis)
    ang = jnp.concatenate([pos_row[:, None] * inv, pos_col[:, None] * inv], axis=-1)
    cos = jnp.concatenate([jnp.cos(ang), jnp.ones((l_ctx, 32), F32)], axis=0)
    sin = jnp.concatenate([jnp.sin(ang), jnp.zeros((l_ctx, 32), F32)], axis=0)
    ctok = jnp.tile(cos, (1, 4))
    stok = jnp.tile(jnp.concatenate([-sin, sin], axis=1), (1, 2))
    n_t = (l_lat + l_ctx) // TILE
    to_fm = lambda a: a.T.reshape(32, n_t, TILE).transpose(1, 0, 2)
    return ctok, stok, to_fm(cos), to_fm(sin)


def kernel(x, c, ctx, c_ctx, w_mod, b_mod, w_in, gla_w_a2, gla_b_a, gla_norm_g, diff_lam, diff_norm_g,
           mla_q_norm_g, mla_kv_norm_g, mla_w_uq, mla_w_ukv, w_branch, w_out, ln1_g, ln1_b, ffn_w_in, ffn_w_out,
           ln2_g, ln2_b):
    n_b, l_lat, _ = x.shape
    l_ctx = ctx.shape[1]
    assert l_ctx == TILE and n_b + 1 <= 16
    assert l_lat % (TILE * max(DIFF_QS, MLA_QS)) == 0 and l_lat % (TILE * KEY_TILES * KEY_UNROLL) == 0
    ltot = l_lat + l_ctx
    n_t = ltot // TILE
    ctok, stok, cosT, sinT = _rope_tables(l_lat, l_ctx)
    c_all = jnp.zeros((16, D_MODEL), F32).at[:n_b].set(c).at[n_b].set(c_ctx)
    mod = _modulation(c_all, w_mod, b_mod)
    ii = lax.broadcasted_iota(jnp.int32, (TILE, TILE), 0)
    jj = lax.broadcasted_iota(jnp.int32, (TILE, TILE), 1)
    same = (ii // GLA_CHUNK) == (jj // GLA_CHUNK)
    tri_lo = (same & (jj <= ii)).astype(BF)
    tri_up = (same & (jj >= ii)).astype(BF)
    xall = jnp.concatenate([x, ctx], axis=1)
    for l in range(N_LAYERS):
        last = l == N_LAYERS - 1
        n_q = n_t - 1 if last else n_t
        lam_init = 0.8 - 0.6 * math.exp(-0.3 * l)
        p = _prep_layer(l, w_in, gla_w_a2, gla_b_a, mla_q_norm_g, mla_kv_norm_g, mla_w_uq, mla_w_ukv, w_branch,
                        w_out, ln1_g, ln1_b, ffn_w_in, ffn_w_out, ln2_g, ln2_b)
        mod_l = mod[l].reshape(16, 1, 6 * D_MODEL)
        gq, gk, gv, gr, dk, la, mk, cq, ckv = _proj_tok(xall, mod_l, ctok, stok, p, n_b, n_t, ltot)
        dqT, dvT, gatesT, mqT, mvT = _proj_feat(xall, mod_l, cosT, sinT, cq, ckv, p, n_b, n_t)
        ya = _gla(gq, gk, gv, la, gr, tri_lo, tri_up, gla_norm_g[l].reshape(1, GLA_DV), n_b, n_t)
        gcol = jnp.broadcast_to(diff_norm_g[l].reshape(128, 1), (128, TILE))
        ybT = _diff_attn(dqT, dk, dvT, diff_lam[l], gcol, n_b, n_t, lam_init)
        ycT = _mla_attn(mqT, mk, mvT, n_b, n_t)
        if not last:
            ybT = _diff_attn(dqT, dk, dvT, diff_lam[l], gcol, n_b, n_t, lam_init, ctx_into=ybT)
            ycT = _mla_attn(mqT, mk, mvT, n_b, n_t, ctx_into=ycT)
        x1 = _merge(xall, ya, ybT, ycT, gatesT, mod_l, p, n_b, n_t, n_q)
        xall = _ffn(x1, mod_l, p, n_b, n_t, n_q)
    return xall
```

```python
import functools
import math

import jax
import jax.numpy as jnp
from jax import lax
from jax.experimental import pallas as pl
from jax.experimental.pallas import tpu as pltpu

BF = jnp.bfloat16
F32 = jnp.float32

D_MODEL = 1024
N_LAYERS = 2
GRID_W = 64
TILE = 256
GLA_H, GLA_DK, GLA_DV, GLA_RANK, GLA_TAU = 4, 128, 256, 16, 16.0
GLA_CHUNK, GLA_SUB = 64, 16
DIFF_H, DIFF_DH = 8, 64
MLA_H, MLA_QR, MLA_KVR, MLA_NOPE, MLA_ROPE, MLA_DV = 8, 256, 128, 128, 64, 128
MLA_SCALE = (MLA_NOPE + MLA_ROPE) ** -0.5
ROPE_DIM, ROPE_BASE = 64, 10000.0
FFN_H = 2816
FFN_CHUNK = 1408
DN_ALPHA = (2 * N_LAYERS) ** 0.25
EPS = 1e-6
DIFF_QS, MLA_QS = 2, 4
KEY_TILES = 2
KEY_UNROLL = 4
LOG2E = math.log2(math.e)
ONES_ROWS = 16
VMEM_LIMIT = 56 * 1024 * 1024

_SIZES = (512, 512, 1024, 1024, 32, 1024, 1024, 1024, 256, 128, 64, 3072)
_OFF = [0]
for _s in _SIZES:
    _OFF.append(_OFF[-1] + _s)
(O_GQ, O_GK, O_GV, O_GR, O_GA, O_DQ, O_DK, O_DV, O_MQ, O_MKV, O_MKR, O_GATES, O_END) = _OFF

T_GQ, T_GK, T_GV, T_GR, T_DK, T_DKS, T_MQ, T_MKV, T_SMA, T_SMB, T_END = (
    0, 512, 1024, 2048, 3072, 4096, 5120, 5376, 5504, 5632, 5760)
F_DQ, F_DV, F_GATES, F_END = 0, 1024, 2048, 5120


def _cparams(sem):
    return pltpu.CompilerParams(dimension_semantics=sem, vmem_limit_bytes=VMEM_LIMIT)


def _resident(shape):
    nd = len(shape)
    return pl.BlockSpec(shape, lambda *_: (0,) * nd, pipeline_mode=pl.Buffered(1))


def _dot(a, b):
    return jnp.dot(a, b, preferred_element_type=F32)


def _dot_nt(a, b):
    return lax.dot_general(a, b, (((1,), (1,)), ((), ())), preferred_element_type=F32)


def _dot_tn(a, b):
    return lax.dot_general(a, b, (((0,), (0,)), ((), ())), preferred_element_type=F32)


def _ln_rows(x):
    mu = jnp.mean(x, axis=-1, keepdims=True)
    xc = x - mu
    var = jnp.mean(xc * xc, axis=-1, keepdims=True)
    return xc * lax.rsqrt(var + EPS)


def _sigmoid(x):
    return 1.0 / (1.0 + jnp.exp(-x))


def _silu(x):
    return x * _sigmoid(x)


def _mod_kernel(c_ref, w_ref, b_ref, o_ref):
    ca = _silu(c_ref[...]).astype(BF)
    o_ref[0] = _dot(ca, w_ref[0].astype(BF)) + b_ref[0]


def _modulation(c_all, w_mod, b_mod):
    nl, d, n6 = w_mod.shape
    r = c_all.shape[0]
    nblk = 1536
    return pl.pallas_call(
        _mod_kernel,
        grid=(nl, n6 // nblk),
        in_specs=[pl.BlockSpec((r, d), lambda l, j: (0, 0)),
                  pl.BlockSpec((1, d, nblk), lambda l, j: (l, 0, j)),
                  pl.BlockSpec((1, 1, nblk), lambda l, j: (l, 0, j))],
        out_specs=pl.BlockSpec((1, r, nblk), lambda l, j: (l, 0, j)),
        out_shape=jax.ShapeDtypeStruct((nl, r, n6), F32),
        compiler_params=_cparams(("arbitrary", "arbitrary")),
        name="modulation",
    )(c_all, w_mod, b_mod.reshape(nl, 1, n6))


def _proj_tok_kernel(x_ref, sh_ref, sc_ref, ct_ref, st_ref, w_ref, wa2_ref, ba_ref, qg_ref, kvg_ref, wk_ref,
                     gq_ref, gk_ref, gv_ref, gr_ref, dk_ref, la_ref, mk_ref, cq_ref, ckv_ref):
    h = (_ln_rows(x_ref[0]) * (1.0 + sc_ref[0]) + sh_ref[0]).astype(BF)

    def proj(a, b):
        return _dot(h, w_ref[:, a:b])

    gq_ref[0] = (proj(T_GQ, T_GK) * GLA_DK ** -0.5).astype(BF)
    gk_ref[0] = proj(T_GK, T_GV).astype(BF)
    gv_ref[0] = proj(T_GV, T_GR).astype(BF)
    gr_ref[0] = proj(T_GR, T_DK).astype(BF)
    ct = ct_ref[...]
    st = st_ref[...]
    dkx = proj(T_DK, T_DKS)
    dks = proj(T_DKS, T_MQ)
    for hh in range(DIFF_H):
        sl = slice(128 * hh, 128 * (hh + 1))
        dk_ref[0, :, sl] = (dkx[:, sl] * ct + dks[:, sl] * st).astype(BF)
    mq = proj(T_MQ, T_MKV)
    cq = mq * lax.rsqrt(jnp.mean(mq * mq, axis=-1, keepdims=True) + EPS) * qg_ref[...]
    cq_ref[0] = cq.astype(BF)
    mkv = proj(T_MKV, T_SMA)
    ckv = (mkv * lax.rsqrt(jnp.mean(mkv * mkv, axis=-1, keepdims=True) + EPS) * kvg_ref[...]).astype(BF)
    ckv_ref[0] = ckv
    kn = _dot(ckv, wk_ref[...]).astype(BF)
    sm = proj(T_SMA, T_END)
    sma = sm[:, 0:128]
    smb = sm[:, 128:256]
    kr = (sma[:, 0:64] * ct[:, 0:64] + smb[:, 0:64] * st[:, 0:64]).astype(BF)
    for hh in range(MLA_H):
        mk_ref[0, hh, :, 0:128] = kn[:, 128 * hh:128 * (hh + 1)]
        mk_ref[0, hh, :, 128:192] = kr
    z = _dot(sma.astype(BF), wa2_ref[...]) + ba_ref[...]
    la_ref[0] = (jnp.minimum(z, 0.0) - jnp.log(1.0 + jnp.exp(-jnp.abs(z)))) * (1.0 / GLA_TAU)


def _proj_tok(xall, mod_l, ctok, stok, p, n_b, n_t, ltot):
    tile_spec = lambda c: pl.BlockSpec((1, TILE, c), lambda b, t: (b, t, 0))
    mrow = lambda b, t: jnp.where(t == n_t - 1, n_b, b)
    bf_out = lambda c: jax.ShapeDtypeStruct((n_b, ltot, c), BF)
    return pl.pallas_call(
        _proj_tok_kernel,
        grid=(n_b, n_t),
        in_specs=[tile_spec(D_MODEL),
                  pl.BlockSpec((1, 1, D_MODEL), lambda b, t: (mrow(b, t), 0, 0)),
                  pl.BlockSpec((1, 1, D_MODEL), lambda b, t: (mrow(b, t), 0, 1)),
                  pl.BlockSpec((TILE, 128), lambda b, t: (t, 0)),
                  pl.BlockSpec((TILE, 128), lambda b, t: (t, 0)),
                  _resident((D_MODEL, T_END)), _resident((128, 1024)), _resident((1, 1024)),
                  _resident((1, MLA_QR)), _resident((1, MLA_KVR)), _resident((MLA_KVR, 1024))],
        out_specs=[tile_spec(512), tile_spec(512), tile_spec(1024), tile_spec(1024), tile_spec(1024),
                   tile_spec(1024),
                   pl.BlockSpec((1, MLA_H, TILE, 192), lambda b, t: (b, 0, t, 0)),
                   tile_spec(MLA_QR), tile_spec(MLA_KVR)],
        out_shape=[bf_out(512), bf_out(512), bf_out(1024), bf_out(1024), bf_out(1024),
                   jax.ShapeDtypeStruct((n_b, ltot, 1024), F32),
                   jax.ShapeDtypeStruct((n_b, MLA_H, ltot, 192), BF),
                   bf_out(MLA_QR), bf_out(MLA_KVR)],
        compiler_params=_cparams(("arbitrary", "arbitrary")),
        name="proj_tok",
    )(xall, mod_l, mod_l, ctok, stok, p["w_tok"], p["wa2"], p["ba"], p["qg"], p["kvg"], p["wukv_k"])


def _rope_rows(x1, x2, cos, sin):
    return x1 * cos - x2 * sin, x1 * sin + x2 * cos


def _proj_feat_kernel(x_ref, sh_ref, sc_ref, cos_ref, sin_ref, cq_ref, ckv_ref, w_ref, wuq_ref, wv_ref,
                      dq_ref, dv_ref, g_ref, mq_ref, mv_ref):
    h = (_ln_rows(x_ref[0]) * (1.0 + sc_ref[0]) + sh_ref[0]).astype(BF)
    cos = cos_ref[0]
    sin = sin_ref[0]
    dq = _dot_nt(w_ref[F_DQ:F_DV, :], h) * LOG2E
    for g in range(2 * DIFF_H):
        o1, o2 = _rope_rows(dq[64 * g:64 * g + 32], dq[64 * g + 32:64 * (g + 1)], cos, sin)
        dq_ref[0, 0, 64 * g:64 * g + 32, :] = o1.astype(BF)
        dq_ref[0, 0, 64 * g + 32:64 * (g + 1), :] = o2.astype(BF)
    dv_ref[0, 0] = _dot_nt(w_ref[F_DV:F_GATES, :], h).astype(BF)
    for j in range(3):
        a = F_GATES + 1024 * j
        g_ref[0, 0, 1024 * j:1024 * (j + 1), :] = _dot_nt(w_ref[a:a + 1024, :], h).astype(BF)
    cq = cq_ref[0]
    mq = _dot_nt(wuq_ref[...], cq) * (MLA_SCALE * LOG2E)
    for hh in range(MLA_H):
        a = 192 * hh
        o1, o2 = _rope_rows(mq[a + 128:a + 160], mq[a + 160:a + 192], cos, sin)
        mq_ref[0, 0, a:a + 128, :] = mq[a:a + 128].astype(BF)
        mq_ref[0, 0, a + 128:a + 160, :] = o1.astype(BF)
        mq_ref[0, 0, a + 160:a + 192, :] = o2.astype(BF)
    mv_ref[0, 0] = _dot_nt(wv_ref[...], ckv_ref[0]).astype(BF)


def _proj_feat(xall, mod_l, cosT, sinT, cq, ckv, p, n_b, n_t):
    tile_spec = lambda c: pl.BlockSpec((1, TILE, c), lambda b, t: (b, t, 0))
    mrow = lambda b, t: jnp.where(t == n_t - 1, n_b, b)
    fm_spec = lambda c: pl.BlockSpec((1, 1, c, TILE), lambda b, t: (b, t, 0, 0))
    fm_out = lambda c: jax.ShapeDtypeStruct((n_b, n_t, c, TILE), BF)
    return pl.pallas_call(
        _proj_feat_kernel,
        grid=(n_b, n_t),
        in_specs=[tile_spec(D_MODEL),
                  pl.BlockSpec((1, 1, D_MODEL), lambda b, t: (mrow(b, t), 0, 0)),
                  pl.BlockSpec((1, 1, D_MODEL), lambda b, t: (mrow(b, t), 0, 1)),
                  pl.BlockSpec((1, 32, TILE), lambda b, t: (t, 0, 0)),
                  pl.BlockSpec((1, 32, TILE), lambda b, t: (t, 0, 0)),
                  tile_spec(MLA_QR), tile_spec(MLA_KVR),
                  _resident((F_END, D_MODEL)), _resident((MLA_H * 192, MLA_QR)), _resident((1024, MLA_KVR))],
        out_specs=[fm_spec(1024), fm_spec(1024), fm_spec(3072), fm_spec(MLA_H * 192), fm_spec(1024)],
        out_shape=[fm_out(1024), fm_out(1024), fm_out(3072), fm_out(MLA_H * 192), fm_out(1024)],
        compiler_params=_cparams(("arbitrary", "arbitrary")),
        name="proj_feat",
    )(xall, mod_l, mod_l, cosT, sinT, cq, ckv, p["w_featT"], p["wuqT"], p["wukv_vT"])


def _flash_T(k_ref, v_ref, q2, m_ref, acc_ref, s_ref, n_lat, latent_keys):
    dv = v_ref.shape[2]
    m_ref[...] = jnp.full(m_ref.shape, -jnp.inf, F32)
    acc_ref[...] = jnp.zeros(acc_ref.shape, F32)
    rows = KEY_TILES * TILE

    def scores(slot, tile0):
        s_ref[slot] = _dot(k_ref[0, pl.ds(pl.multiple_of(tile0 * TILE, TILE), rows), :], q2)

    def consume(s, tile0, ntiles):
        m_old = m_ref[...]
        m_new = jnp.maximum(m_old, jnp.max(s, axis=0, keepdims=True))
        alpha = jnp.exp2(m_old - m_new)
        p = jnp.exp2(s - m_new).astype(BF)
        v = jnp.concatenate([v_ref[0, tile0 + i] for i in range(ntiles)], axis=1)
        vext = jnp.concatenate([v, jnp.ones((ONES_ROWS, ntiles * TILE), BF)], axis=0)
        acc_ref[...] = acc_ref[...] * alpha + _dot(vext, p)
        m_ref[...] = m_new

    if latent_keys:
        n_steps = n_lat // KEY_TILES
        scores(0, 0)

        def body(jj, carry):
            for u in range(KEY_UNROLL):
                j = KEY_UNROLL * jj + u
                scores((u + 1) % 2, (j + 1) * KEY_TILES)
                consume(s_ref[u % 2], j * KEY_TILES, KEY_TILES)
            return carry
        lax.fori_loop(0, n_steps // KEY_UNROLL - 1, body, 0)
        for u in range(KEY_UNROLL):
            j = n_steps - KEY_UNROLL + u
            if u < KEY_UNROLL - 1:
                scores((u + 1) % 2, (j + 1) * KEY_TILES)
            else:
                s_ctx = _dot(k_ref[0, pl.ds(n_lat * TILE, TILE), :], q2)
            consume(s_ref[u % 2], j * KEY_TILES, KEY_TILES)
        consume(s_ctx, n_lat, 1)
    else:
        consume(_dot(k_ref[0, pl.ds(0, TILE), :], q2), 0, 1)
    acc = acc_ref[...]
    return acc[0:dv] * (1.0 / acc[dv:dv + 1])


def _attn_call(kern, name, q_rows, k_spec_fn, n_b, n_heads, n_t, qs, scratch, qT, k, vT, extra, extra_specs,
               ctx_into=None):
    n_lat = n_t - 1
    if ctx_into is None:
        grid = (n_b, n_heads, n_lat // qs)
        q_spec = pl.BlockSpec((1, qs, q_rows, TILE), lambda b, h, q: (b, q, h, 0))
        k_spec = k_spec_fn(n_t * TILE, 0)
        v_spec = pl.BlockSpec((1, n_t, 128, TILE), lambda b, h, q: (b, 0, h, 0))
        o_spec = pl.BlockSpec((1, qs, 128, TILE), lambda b, h, q: (b, q, h, 0))
        alias_in, alias_specs, aliases = [], [], {}
    else:
        grid = (n_b, n_heads, 1)
        q_spec = pl.BlockSpec((1, 1, q_rows, TILE), lambda b, h, q: (b, n_lat, h, 0))
        k_spec = k_spec_fn(TILE, n_lat)
        v_spec = pl.BlockSpec((1, 1, 128, TILE), lambda b, h, q: (b, n_lat, h, 0))
        o_spec = pl.BlockSpec((1, 1, 128, TILE), lambda b, h, q: (b, n_lat, h, 0))
        alias_in, alias_specs = [ctx_into], [pl.BlockSpec(memory_space=pl.ANY)]
        aliases = {3 + len(extra): 0}
    return pl.pallas_call(
        kern,
        grid=grid,
        in_specs=[q_spec, k_spec, v_spec] + extra_specs + alias_specs,
        out_specs=o_spec,
        out_shape=jax.ShapeDtypeStruct((n_b, n_t, 1024, TILE), BF),
        scratch_shapes=scratch,
        input_output_aliases=aliases,
        compiler_params=_cparams(("arbitrary", "arbitrary", "arbitrary")),
        name=name,
    )(qT, k, vT, *extra, *alias_in)


def _diff_attn_kernel(q_ref, k_ref, v_ref, lam_ref, g_ref, *rest, n_lat, lam_init, qs, latent):
    o_ref, q2_ref, m_ref, acc_ref, s_ref = rest[-5:]
    nq = qs * TILE
    zeros = jnp.zeros((DIFF_DH, TILE), BF)
    for i in range(qs):
        q = q_ref[0, i]
        q2_ref[0:64, i * TILE:(i + 1) * TILE] = q[0:64]
        q2_ref[64:128, i * TILE:(i + 1) * TILE] = zeros
        q2_ref[0:64, nq + i * TILE:nq + (i + 1) * TILE] = zeros
        q2_ref[64:128, nq + i * TILE:nq + (i + 1) * TILE] = q[64:128]
    o = _flash_T(k_ref, v_ref, q2_ref[...], m_ref, acc_ref, s_ref, n_lat, latent)
    dl = lam_ref[...]
    lam = (jnp.exp(jnp.sum(dl[0:1] * dl[1:2], axis=1, keepdims=True))
           - jnp.exp(jnp.sum(dl[2:3] * dl[3:4], axis=1, keepdims=True)) + lam_init)
    od = o[:, 0:nq] - lam * o[:, nq:2 * nq]
    y = od * lax.rsqrt(jnp.mean(od * od, axis=0, keepdims=True) + EPS) * (1.0 - lam_init)
    for i in range(qs):
        o_ref[0, i] = (y[:, i * TILE:(i + 1) * TILE] * g_ref[...]).astype(BF)


def _diff_attn(dqT, dk, dvT, dlam, gcol, n_b, n_t, lam_init, ctx_into=None):
    qs = DIFF_QS if ctx_into is None else 1
    kern = functools.partial(_diff_attn_kernel, n_lat=n_t - 1, lam_init=lam_init, qs=qs, latent=ctx_into is None)
    k_spec_fn = lambda rows, blk: pl.BlockSpec((1, rows, 128), lambda b, h, q: (b, blk, h))
    scratch = [pltpu.VMEM((128, 2 * qs * TILE), BF), pltpu.VMEM((1, 2 * qs * TILE), F32),
               pltpu.VMEM((128 + ONES_ROWS, 2 * qs * TILE), F32),
               pltpu.VMEM((2, KEY_TILES * TILE, 2 * qs * TILE), F32)]
    extra_specs = [pl.BlockSpec((4, DIFF_DH), lambda b, h, q: (0, 0)),
                   pl.BlockSpec((128, TILE), lambda b, h, q: (0, 0))]
    return _attn_call(kern, "diff_attn" if ctx_into is None else "diff_attn_ctx", 128, k_spec_fn, n_b, DIFF_H, n_t,
                      qs, scratch, dqT, dk, dvT, [dlam, gcol], extra_specs, ctx_into)


def _mla_attn_kernel(q_ref, k_ref, v_ref, *rest, n_lat, qs, latent):
    o_ref, q2_ref, m_ref, acc_ref, s_ref = rest[-5:]
    for i in range(qs):
        q2_ref[:, i * TILE:(i + 1) * TILE] = q_ref[0, i]
    o = _flash_T(k_ref.at[0], v_ref, q2_ref[...], m_ref, acc_ref, s_ref, n_lat, latent)
    for i in range(qs):
        o_ref[0, i] = o[:, i * TILE:(i + 1) * TILE].astype(BF)


def _mla_attn(mqT, mk, mvT, n_b, n_t, ctx_into=None):
    qs = MLA_QS if ctx_into is None else 1
    kern = functools.partial(_mla_attn_kernel, n_lat=n_t - 1, qs=qs, latent=ctx_into is None)
    k_spec_fn = lambda rows, blk: pl.BlockSpec((1, 1, rows, 192), lambda b, h, q: (b, h, blk, 0))
    scratch = [pltpu.VMEM((192, qs * TILE), BF), pltpu.VMEM((1, qs * TILE), F32),
               pltpu.VMEM((128 + ONES_ROWS, qs * TILE), F32),
               pltpu.VMEM((2, KEY_TILES * TILE, qs * TILE), F32)]
    return _attn_call(kern, "mla_attn" if ctx_into is None else "mla_attn_ctx", 192, k_spec_fn, n_b, MLA_H, n_t,
                      qs, scratch, mqT, mk, mvT, [], [], ctx_into)


def _gla_tile(q_ref, k_ref, v_ref, la_ref, tri_ref, st_ref, reverse):
    g = la_ref[0]
    g1 = g.astype(BF)
    r1 = g - g1.astype(F32)
    g2 = r1.astype(BF)
    tri = tri_ref[...]
    bcum = _dot(tri, g1) + _dot(tri, g2)
    n_chunk = TILE // GLA_CHUNK
    n_sub = GLA_CHUNK // GLA_SUB
    rows64 = lax.broadcasted_iota(jnp.int32, (GLA_CHUNK, GLA_DK), 0)
    r16 = lax.broadcasted_iota(jnp.int32, (GLA_SUB, GLA_CHUNK), 0)
    c16 = lax.broadcasted_iota(jnp.int32, (GLA_SUB, GLA_CHUNK), 1)
    outs = []
    for hh in range(GLA_H):
        kc0, kc1 = GLA_DK * hh, GLA_DK * (hh + 1)
        o_chunks = [None] * n_chunk
        st = st_ref[hh]
        for c in (range(n_chunk - 1, -1, -1) if reverse else range(n_chunk)):
            r0 = GLA_CHUNK * c
            bc = bcum[r0:r0 + GLA_CHUNK, kc0:kc1]
            qc = q_ref[0, r0:r0 + GLA_CHUNK, kc0:kc1].astype(F32)
            kc = k_ref[0, r0:r0 + GLA_CHUNK, kc0:kc1].astype(F32)
            vc = v_ref[0, r0:r0 + GLA_CHUNK, GLA_DV * hh:GLA_DV * (hh + 1)]
            b_tot = bc[0:1] if reverse else bc[GLA_CHUNK - 1:GLA_CHUNK]
            a_rows = []
            for i in range(n_sub):
                s0 = GLA_SUB * i
                ref_row = bc[s0 + GLA_SUB - 1:s0 + GLA_SUB] if reverse else bc[s0:s0 + 1]
                qs = (qc[s0:s0 + GLA_SUB] * jnp.exp(bc[s0:s0 + GLA_SUB] - ref_row)).astype(BF)
                valid = (rows64 >= s0) if reverse else (rows64 < s0 + GLA_SUB)
                ks = (kc * jnp.exp(jnp.where(valid, ref_row - bc, -jnp.inf))).astype(BF)
                a = _dot_nt(qs, ks)
                keep = (c16 > r16 + s0) if reverse else (c16 <= r16 + s0)
                a_rows.append(jnp.where(keep, a, 0.0))
            a_mat = jnp.concatenate(a_rows, axis=0).astype(BF)
            q_in = (qc * jnp.exp(bc)).astype(BF)
            o_chunks[c] = _dot(a_mat, vc) + _dot_nt(q_in, st.astype(BF))
            k_d = (kc * jnp.exp(b_tot - bc)).astype(BF)
            st = st * jnp.exp(b_tot) + _dot_tn(vc, k_d)
        st_ref[hh] = st
        outs.append(jnp.concatenate(o_chunks, axis=0))
    return jnp.concatenate(outs, axis=1)


def _gla_fwd_kernel(q_ref, k_ref, v_ref, la_ref, tri_ref, o_ref, st_ref):
    @pl.when(pl.program_id(1) == 0)
    def _():
        st_ref[...] = jnp.zeros(st_ref.shape, F32)
    o_ref[0] = _gla_tile(q_ref, k_ref, v_ref, la_ref, tri_ref, st_ref, False)


def _gla_bwd_kernel(q_ref, k_ref, v_ref, la_ref, tri_ref, of_ref, r_ref, g_ref, y_ref, st_ref):
    @pl.when(pl.program_id(1) == 0)
    def _():
        st_ref[...] = jnp.zeros(st_ref.shape, F32)
    o = of_ref[0] + _gla_tile(q_ref, k_ref, v_ref, la_ref, tri_ref, st_ref, True)
    gn = g_ref[...]
    for hh in range(GLA_H):
        sl = slice(GLA_DV * hh, GLA_DV * (hh + 1))
        oh = o[:, sl]
        yh = oh * lax.rsqrt(jnp.mean(oh * oh, axis=-1, keepdims=True) + EPS) * gn
        y_ref[0, :, sl] = (yh * _silu(r_ref[0, :, sl].astype(F32))).astype(BF)


def _gla(gq, gk, gv, la, gr, tri_lo, tri_up, gnorm, n_b, n_t):
    ltot = n_t * TILE
    t_fwd = lambda s: (s + n_t - 1) % n_t
    t_bwd = lambda s: jnp.where(s == 0, n_t - 1, n_t - 1 - s)

    def specs(tmap, dirn):
        ts = lambda c, cb=0: pl.BlockSpec((1, TILE, c), lambda b, s: (b, tmap(s), cb))
        return [ts(512), ts(512), ts(1024), ts(512, dirn), pl.BlockSpec((TILE, TILE), lambda b, s: (0, 0))]

    o_f = pl.pallas_call(
        _gla_fwd_kernel,
        grid=(n_b, n_t),
        in_specs=specs(t_fwd, 0),
        out_specs=pl.BlockSpec((1, TILE, 1024), lambda b, s: (b, t_fwd(s), 0)),
        out_shape=jax.ShapeDtypeStruct((n_b, ltot, 1024), F32),
        scratch_shapes=[pltpu.VMEM((GLA_H, GLA_DV, GLA_DK), F32)],
        compiler_params=_cparams(("arbitrary", "arbitrary")),
        name="gla_fwd",
    )(gq, gk, gv, la, tri_lo)
    tsb = lambda c: pl.BlockSpec((1, TILE, c), lambda b, s: (b, t_bwd(s), 0))
    return pl.pallas_call(
        _gla_bwd_kernel,
        grid=(n_b, n_t),
        in_specs=specs(t_bwd, 1) + [tsb(1024), tsb(1024), pl.BlockSpec((1, GLA_DV), lambda b, s: (0, 0))],
        out_specs=tsb(1024),
        out_shape=jax.ShapeDtypeStruct((n_b, ltot, 1024), BF),
        scratch_shapes=[pltpu.VMEM((GLA_H, GLA_DV, GLA_DK), F32)],
        compiler_params=_cparams(("arbitrary", "arbitrary")),
        name="gla_bwd",
    )(gq, gk, gv, la, tri_up, o_f, gr, gnorm)


def _merge_kernel(x_ref, ya_ref, yb_ref, yc_ref, g_ref, g1_ref, wb_ref, wo_ref, lg_ref, lb_ref, o_ref):
    zt = (_sigmoid(g_ref[0, 0, 0:1024, :].astype(F32)) * _dot_nt(wb_ref[0], ya_ref[0])
          + _sigmoid(g_ref[0, 0, 1024:2048, :].astype(F32)) * _dot(wb_ref[1], yb_ref[0, 0])
          + _sigmoid(g_ref[0, 0, 2048:3072, :].astype(F32)) * _dot(wb_ref[2], yc_ref[0, 0]))
    u = _dot(wo_ref[...], zt.astype(BF)).T
    y = _ln_rows(DN_ALPHA * x_ref[0] + g1_ref[0] * u)
    o_ref[0] = y * lg_ref[...] + lb_ref[...]


def _merge(xall, ya, ybT, ycT, gatesT, mod_l, p, n_b, n_t, n_q):
    ltot = n_t * TILE
    tile_spec = lambda c: pl.BlockSpec((1, TILE, c), lambda b, t: (b, t, 0))
    fm_spec = lambda c: pl.BlockSpec((1, 1, c, TILE), lambda b, t: (b, t, 0, 0))
    mrow = lambda b, t: jnp.where(t == n_t - 1, n_b, b)
    return pl.pallas_call(
        _merge_kernel,
        grid=(n_b, n_q),
        in_specs=[tile_spec(D_MODEL), tile_spec(1024), fm_spec(1024), fm_spec(1024), fm_spec(3072),
                  pl.BlockSpec((1, 1, D_MODEL), lambda b, t: (mrow(b, t), 0, 2)),
                  _resident((3, D_MODEL, D_MODEL)), _resident((D_MODEL, D_MODEL)),
                  _resident((1, D_MODEL)), _resident((1, D_MODEL))],
        out_specs=tile_spec(D_MODEL),
        out_shape=jax.ShapeDtypeStruct((n_b, n_q * TILE, D_MODEL), F32),
        compiler_params=_cparams(("arbitrary", "arbitrary")),
        name="merge",
    )(xall, ya, ybT, ycT, gatesT, mod_l, p["wbT"], p["woT"], p["ln1_g"], p["ln1_b"])


def _ffn_kernel(x_ref, sh_ref, sc_ref, g2_ref, wi_ref, wo_ref, lg_ref, lb_ref, o_ref):
    x = x_ref[0]
    h = (_ln_rows(x) * (1.0 + sc_ref[0]) + sh_ref[0]).astype(BF)
    acc = jnp.zeros((TILE, D_MODEL), F32)
    for j in range(FFN_H // FFN_CHUNK):
        a = FFN_CHUNK * j
        gate = _dot(h, wi_ref[:, a:a + FFN_CHUNK])
        up = _dot(h, wi_ref[:, FFN_H + a:FFN_H + a + FFN_CHUNK])
        acc = acc + _dot((_silu(gate) * up).astype(BF), wo_ref[a:a + FFN_CHUNK, :])
    y = _ln_rows(DN_ALPHA * x + g2_ref[0] * acc)
    o_ref[0] = y * lg_ref[...] + lb_ref[...]


def _ffn(x1, mod_l, p, n_b, n_t, n_q):
    tile_spec = lambda c: pl.BlockSpec((1, TILE, c), lambda b, t: (b, t, 0))
    mrow = lambda b, t: jnp.where(t == n_t - 1, n_b, b)
    mspec = lambda j: pl.BlockSpec((1, 1, D_MODEL), lambda b, t: (mrow(b, t), 0, j))
    return pl.pallas_call(
        _ffn_kernel,
        grid=(n_b, n_q),
        in_specs=[tile_spec(D_MODEL), mspec(3), mspec(4), mspec(5),
                  _resident((D_MODEL, 2 * FFN_H)), _resident((FFN_H, D_MODEL)),
                  _resident((1, D_MODEL)), _resident((1, D_MODEL))],
        out_specs=tile_spec(D_MODEL),
        out_shape=jax.ShapeDtypeStruct((n_b, n_q * TILE, D_MODEL), F32),
        compiler_params=_cparams(("arbitrary", "arbitrary")),
        name="ffn",
    )(x1, mod_l, mod_l, mod_l, p["ffn_wi"], p["ffn_wo"], p["ln2_g"], p["ln2_b"])


def _swap32(w):
    k, n = w.shape
    return w.reshape(k, n // 64, 2, 32)[:, :, ::-1, :].reshape(k, n)


def _prep_layer(l, w_in, gla_w_a2, gla_b_a, mla_q_norm_g, mla_kv_norm_g, mla_w_uq, mla_w_ukv, w_branch, w_out,
                ln1_g, ln1_b, ffn_w_in, ffn_w_out, ln2_g, ln2_b):
    w = w_in[l]
    seg = lambda a, b: w[:, a:b]
    zc = lambda n: jnp.zeros((D_MODEL, n), F32)
    dk_w, mkr_w = seg(O_DK, O_DV), seg(O_MKR, O_GATES)
    w_tok = jnp.concatenate([
        seg(O_GQ, O_GK), seg(O_GK, O_GV), seg(O_GV, O_GR), seg(O_GR, O_GA), dk_w, _swap32(dk_w),
        seg(O_MQ, O_MKV), seg(O_MKV, O_MKR),
        mkr_w, seg(O_GA, O_DQ), zc(32), _swap32(mkr_w), zc(64)], axis=1).astype(BF)
    w_featT = jnp.concatenate([seg(O_DQ, O_DK) * DIFF_DH ** -0.5, seg(O_DV, O_MQ), seg(O_GATES, O_END)],
                              axis=1).T.astype(BF)
    wa2 = jnp.zeros((128, 1024), F32)
    wa2 = wa2.at[64:80, 0:512].set(gla_w_a2[l, 0]).at[80:96, 512:1024].set(gla_w_a2[l, 1]).astype(BF)
    ukv = mla_w_ukv[l].reshape(MLA_KVR, MLA_H, MLA_NOPE + MLA_DV)
    return dict(
        w_tok=w_tok, w_featT=w_featT, wa2=wa2, ba=gla_b_a[l].reshape(1, 1024),
        qg=mla_q_norm_g[l].reshape(1, MLA_QR), kvg=mla_kv_norm_g[l].reshape(1, MLA_KVR),
        wukv_k=ukv[:, :, :MLA_NOPE].reshape(MLA_KVR, 1024).astype(BF),
        wukv_vT=ukv[:, :, MLA_NOPE:].reshape(MLA_KVR, 1024).T.astype(BF),
        wuqT=mla_w_uq[l].T.astype(BF),
        wbT=jnp.swapaxes(w_branch[l], 1, 2).astype(BF), woT=w_out[l].T.astype(BF),
        ln1_g=ln1_g[l].reshape(1, D_MODEL), ln1_b=ln1_b[l].reshape(1, D_MODEL),
        ffn_wi=ffn_w_in[l].astype(BF), ffn_wo=ffn_w_out[l].astype(BF),
        ln2_g=ln2_g[l].reshape(1, D_MODEL), ln2_b=ln2_b[l].reshape(1, D_MODEL))


def _rope_tables(l_lat, l_ctx):
    rows = l_lat // GRID_W
    pos_row = jnp.broadcast_to(jnp.arange(rows, dtype=F32)[:, None], (rows, GRID_W)).reshape(l_lat)
    pos_col = jnp.broadcast_to(jnp.arange(GRID_W, dtype=F32)[None, :], (rows, GRID_W)).reshape(l_lat)
    d_axis = ROPE_DIM // 2
    inv = ROPE_BASE ** (-jnp.arange(0, d_axis, 2, dtype=F32) / d_axis)
    ang = jnp.concatenate([pos_row[:, None] * inv, pos_col[:, None] * inv], axis=-1)
    cos = jnp.concatenate([jnp.cos(ang), jnp.ones((l_ctx, 32), F32)], axis=0)
    sin = jnp.concatenate([jnp.sin(ang), jnp.zeros((l_ctx, 32), F32)], axis=0)
    ctok = jnp.tile(cos, (1, 4))
    stok = jnp.tile(jnp.concatenate([-sin, sin], axis=1), (1, 2))
    n_t = (l_lat + l_ctx) // TILE
    to_fm = lambda a: a.T.reshape(32, n_t, TILE).transpose(1, 0, 2)
    return ctok, stok, to_fm(cos), to_fm(sin)


def kernel(x, c, ctx, c_ctx, w_mod, b_mod, w_in, gla_w_a2, gla_b_a, gla_norm_g, diff_lam, diff_norm_g,
           mla_q_norm_g, mla_kv_norm_g, mla_w_uq, mla_w_ukv, w_branch, w_out, ln1_g, ln1_b, ffn_w_in, ffn_w_out,
           ln2_g, ln2_b):
    n_b, l_lat, _ = x.shape
    l_ctx = ctx.shape[1]
    assert l_ctx == TILE and n_b + 1 <= 16
    assert l_lat % (TILE * max(DIFF_QS, MLA_QS)) == 0 and l_lat % (TILE * KEY_TILES * KEY_UNROLL) == 0
    ltot = l_lat + l_ctx
    n_t = ltot // TILE
    ctok, stok, cosT, sinT = _rope_tables(l_lat, l_ctx)
    c_all = jnp.zeros((16, D_MODEL), F32).at[:n_b].set(c).at[n_b].set(c_ctx)
    mod = _modulation(c_all, w_mod, b_mod)
    ii = lax.broadcasted_iota(jnp.int32, (TILE, TILE), 0)
    jj = lax.broadcasted_iota(jnp.int32, (TILE, TILE), 1)
    same = (ii // GLA_CHUNK) == (jj // GLA_CHUNK)
    tri_lo = (same & (jj <= ii)).astype(BF)
    tri_up = (same & (jj >= ii)).astype(BF)
    xall = jnp.concatenate([x, ctx], axis=1)
    for l in range(N_LAYERS):
        last = l == N_LAYERS - 1
        n_q = n_t - 1 if last else n_t
        lam_init = 0.8 - 0.6 * math.exp(-0.3 * l)
        p = _prep_layer(l, w_in, gla_w_a2, gla_b_a, mla_q_norm_g, mla_kv_norm_g, mla_w_uq, mla_w_ukv, w_branch,
                        w_out, ln1_g, ln1_b, ffn_w_in, ffn_w_out, ln2_g, ln2_b)
        mod_l = mod[l].reshape(16, 1, 6 * D_MODEL)
        gq, gk, gv, gr, dk, la, mk, cq, ckv = _proj_tok(xall, mod_l, ctok, stok, p, n_b, n_t, ltot)
        dqT, dvT, gatesT, mqT, mvT = _proj_feat(xall, mod_l, cosT, sinT, cq, ckv, p, n_b, n_t)
        ya = _gla(gq, gk, gv, la, gr, tri_lo, tri_up, gla_norm_g[l].reshape(1, GLA_DV), n_b, n_t)
        gcol = jnp.broadcast_to(diff_norm_g[l].reshape(128, 1), (128, TILE))
        ybT = _diff_attn(dqT, dk, dvT, diff_lam[l], gcol, n_b, n_t, lam_init)
        ycT = _mla_attn(mqT, mk, mvT, n_b, n_t)
        if not last:
            ybT = _diff_attn(dqT, dk, dvT, diff_lam[l], gcol, n_b, n_t, lam_init, ctx_into=ybT)
            ycT = _mla_attn(mqT, mk, mvT, n_b, n_t, ctx_into=ycT)
        x1 = _merge(xall, ya, ybT, ycT, gatesT, mod_l, p, n_b, n_t, n_q)
        xall = _ffn(x1, mod_l, p, n_b, n_t, n_q)
    return xall
```

```python
import functools
import math

import jax
import jax.numpy as jnp
from jax import lax
from jax.experimental import pallas as pl
from jax.experimental.pallas import tpu as pltpu

BF = jnp.bfloat16
F32 = jnp.float32

D_MODEL = 1024
N_LAYERS = 2
GRID_W = 64
TILE = 256
GLA_H, GLA_DK, GLA_DV, GLA_RANK, GLA_TAU = 4, 128, 256, 16, 16.0
GLA_CHUNK, GLA_SUB = 64, 16
DIFF_H, DIFF_DH = 8, 64
MLA_H, MLA_QR, MLA_KVR, MLA_NOPE, MLA_ROPE, MLA_DV = 8, 256, 128, 128, 64, 128
MLA_SCALE = (MLA_NOPE + MLA_ROPE) ** -0.5
ROPE_DIM, ROPE_BASE = 64, 10000.0
FFN_H = 2816
FFN_CHUNKS = ((0, 1536), (1536, 2816))
DN_ALPHA = (2 * N_LAYERS) ** 0.25
EPS = 1e-6
DIFF_QS, MLA_QS = 2, 4
KEY_TILES = 2
KEY_UNROLL = 4
LOG2E = math.log2(math.e)
ONES_ROWS = 16
VMEM_LIMIT = 56 * 1024 * 1024

_SIZES = (512, 512, 1024, 1024, 32, 1024, 1024, 1024, 256, 128, 64, 3072)
_OFF = [0]
for _s in _SIZES:
    _OFF.append(_OFF[-1] + _s)
(O_GQ, O_GK, O_GV, O_GR, O_GA, O_DQ, O_DK, O_DV, O_MQ, O_MKV, O_MKR, O_GATES, O_END) = _OFF

T_GQ, T_GK, T_GV, T_GR, T_DK, T_MQ, T_MKV, T_SMA, T_END = (0, 512, 1024, 2048, 3072, 4096, 4352, 4480, 4608)
F_DQ, F_DV, F_GATES, F_END = 0, 1024, 2048, 5120


def _cparams(sem):
    return pltpu.CompilerParams(dimension_semantics=sem, vmem_limit_bytes=VMEM_LIMIT)


def _resident(shape):
    nd = len(shape)
    return pl.BlockSpec(shape, lambda *_: (0,) * nd, pipeline_mode=pl.Buffered(1))


def _dot(a, b):
    return jnp.dot(a, b, preferred_element_type=F32)


def _dot_nt(a, b):
    return lax.dot_general(a, b, (((1,), (1,)), ((), ())), preferred_element_type=F32)


def _dot_tn(a, b):
    return lax.dot_general(a, b, (((0,), (0,)), ((), ())), preferred_element_type=F32)


def _ln_rows(x):
    mu = jnp.mean(x, axis=-1, keepdims=True)
    xc = x - mu
    var = jnp.mean(xc * xc, axis=-1, keepdims=True)
    return xc * lax.rsqrt(var + EPS)


def _sigmoid(x):
    return 1.0 / (1.0 + jnp.exp(-x))


def _silu(x):
    return x * _sigmoid(x)


def _mod_kernel(c_ref, w_ref, b_ref, o_ref):
    ca = _silu(c_ref[...]).astype(BF)
    o_ref[0] = _dot(ca, w_ref[0].astype(BF)) + b_ref[0]


def _modulation(c_all, w_mod, b_mod):
    nl, d, n6 = w_mod.shape
    r = c_all.shape[0]
    nblk = 1536
    return pl.pallas_call(
        _mod_kernel,
        grid=(nl, n6 // nblk),
        in_specs=[pl.BlockSpec((r, d), lambda l, j: (0, 0)),
                  pl.BlockSpec((1, d, nblk), lambda l, j: (l, 0, j)),
                  pl.BlockSpec((1, 1, nblk), lambda l, j: (l, 0, j))],
        out_specs=pl.BlockSpec((1, r, nblk), lambda l, j: (l, 0, j)),
        out_shape=jax.ShapeDtypeStruct((nl, r, n6), F32),
        compiler_params=_cparams(("arbitrary", "arbitrary")),
        name="modulation",
    )(c_all, w_mod, b_mod.reshape(nl, 1, n6))


def _x_specs(n_lat, ctx_blk):
    return [pl.BlockSpec((1, TILE, D_MODEL), lambda b, t: (b, jnp.minimum(t, n_lat - 1), 0)),
            pl.BlockSpec((1, TILE, D_MODEL), lambda b, t: (b, ctx_blk, 0))]


def _x_tile(x_ref, c_ref, n_lat):
    return jnp.where(pl.program_id(1) == n_lat, c_ref[0], x_ref[0])


def _proj_tok_kernel(x_ref, c_ref, sh_ref, sc_ref, ct_ref, st_ref, w_ref, wa2_ref, ba_ref, qg_ref, kvg_ref, wk_ref,
                     gq_ref, gk_ref, gv_ref, gr_ref, dk_ref, la_ref, mk_ref, cq_ref, ckv_ref, *, n_lat):
    h = (_ln_rows(_x_tile(x_ref, c_ref, n_lat)) * (1.0 + sc_ref[0]) + sh_ref[0]).astype(BF)

    def proj(a, b):
        return _dot(h, w_ref[:, a:b])

    gq_ref[0] = (proj(T_GQ, T_GK) * GLA_DK ** -0.5).astype(BF)
    gk_ref[0] = proj(T_GK, T_GV).astype(BF)
    gv_ref[0] = proj(T_GV, T_GR).astype(BF)
    gr_ref[0] = proj(T_GR, T_DK).astype(BF)
    ct = ct_ref[...]
    st = st_ref[...]
    first_half = lax.broadcasted_iota(jnp.int32, (TILE, 128), 1) % ROPE_DIM < ROPE_DIM // 2

    def rope(x):
        partner = jnp.where(first_half, pltpu.roll(x, 128 - ROPE_DIM // 2, 1), pltpu.roll(x, ROPE_DIM // 2, 1))
        return x * ct + partner * st

    dkx = proj(T_DK, T_MQ)
    for hh in range(DIFF_H):
        sl = slice(128 * hh, 128 * (hh + 1))
        dk_ref[0, :, sl] = rope(dkx[:, sl]).astype(BF)
    mq = proj(T_MQ, T_MKV)
    cq = mq * lax.rsqrt(jnp.mean(mq * mq, axis=-1, keepdims=True) + EPS) * qg_ref[...]
    cq_ref[0] = cq.astype(BF)
    mm = proj(T_MKV, T_END)
    mkv = mm[:, 0:128]
    sma = mm[:, 128:256]
    ckv = (mkv * lax.rsqrt(jnp.mean(mkv * mkv, axis=-1, keepdims=True) + EPS) * kvg_ref[...]).astype(BF)
    ckv_ref[0] = ckv
    kn = _dot(ckv, wk_ref[...]).astype(BF)
    kr = rope(sma)[:, 0:64].astype(BF)
    for hh in range(MLA_H):
        mk_ref[0, hh, :, 0:128] = kn[:, 128 * hh:128 * (hh + 1)]
        mk_ref[0, hh, :, 128:192] = kr
    z = _dot(sma.astype(BF), wa2_ref[...]) + ba_ref[...]
    la_ref[0] = (jnp.minimum(z, 0.0) - jnp.log(1.0 + jnp.exp(-jnp.abs(z)))) * (1.0 / GLA_TAU)


def _proj_tok(xs, mod_l, ctok, stok, p, n_b, n_t, ltot):
    x_lat, x_ctx, ctx_blk = xs
    tile_spec = lambda c: pl.BlockSpec((1, TILE, c), lambda b, t: (b, t, 0))
    mrow = lambda b, t: jnp.where(t == n_t - 1, n_b, b)
    bf_out = lambda c: jax.ShapeDtypeStruct((n_b, ltot, c), BF)
    return pl.pallas_call(
        functools.partial(_proj_tok_kernel, n_lat=n_t - 1),
        grid=(n_b, n_t),
        in_specs=_x_specs(n_t - 1, ctx_blk) + [
                  pl.BlockSpec((1, 1, D_MODEL), lambda b, t: (mrow(b, t), 0, 0)),
                  pl.BlockSpec((1, 1, D_MODEL), lambda b, t: (mrow(b, t), 0, 1)),
                  pl.BlockSpec((TILE, 128), lambda b, t: (t, 0)),
                  pl.BlockSpec((TILE, 128), lambda b, t: (t, 0)),
                  _resident((D_MODEL, T_END)), _resident((128, 1024)), _resident((1, 1024)),
                  _resident((1, MLA_QR)), _resident((1, MLA_KVR)), _resident((MLA_KVR, 1024))],
        out_specs=[tile_spec(512), tile_spec(512), tile_spec(1024), tile_spec(1024), tile_spec(1024),
                   tile_spec(1024),
                   pl.BlockSpec((1, MLA_H, TILE, 192), lambda b, t: (b, 0, t, 0)),
                   tile_spec(MLA_QR), tile_spec(MLA_KVR)],
        out_shape=[bf_out(512), bf_out(512), bf_out(1024), bf_out(1024), bf_out(1024),
                   jax.ShapeDtypeStruct((n_b, ltot, 1024), F32),
                   jax.ShapeDtypeStruct((n_b, MLA_H, ltot, 192), BF),
                   bf_out(MLA_QR), bf_out(MLA_KVR)],
        compiler_params=_cparams(("arbitrary", "arbitrary")),
        name="proj_tok",
    )(x_lat, x_ctx, mod_l, mod_l, ctok, stok, p["w_tok"], p["wa2"], p["ba"], p["qg"], p["kvg"], p["wukv_k"])


def _rope_rows(x1, x2, cos, sin):
    return x1 * cos - x2 * sin, x1 * sin + x2 * cos


def _proj_feat_kernel(x_ref, c_ref, sh_ref, sc_ref, cos_ref, sin_ref, cq_ref, ckv_ref, w_ref, wuq_ref, wv_ref,
                      dq_ref, dv_ref, g_ref, mq_ref, mv_ref, *, n_lat):
    h = (_ln_rows(_x_tile(x_ref, c_ref, n_lat)) * (1.0 + sc_ref[0]) + sh_ref[0]).astype(BF)
    cos = cos_ref[0]
    sin = sin_ref[0]
    dq = _dot_nt(w_ref[F_DQ:F_DV, :], h) * LOG2E
    for g in range(2 * DIFF_H):
        o1, o2 = _rope_rows(dq[64 * g:64 * g + 32], dq[64 * g + 32:64 * (g + 1)], cos, sin)
        dq_ref[0, 0, 64 * g:64 * g + 32, :] = o1.astype(BF)
        dq_ref[0, 0, 64 * g + 32:64 * (g + 1), :] = o2.astype(BF)
    dv_ref[0, 0] = _dot_nt(w_ref[F_DV:F_GATES, :], h).astype(BF)
    for j in range(3):
        a = F_GATES + 1024 * j
        g_ref[0, 0, 1024 * j:1024 * (j + 1), :] = _dot_nt(w_ref[a:a + 1024, :], h).astype(BF)
    cq = cq_ref[0]
    mq = _dot_nt(wuq_ref[...], cq) * (MLA_SCALE * LOG2E)
    for hh in range(MLA_H):
        a = 192 * hh
        o1, o2 = _rope_rows(mq[a + 128:a + 160], mq[a + 160:a + 192], cos, sin)
        mq_ref[0, 0, a:a + 128, :] = mq[a:a + 128].astype(BF)
        mq_ref[0, 0, a + 128:a + 160, :] = o1.astype(BF)
        mq_ref[0, 0, a + 160:a + 192, :] = o2.astype(BF)
    mv_ref[0, 0] = _dot_nt(wv_ref[...], ckv_ref[0]).astype(BF)


def _proj_feat(xs, mod_l, cosT, sinT, cq, ckv, p, n_b, n_t):
    x_lat, x_ctx, ctx_blk = xs
    tile_spec = lambda c: pl.BlockSpec((1, TILE, c), lambda b, t: (b, t, 0))
    mrow = lambda b, t: jnp.where(t == n_t - 1, n_b, b)
    fm_spec = lambda c: pl.BlockSpec((1, 1, c, TILE), lambda b, t: (b, t, 0, 0))
    fm_out = lambda c: jax.ShapeDtypeStruct((n_b, n_t, c, TILE), BF)
    return pl.pallas_call(
        functools.partial(_proj_feat_kernel, n_lat=n_t - 1),
        grid=(n_b, n_t),
        in_specs=_x_specs(n_t - 1, ctx_blk) + [
                  pl.BlockSpec((1, 1, D_MODEL), lambda b, t: (mrow(b, t), 0, 0)),
                  pl.BlockSpec((1, 1, D_MODEL), lambda b, t: (mrow(b, t), 0, 1)),
                  pl.BlockSpec((1, 32, TILE), lambda b, t: (t, 0, 0)),
                  pl.BlockSpec((1, 32, TILE), lambda b, t: (t, 0, 0)),
                  tile_spec(MLA_QR), tile_spec(MLA_KVR),
                  _resident((F_END, D_MODEL)), _resident((MLA_H * 192, MLA_QR)), _resident((1024, MLA_KVR))],
        out_specs=[fm_spec(1024), fm_spec(1024), fm_spec(3072), fm_spec(MLA_H * 192), fm_spec(1024)],
        out_shape=[fm_out(1024), fm_out(1024), fm_out(3072), fm_out(MLA_H * 192), fm_out(1024)],
        compiler_params=_cparams(("arbitrary", "arbitrary")),
        name="proj_feat",
    )(x_lat, x_ctx, mod_l, mod_l, cosT, sinT, cq, ckv, p["w_featT"], p["wuqT"], p["wukv_vT"])


def _flash_T(k_ref, v_ref, q2, m_ref, acc_ref, s_ref, n_lat, latent_keys):
    dv = v_ref.shape[2]
    m_ref[...] = jnp.full(m_ref.shape, -jnp.inf, F32)
    acc_ref[...] = jnp.zeros(acc_ref.shape, F32)
    rows = KEY_TILES * TILE

    def scores(slot, tile0):
        s_ref[slot] = _dot(k_ref[0, pl.ds(pl.multiple_of(tile0 * TILE, TILE), rows), :], q2)

    def consume(s, tile0, ntiles):
        m_old = m_ref[...]
        m_new = jnp.maximum(m_old, jnp.max(s, axis=0, keepdims=True))
        alpha = jnp.exp2(m_old - m_new)
        p = jnp.exp2(s - m_new).astype(BF)
        v = jnp.concatenate([v_ref[0, tile0 + i] for i in range(ntiles)], axis=1)
        vext = jnp.concatenate([v, jnp.ones((ONES_ROWS, ntiles * TILE), BF)], axis=0)
        acc_ref[...] = acc_ref[...] * alpha + _dot(vext, p)
        m_ref[...] = m_new

    if latent_keys:
        n_steps = n_lat // KEY_TILES
        scores(0, 0)

        def body(jj, carry):
            for u in range(KEY_UNROLL):
                j = KEY_UNROLL * jj + u
                scores((u + 1) % 2, (j + 1) * KEY_TILES)
                consume(s_ref[u % 2], j * KEY_TILES, KEY_TILES)
            return carry
        lax.fori_loop(0, n_steps // KEY_UNROLL - 1, body, 0)
        for u in range(KEY_UNROLL):
            j = n_steps - KEY_UNROLL + u
            if u < KEY_UNROLL - 1:
                scores((u + 1) % 2, (j + 1) * KEY_TILES)
            else:
                s_ctx = _dot(k_ref[0, pl.ds(n_lat * TILE, TILE), :], q2)
            consume(s_ref[u % 2], j * KEY_TILES, KEY_TILES)
        consume(s_ctx, n_lat, 1)
    else:
        consume(_dot(k_ref[0, pl.ds(0, TILE), :], q2), 0, 1)
    acc = acc_ref[...]
    return acc[0:dv] * (1.0 / acc[dv:dv + 1])


def _attn_call(kern, name, q_rows, k_spec_fn, n_b, n_heads, n_t, qs, scratch, qT, k, vT, extra, extra_specs,
               ctx_into=None):
    n_lat = n_t - 1
    if ctx_into is None:
        grid = (n_b, n_heads, n_lat // qs)
        q_spec = pl.BlockSpec((1, qs, q_rows, TILE), lambda b, h, q: (b, q, h, 0))
        k_spec = k_spec_fn(n_t * TILE, 0)
        v_spec = pl.BlockSpec((1, n_t, 128, TILE), lambda b, h, q: (b, 0, h, 0))
        o_spec = pl.BlockSpec((1, qs, 128, TILE), lambda b, h, q: (b, q, h, 0))
        alias_in, alias_specs, aliases = [], [], {}
    else:
        grid = (n_b, n_heads, 1)
        q_spec = pl.BlockSpec((1, 1, q_rows, TILE), lambda b, h, q: (b, n_lat, h, 0))
        k_spec = k_spec_fn(TILE, n_lat)
        v_spec = pl.BlockSpec((1, 1, 128, TILE), lambda b, h, q: (b, n_lat, h, 0))
        o_spec = pl.BlockSpec((1, 1, 128, TILE), lambda b, h, q: (b, n_lat, h, 0))
        alias_in, alias_specs = [ctx_into], [pl.BlockSpec(memory_space=pl.ANY)]
        aliases = {3 + len(extra): 0}
    return pl.pallas_call(
        kern,
        grid=grid,
        in_specs=[q_spec, k_spec, v_spec] + extra_specs + alias_specs,
        out_specs=o_spec,
        out_shape=jax.ShapeDtypeStruct((n_b, n_t, 1024, TILE), BF),
        scratch_shapes=scratch,
        input_output_aliases=aliases,
        compiler_params=_cparams(("arbitrary", "arbitrary", "arbitrary")),
        name=name,
    )(qT, k, vT, *extra, *alias_in)


def _diff_attn_kernel(q_ref, k_ref, v_ref, lam_ref, g_ref, *rest, n_lat, lam_init, qs, latent):
    o_ref, q2_ref, m_ref, acc_ref, s_ref = rest[-5:]
    nq = qs * TILE
    zeros = jnp.zeros((DIFF_DH, TILE), BF)
    for i in range(qs):
        q = q_ref[0, i]
        q2_ref[0:64, i * TILE:(i + 1) * TILE] = q[0:64]
        q2_ref[64:128, i * TILE:(i + 1) * TILE] = zeros
        q2_ref[0:64, nq + i * TILE:nq + (i + 1) * TILE] = zeros
        q2_ref[64:128, nq + i * TILE:nq + (i + 1) * TILE] = q[64:128]
    o = _flash_T(k_ref, v_ref, q2_ref[...], m_ref, acc_ref, s_ref, n_lat, latent)
    dl = lam_ref[...]
    lam = (jnp.exp(jnp.sum(dl[0:1] * dl[1:2], axis=1, keepdims=True))
           - jnp.exp(jnp.sum(dl[2:3] * dl[3:4], axis=1, keepdims=True)) + lam_init)
    od = o[:, 0:nq] - lam * o[:, nq:2 * nq]
    y = od * lax.rsqrt(jnp.mean(od * od, axis=0, keepdims=True) + EPS) * (1.0 - lam_init)
    for i in range(qs):
        o_ref[0, i] = (y[:, i * TILE:(i + 1) * TILE] * g_ref[...]).astype(BF)


def _diff_attn(dqT, dk, dvT, dlam, gcol, n_b, n_t, lam_init, ctx_into=None):
    qs = DIFF_QS if ctx_into is None else 1
    kern = functools.partial(_diff_attn_kernel, n_lat=n_t - 1, lam_init=lam_init, qs=qs, latent=ctx_into is None)
    k_spec_fn = lambda rows, blk: pl.BlockSpec((1, rows, 128), lambda b, h, q: (b, blk, h))
    scratch = [pltpu.VMEM((128, 2 * qs * TILE), BF), pltpu.VMEM((1, 2 * qs * TILE), F32),
               pltpu.VMEM((128 + ONES_ROWS, 2 * qs * TILE), F32),
               pltpu.VMEM((2, KEY_TILES * TILE, 2 * qs * TILE), F32)]
    extra_specs = [pl.BlockSpec((4, DIFF_DH), lambda b, h, q: (0, 0)),
                   pl.BlockSpec((128, TILE), lambda b, h, q: (0, 0))]
    return _attn_call(kern, "diff_attn" if ctx_into is None else "diff_attn_ctx", 128, k_spec_fn, n_b, DIFF_H, n_t,
                      qs, scratch, dqT, dk, dvT, [dlam, gcol], extra_specs, ctx_into)


def _mla_attn_kernel(q_ref, k_ref, v_ref, *rest, n_lat, qs, latent):
    o_ref, q2_ref, m_ref, acc_ref, s_ref = rest[-5:]
    for i in range(qs):
        q2_ref[:, i * TILE:(i + 1) * TILE] = q_ref[0, i]
    o = _flash_T(k_ref.at[0], v_ref, q2_ref[...], m_ref, acc_ref, s_ref, n_lat, latent)
    for i in range(qs):
        o_ref[0, i] = o[:, i * TILE:(i + 1) * TILE].astype(BF)


def _mla_attn(mqT, mk, mvT, n_b, n_t, ctx_into=None):
    qs = MLA_QS if ctx_into is None else 1
    kern = functools.partial(_mla_attn_kernel, n_lat=n_t - 1, qs=qs, latent=ctx_into is None)
    k_spec_fn = lambda rows, blk: pl.BlockSpec((1, 1, rows, 192), lambda b, h, q: (b, h, blk, 0))
    scratch = [pltpu.VMEM((192, qs * TILE), BF), pltpu.VMEM((1, qs * TILE), F32),
               pltpu.VMEM((128 + ONES_ROWS, qs * TILE), F32),
               pltpu.VMEM((2, KEY_TILES * TILE, qs * TILE), F32)]
    return _attn_call(kern, "mla_attn" if ctx_into is None else "mla_attn_ctx", 192, k_spec_fn, n_b, MLA_H, n_t,
                      qs, scratch, mqT, mk, mvT, [], [], ctx_into)


def _gla_tile(q_ref, k_ref, v_ref, la_ref, tri_ref, st_ref, reverse):
    g = la_ref[0]
    g1 = g.astype(BF)
    r1 = g - g1.astype(F32)
    g2 = r1.astype(BF)
    tri = tri_ref[...]
    bcum = _dot(tri, g1) + _dot(tri, g2)
    n_chunk = TILE // GLA_CHUNK
    n_sub = GLA_CHUNK // GLA_SUB
    rows64 = lax.broadcasted_iota(jnp.int32, (GLA_CHUNK, GLA_DK), 0)
    r16 = lax.broadcasted_iota(jnp.int32, (GLA_SUB, GLA_CHUNK), 0)
    c16 = lax.broadcasted_iota(jnp.int32, (GLA_SUB, GLA_CHUNK), 1)
    outs = []
    for hh in range(GLA_H):
        kc0, kc1 = GLA_DK * hh, GLA_DK * (hh + 1)
        o_chunks = [None] * n_chunk
        st = st_ref[hh]
        for c in (range(n_chunk - 1, -1, -1) if reverse else range(n_chunk)):
            r0 = GLA_CHUNK * c
            bc = bcum[r0:r0 + GLA_CHUNK, kc0:kc1]
            qc = q_ref[0, r0:r0 + GLA_CHUNK, kc0:kc1].astype(F32)
            kc = k_ref[0, r0:r0 + GLA_CHUNK, kc0:kc1].astype(F32)
            vc = v_ref[0, r0:r0 + GLA_CHUNK, GLA_DV * hh:GLA_DV * (hh + 1)]
            b_tot = bc[0:1] if reverse else bc[GLA_CHUNK - 1:GLA_CHUNK]
            a_rows = []
            for i in range(n_sub):
                s0 = GLA_SUB * i
                ref_row = bc[s0 + GLA_SUB - 1:s0 + GLA_SUB] if reverse else bc[s0:s0 + 1]
                qs = (qc[s0:s0 + GLA_SUB] * jnp.exp(bc[s0:s0 + GLA_SUB] - ref_row)).astype(BF)
                valid = (rows64 >= s0) if reverse else (rows64 < s0 + GLA_SUB)
                ks = (kc * jnp.exp(jnp.where(valid, ref_row - bc, -jnp.inf))).astype(BF)
                a = _dot_nt(qs, ks)
                keep = (c16 > r16 + s0) if reverse else (c16 <= r16 + s0)
                a_rows.append(jnp.where(keep, a, 0.0))
            a_mat = jnp.concatenate(a_rows, axis=0).astype(BF)
            q_in = (qc * jnp.exp(bc)).astype(BF)
            o_chunks[c] = _dot(a_mat, vc) + _dot_nt(q_in, st.astype(BF))
            k_d = (kc * jnp.exp(b_tot - bc)).astype(BF)
            st = st * jnp.exp(b_tot) + _dot_tn(vc, k_d)
        st_ref[hh] = st
        outs.append(jnp.concatenate(o_chunks, axis=0))
    return jnp.concatenate(outs, axis=1)


def _gla_fwd_kernel(q_ref, k_ref, v_ref, la_ref, tri_ref, o_ref, st_ref):
    @pl.when(pl.program_id(1) == 0)
    def _():
        st_ref[...] = jnp.zeros(st_ref.shape, F32)
    o_ref[0] = _gla_tile(q_ref, k_ref, v_ref, la_ref, tri_ref, st_ref, False)


def _gla_bwd_kernel(q_ref, k_ref, v_ref, la_ref, tri_ref, of_ref, r_ref, g_ref, y_ref, st_ref):
    @pl.when(pl.program_id(1) == 0)
    def _():
        st_ref[...] = jnp.zeros(st_ref.shape, F32)
    o = of_ref[0] + _gla_tile(q_ref, k_ref, v_ref, la_ref, tri_ref, st_ref, True)
    gn = g_ref[...]
    for hh in range(GLA_H):
        sl = slice(GLA_DV * hh, GLA_DV * (hh + 1))
        oh = o[:, sl]
        yh = oh * lax.rsqrt(jnp.mean(oh * oh, axis=-1, keepdims=True) + EPS) * gn
        y_ref[0, :, sl] = (yh * _silu(r_ref[0, :, sl].astype(F32))).astype(BF)


def _gla(gq, gk, gv, la, gr, tri_lo, tri_up, gnorm, n_b, n_t):
    ltot = n_t * TILE
    t_fwd = lambda s: (s + n_t - 1) % n_t
    t_bwd = lambda s: jnp.where(s == 0, n_t - 1, n_t - 1 - s)

    def specs(tmap, dirn):
        ts = lambda c, cb=0: pl.BlockSpec((1, TILE, c), lambda b, s: (b, tmap(s), cb))
        return [ts(512), ts(512), ts(1024), ts(512, dirn), pl.BlockSpec((TILE, TILE), lambda b, s: (0, 0))]

    o_f = pl.pallas_call(
        _gla_fwd_kernel,
        grid=(n_b, n_t),
        in_specs=specs(t_fwd, 0),
        out_specs=pl.BlockSpec((1, TILE, 1024), lambda b, s: (b, t_fwd(s), 0)),
        out_shape=jax.ShapeDtypeStruct((n_b, ltot, 1024), F32),
        scratch_shapes=[pltpu.VMEM((GLA_H, GLA_DV, GLA_DK), F32)],
        compiler_params=_cparams(("arbitrary", "arbitrary")),
        name="gla_fwd",
    )(gq, gk, gv, la, tri_lo)
    tsb = lambda c: pl.BlockSpec((1, TILE, c), lambda b, s: (b, t_bwd(s), 0))
    return pl.pallas_call(
        _gla_bwd_kernel,
        grid=(n_b, n_t),
        in_specs=specs(t_bwd, 1) + [tsb(1024), tsb(1024), pl.BlockSpec((1, GLA_DV), lambda b, s: (0, 0))],
        out_specs=tsb(1024),
        out_shape=jax.ShapeDtypeStruct((n_b, ltot, 1024), BF),
        scratch_shapes=[pltpu.VMEM((GLA_H, GLA_DV, GLA_DK), F32)],
        compiler_params=_cparams(("arbitrary", "arbitrary")),
        name="gla_bwd",
    )(gq, gk, gv, la, tri_up, o_f, gr, gnorm)


def _merge_kernel(x_ref, c_ref, ya_ref, yb_ref, yc_ref, g_ref, g1_ref, wb_ref, wo_ref, lg_ref, lb_ref, o_ref, *,
                  n_lat):
    zt = (_sigmoid(g_ref[0, 0, 0:1024, :].astype(F32)) * _dot_nt(wb_ref[0], ya_ref[0])
          + _sigmoid(g_ref[0, 0, 1024:2048, :].astype(F32)) * _dot(wb_ref[1], yb_ref[0, 0])
          + _sigmoid(g_ref[0, 0, 2048:3072, :].astype(F32)) * _dot(wb_ref[2], yc_ref[0, 0]))
    u = _dot(wo_ref[...], zt.astype(BF)).T
    y = _ln_rows(DN_ALPHA * _x_tile(x_ref, c_ref, n_lat) + g1_ref[0] * u)
    o_ref[0] = y * lg_ref[...] + lb_ref[...]


def _merge(xs, ya, ybT, ycT, gatesT, mod_l, p, n_b, n_t, n_q):
    x_lat, x_ctx, ctx_blk = xs
    tile_spec = lambda c: pl.BlockSpec((1, TILE, c), lambda b, t: (b, t, 0))
    fm_spec = lambda c: pl.BlockSpec((1, 1, c, TILE), lambda b, t: (b, t, 0, 0))
    mrow = lambda b, t: jnp.where(t == n_t - 1, n_b, b)
    return pl.pallas_call(
        functools.partial(_merge_kernel, n_lat=n_t - 1),
        grid=(n_b, n_q),
        in_specs=_x_specs(n_t - 1, ctx_blk) + [
                  tile_spec(1024), fm_spec(1024), fm_spec(1024), fm_spec(3072),
                  pl.BlockSpec((1, 1, D_MODEL), lambda b, t: (mrow(b, t), 0, 2)),
                  _resident((3, D_MODEL, D_MODEL)), _resident((D_MODEL, D_MODEL)),
                  _resident((1, D_MODEL)), _resident((1, D_MODEL))],
        out_specs=tile_spec(D_MODEL),
        out_shape=jax.ShapeDtypeStruct((n_b, n_q * TILE, D_MODEL), F32),
        compiler_params=_cparams(("arbitrary", "arbitrary")),
        name="merge",
    )(x_lat, x_ctx, ya, ybT, ycT, gatesT, mod_l, p["wbT"], p["woT"], p["ln1_g"], p["ln1_b"])


def _ffn_kernel(x_ref, sh_ref, sc_ref, g2_ref, wi_ref, wo_ref, lg_ref, lb_ref, o_ref):
    x = x_ref[0]
    h = (_ln_rows(x) * (1.0 + sc_ref[0]) + sh_ref[0]).astype(BF)
    acc = jnp.zeros((TILE, D_MODEL), F32)
    for a, b in FFN_CHUNKS:
        gate = _dot(h, wi_ref[:, a:b])
        up = _dot(h, wi_ref[:, FFN_H + a:FFN_H + b])
        acc = acc + _dot((_silu(gate) * up).astype(BF), wo_ref[a:b, :])
    y = _ln_rows(DN_ALPHA * x + g2_ref[0] * acc)
    o_ref[0] = y * lg_ref[...] + lb_ref[...]


def _ffn(x1, mod_l, p, n_b, n_t, n_q):
    tile_spec = lambda c: pl.BlockSpec((1, TILE, c), lambda b, t: (b, t, 0))
    mrow = lambda b, t: jnp.where(t == n_t - 1, n_b, b)
    mspec = lambda j: pl.BlockSpec((1, 1, D_MODEL), lambda b, t: (mrow(b, t), 0, j))
    return pl.pallas_call(
        _ffn_kernel,
        grid=(n_b, n_q),
        in_specs=[tile_spec(D_MODEL), mspec(3), mspec(4), mspec(5),
                  _resident((D_MODEL, 2 * FFN_H)), _resident((FFN_H, D_MODEL)),
                  _resident((1, D_MODEL)), _resident((1, D_MODEL))],
        out_specs=tile_spec(D_MODEL),
        out_shape=jax.ShapeDtypeStruct((n_b, n_q * TILE, D_MODEL), F32),
        compiler_params=_cparams(("arbitrary", "arbitrary")),
        name="ffn",
    )(x1, mod_l, mod_l, mod_l, p["ffn_wi"], p["ffn_wo"], p["ln2_g"], p["ln2_b"])


def _prep_layer(l, w_in, gla_w_a2, gla_b_a, mla_q_norm_g, mla_kv_norm_g, mla_w_uq, mla_w_ukv, w_branch, w_out,
                ln1_g, ln1_b, ffn_w_in, ffn_w_out, ln2_g, ln2_b):
    w = w_in[l]
    seg = lambda a, b: w[:, a:b]
    w_tok = jnp.concatenate([
        seg(O_GQ, O_GK), seg(O_GK, O_GV), seg(O_GV, O_GR), seg(O_GR, O_GA), seg(O_DK, O_DV),
        seg(O_MQ, O_MKV), seg(O_MKV, O_MKR),
        seg(O_MKR, O_GATES), seg(O_GA, O_DQ), jnp.zeros((D_MODEL, 32), F32)], axis=1).astype(BF)
    w_featT = jnp.concatenate([seg(O_DQ, O_DK) * DIFF_DH ** -0.5, seg(O_DV, O_MQ), seg(O_GATES, O_END)],
                              axis=1).T.astype(BF)
    wa2 = jnp.zeros((128, 1024), F32)
    wa2 = wa2.at[64:80, 0:512].set(gla_w_a2[l, 0]).at[80:96, 512:1024].set(gla_w_a2[l, 1]).astype(BF)
    ukv = mla_w_ukv[l].reshape(MLA_KVR, MLA_H, MLA_NOPE + MLA_DV)
    return dict(
        w_tok=w_tok, w_featT=w_featT, wa2=wa2, ba=gla_b_a[l].reshape(1, 1024),
        qg=mla_q_norm_g[l].reshape(1, MLA_QR), kvg=mla_kv_norm_g[l].reshape(1, MLA_KVR),
        wukv_k=ukv[:, :, :MLA_NOPE].reshape(MLA_KVR, 1024).astype(BF),
        wukv_vT=ukv[:, :, MLA_NOPE:].reshape(MLA_KVR, 1024).T.astype(BF),
        wuqT=mla_w_uq[l].T.astype(BF),
        wbT=jnp.swapaxes(w_branch[l], 1, 2).astype(BF), woT=w_out[l].T.astype(BF),
        ln1_g=ln1_g[l].reshape(1, D_MODEL), ln1_b=ln1_b[l].reshape(1, D_MODEL),
        ffn_wi=ffn_w_in[l].astype(BF), ffn_wo=ffn_w_out[l].astype(BF),
        ln2_g=ln2_g[l].reshape(1, D_MODEL), ln2_b=ln2_b[l].reshape(1, D_MODEL))


def _rope_tables(l_lat, l_ctx):
    rows = l_lat // GRID_W
    pos_row = jnp.broadcast_to(jnp.arange(rows, dtype=F32)[:, None], (rows, GRID_W)).reshape(l_lat)
    pos_col = jnp.broadcast_to(jnp.arange(GRID_W, dtype=F32)[None, :], (rows, GRID_W)).reshape(l_lat)
    d_axis = ROPE_DIM // 2
    inv = ROPE_BASE ** (-jnp.arange(0, d_axis, 2, dtype=F32) / d_axis)
    ang = jnp.concatenate([pos_row[:, None] * inv, pos_col[:, None] * inv], axis=-1)
    cos = jnp.concatenate([jnp.cos(ang), jnp.ones((l_ctx, 32), F32)], axis=0)
    sin = jnp.concatenate([jnp.sin(ang), jnp.zeros((l_ctx, 32), F32)], axis=0)
    ctok = jnp.tile(cos, (1, 4))
    stok = jnp.tile(jnp.concatenate([-sin, sin], axis=1), (1, 2))
    n_t = (l_lat + l_ctx) // TILE
    to_fm = lambda a: a.T.reshape(32, n_t, TILE).transpose(1, 0, 2)
    return ctok, stok, to_fm(cos), to_fm(sin)


def kernel(x, c, ctx, c_ctx, w_mod, b_mod, w_in, gla_w_a2, gla_b_a, gla_norm_g, diff_lam, diff_norm_g,
           mla_q_norm_g, mla_kv_norm_g, mla_w_uq, mla_w_ukv, w_branch, w_out, ln1_g, ln1_b, ffn_w_in, ffn_w_out,
           ln2_g, ln2_b):
    n_b, l_lat, _ = x.shape
    l_ctx = ctx.shape[1]
    assert l_ctx == TILE and n_b + 1 <= 16
    assert l_lat % (TILE * max(DIFF_QS, MLA_QS)) == 0 and l_lat % (TILE * KEY_TILES * KEY_UNROLL) == 0
    ltot = l_lat + l_ctx
    n_t = ltot // TILE
    ctok, stok, cosT, sinT = _rope_tables(l_lat, l_ctx)
    c_all = jnp.zeros((16, D_MODEL), F32).at[:n_b].set(c).at[n_b].set(c_ctx)
    mod = _modulation(c_all, w_mod, b_mod)
    ii = lax.broadcasted_iota(jnp.int32, (TILE, TILE), 0)
    jj = lax.broadcasted_iota(jnp.int32, (TILE, TILE), 1)
    same = (ii // GLA_CHUNK) == (jj // GLA_CHUNK)
    tri_lo = (same & (jj <= ii)).astype(BF)
    tri_up = (same & (jj >= ii)).astype(BF)
    xs = (x, ctx, 0)
    for l in range(N_LAYERS):
        last = l == N_LAYERS - 1
        n_q = n_t - 1 if last else n_t
        lam_init = 0.8 - 0.6 * math.exp(-0.3 * l)
        p = _prep_layer(l, w_in, gla_w_a2, gla_b_a, mla_q_norm_g, mla_kv_norm_g, mla_w_uq, mla_w_ukv, w_branch,
                        w_out, ln1_g, ln1_b, ffn_w_in, ffn_w_out, ln2_g, ln2_b)
        mod_l = mod[l].reshape(16, 1, 6 * D_MODEL)
        gq, gk, gv, gr, dk, la, mk, cq, ckv = _proj_tok(xs, mod_l, ctok, stok, p, n_b, n_t, ltot)
        dqT, dvT, gatesT, mqT, mvT = _proj_feat(xs, mod_l, cosT, sinT, cq, ckv, p, n_b, n_t)
        ya = _gla(gq, gk, gv, la, gr, tri_lo, tri_up, gla_norm_g[l].reshape(1, GLA_DV), n_b, n_t)
        gcol = jnp.broadcast_to(diff_norm_g[l].reshape(128, 1), (128, TILE))
        ybT = _diff_attn(dqT, dk, dvT, diff_lam[l], gcol, n_b, n_t, lam_init)
        ycT = _mla_attn(mqT, mk, mvT, n_b, n_t)
        if not last:
            ybT = _diff_attn(dqT, dk, dvT, diff_lam[l], gcol, n_b, n_t, lam_init, ctx_into=ybT)
            ycT = _mla_attn(mqT, mk, mvT, n_b, n_t, ctx_into=ycT)
        x1 = _merge(xs, ya, ybT, ycT, gatesT, mod_l, p, n_b, n_t, n_q)
        x2 = _ffn(x1, mod_l, p, n_b, n_t, n_q)
        xs = (x2, x2, n_t - 1)
    return x2
```

```python
import functools
import math

import jax
import jax.numpy as jnp
from jax import lax
from jax.experimental import pallas as pl
from jax.experimental.pallas import tpu as pltpu

BF = jnp.bfloat16
F32 = jnp.float32

D_MODEL = 1024
N_LAYERS = 2
GRID_W = 64
TILE = 256
GLA_H, GLA_DK, GLA_DV, GLA_RANK, GLA_TAU = 4, 128, 256, 16, 16.0
GLA_CHUNK, GLA_SUB = 64, 16
DIFF_H, DIFF_DH = 8, 64
MLA_H, MLA_QR, MLA_KVR, MLA_NOPE, MLA_ROPE, MLA_DV = 8, 256, 128, 128, 64, 128
MLA_SCALE = (MLA_NOPE + MLA_ROPE) ** -0.5
ROPE_DIM, ROPE_BASE = 64, 10000.0
FFN_H = 2816
FFN_CHUNKS = ((0, 1536), (1536, 2816))
DN_ALPHA = (2 * N_LAYERS) ** 0.25
EPS = 1e-6
DIFF_QS, MLA_QS = 2, 4
KEY_TILES = 2
KEY_UNROLL = 4
LOG2E = math.log2(math.e)
ONES_ROWS = 16
VMEM_LIMIT = 56 * 1024 * 1024

_SIZES = (512, 512, 1024, 1024, 32, 1024, 1024, 1024, 256, 128, 64, 3072)
_OFF = [0]
for _s in _SIZES:
    _OFF.append(_OFF[-1] + _s)
(O_GQ, O_GK, O_GV, O_GR, O_GA, O_DQ, O_DK, O_DV, O_MQ, O_MKV, O_MKR, O_GATES, O_END) = _OFF

T_GQ, T_GK, T_GV, T_GR, T_DK, T_MQ, T_MKV, T_SMA, T_END = (0, 512, 1024, 2048, 3072, 4096, 4352, 4480, 4608)
F_DQ, F_DV, F_GATES, F_END = 0, 1024, 2048, 5120


def _cparams(sem):
    return pltpu.CompilerParams(dimension_semantics=sem, vmem_limit_bytes=VMEM_LIMIT)


def _resident(shape):
    nd = len(shape)
    return pl.BlockSpec(shape, lambda *_: (0,) * nd, pipeline_mode=pl.Buffered(1))


def _dot(a, b):
    return jnp.dot(a, b, preferred_element_type=F32)


def _dot_nt(a, b):
    return lax.dot_general(a, b, (((1,), (1,)), ((), ())), preferred_element_type=F32)


def _dot_tn(a, b):
    return lax.dot_general(a, b, (((0,), (0,)), ((), ())), preferred_element_type=F32)


def _ln_rows(x):
    mu = jnp.mean(x, axis=-1, keepdims=True)
    xc = x - mu
    var = jnp.mean(xc * xc, axis=-1, keepdims=True)
    return xc * lax.rsqrt(var + EPS)


def _sigmoid(x):
    return 1.0 / (1.0 + jnp.exp(-x))


def _silu(x):
    return x * _sigmoid(x)


def _mod_kernel(c_ref, w_ref, b_ref, o_ref):
    ca = _silu(c_ref[...]).astype(BF)
    o_ref[0] = _dot(ca, w_ref[0].astype(BF)) + b_ref[0]


def _modulation(c_all, w_mod, b_mod):
    nl, d, n6 = w_mod.shape
    r = c_all.shape[0]
    nblk = 1536
    return pl.pallas_call(
        _mod_kernel,
        grid=(nl, n6 // nblk),
        in_specs=[pl.BlockSpec((r, d), lambda l, j: (0, 0)),
                  pl.BlockSpec((1, d, nblk), lambda l, j: (l, 0, j)),
                  pl.BlockSpec((1, 1, nblk), lambda l, j: (l, 0, j))],
        out_specs=pl.BlockSpec((1, r, nblk), lambda l, j: (l, 0, j)),
        out_shape=jax.ShapeDtypeStruct((nl, r, n6), F32),
        compiler_params=_cparams(("arbitrary", "arbitrary")),
        name="modulation",
    )(c_all, w_mod, b_mod.reshape(nl, 1, n6))


def _x_specs(n_lat, ctx_blk):
    return [pl.BlockSpec((1, TILE, D_MODEL), lambda b, t: (b, jnp.minimum(t, n_lat - 1), 0)),
            pl.BlockSpec((1, TILE, D_MODEL), lambda b, t: (b, ctx_blk, 0))]


def _x_tile(x_ref, c_ref, n_lat):
    return jnp.where(pl.program_id(1) == n_lat, c_ref[0], x_ref[0])


def _proj_tok_kernel(x_ref, c_ref, sh_ref, sc_ref, ct_ref, st_ref, w_ref, wa2_ref, ba_ref, qg_ref, kvg_ref, wk_ref,
                     gq_ref, gk_ref, gv_ref, gr_ref, dk_ref, la_ref, mk_ref, cq_ref, ckv_ref, *, n_lat):
    h = (_ln_rows(_x_tile(x_ref, c_ref, n_lat)) * (1.0 + sc_ref[0]) + sh_ref[0]).astype(BF)

    def proj(a, b):
        return _dot(h, w_ref[:, a:b])

    gq_ref[0] = (proj(T_GQ, T_GK) * GLA_DK ** -0.5).astype(BF)
    gk_ref[0] = proj(T_GK, T_GV).astype(BF)
    gv_ref[0] = proj(T_GV, T_GR).astype(BF)
    gr_ref[0] = proj(T_GR, T_DK).astype(BF)
    ct = ct_ref[...]
    st = st_ref[...]
    first_half = lax.broadcasted_iota(jnp.int32, (TILE, 128), 1) % ROPE_DIM < ROPE_DIM // 2

    def rope(x):
        partner = jnp.where(first_half, pltpu.roll(x, 128 - ROPE_DIM // 2, 1), pltpu.roll(x, ROPE_DIM // 2, 1))
        return x * ct + partner * st

    dkx = proj(T_DK, T_MQ)
    for hh in range(DIFF_H):
        sl = slice(128 * hh, 128 * (hh + 1))
        dk_ref[0, :, sl] = rope(dkx[:, sl]).astype(BF)
    mq = proj(T_MQ, T_MKV)
    cq = mq * lax.rsqrt(jnp.mean(mq * mq, axis=-1, keepdims=True) + EPS) * qg_ref[...]
    cq_ref[0] = cq.astype(BF)
    mm = proj(T_MKV, T_END)
    mkv = mm[:, 0:128]
    sma = mm[:, 128:256]
    ckv = (mkv * lax.rsqrt(jnp.mean(mkv * mkv, axis=-1, keepdims=True) + EPS) * kvg_ref[...]).astype(BF)
    ckv_ref[0] = ckv
    kn = _dot(ckv, wk_ref[...]).astype(BF)
    kr = rope(sma)[:, 0:64].astype(BF)
    for hh in range(MLA_H):
        mk_ref[0, hh, :, 0:128] = kn[:, 128 * hh:128 * (hh + 1)]
        mk_ref[0, hh, :, 128:192] = kr
    z = _dot(sma.astype(BF), wa2_ref[...]) + ba_ref[...]
    la_ref[0] = (jnp.minimum(z, 0.0) - jnp.log(1.0 + jnp.exp(-jnp.abs(z)))) * (1.0 / GLA_TAU)


def _proj_tok(xs, mod_l, ctok, stok, p, n_b, n_t, ltot):
    x_lat, x_ctx, ctx_blk = xs
    tile_spec = lambda c: pl.BlockSpec((1, TILE, c), lambda b, t: (b, t, 0))
    mrow = lambda b, t: jnp.where(t == n_t - 1, n_b, b)
    bf_out = lambda c: jax.ShapeDtypeStruct((n_b, ltot, c), BF)
    return pl.pallas_call(
        functools.partial(_proj_tok_kernel, n_lat=n_t - 1),
        grid=(n_b, n_t),
        in_specs=_x_specs(n_t - 1, ctx_blk) + [
                  pl.BlockSpec((1, 1, D_MODEL), lambda b, t: (mrow(b, t), 0, 0)),
                  pl.BlockSpec((1, 1, D_MODEL), lambda b, t: (mrow(b, t), 0, 1)),
                  pl.BlockSpec((TILE, 128), lambda b, t: (t, 0)),
                  pl.BlockSpec((TILE, 128), lambda b, t: (t, 0)),
                  _resident((D_MODEL, T_END)), _resident((128, 1024)), _resident((1, 1024)),
                  _resident((1, MLA_QR)), _resident((1, MLA_KVR)), _resident((MLA_KVR, 1024))],
        out_specs=[tile_spec(512), tile_spec(512), tile_spec(1024), tile_spec(1024), tile_spec(1024),
                   tile_spec(1024),
                   pl.BlockSpec((1, MLA_H, TILE, 192), lambda b, t: (b, 0, t, 0)),
                   tile_spec(MLA_QR), tile_spec(MLA_KVR)],
        out_shape=[bf_out(512), bf_out(512), bf_out(1024), bf_out(1024), bf_out(1024),
                   jax.ShapeDtypeStruct((n_b, ltot, 1024), F32),
                   jax.ShapeDtypeStruct((n_b, MLA_H, ltot, 192), BF),
                   bf_out(MLA_QR), bf_out(MLA_KVR)],
        compiler_params=_cparams(("arbitrary", "arbitrary")),
        name="proj_tok",
    )(x_lat, x_ctx, mod_l, mod_l, ctok, stok, p["w_tok"], p["wa2"], p["ba"], p["qg"], p["kvg"], p["wukv_k"])


def _rope_rows(x1, x2, cos, sin):
    return x1 * cos - x2 * sin, x1 * sin + x2 * cos


def _proj_feat_kernel(x_ref, c_ref, sh_ref, sc_ref, cos_ref, sin_ref, cq_ref, ckv_ref, w_ref, wuq_ref, wv_ref,
                      dq_ref, dv_ref, g_ref, mq_ref, mv_ref, *, n_lat):
    h = (_ln_rows(_x_tile(x_ref, c_ref, n_lat)) * (1.0 + sc_ref[0]) + sh_ref[0]).astype(BF)
    cos = cos_ref[0]
    sin = sin_ref[0]
    dq = _dot_nt(w_ref[F_DQ:F_DV, :], h) * LOG2E
    for g in range(2 * DIFF_H):
        o1, o2 = _rope_rows(dq[64 * g:64 * g + 32], dq[64 * g + 32:64 * (g + 1)], cos, sin)
        dq_ref[0, 0, 64 * g:64 * g + 32, :] = o1.astype(BF)
        dq_ref[0, 0, 64 * g + 32:64 * (g + 1), :] = o2.astype(BF)
    dv_ref[0, 0] = _dot_nt(w_ref[F_DV:F_GATES, :], h).astype(BF)
    for j in range(3):
        a = F_GATES + 1024 * j
        g_ref[0, 0, 1024 * j:1024 * (j + 1), :] = _dot_nt(w_ref[a:a + 1024, :], h).astype(BF)
    cq = cq_ref[0]
    mq = _dot_nt(wuq_ref[...], cq) * (MLA_SCALE * LOG2E)
    for hh in range(MLA_H):
        a = 192 * hh
        o1, o2 = _rope_rows(mq[a + 128:a + 160], mq[a + 160:a + 192], cos, sin)
        mq_ref[0, 0, a:a + 128, :] = mq[a:a + 128].astype(BF)
        mq_ref[0, 0, a + 128:a + 160, :] = o1.astype(BF)
        mq_ref[0, 0, a + 160:a + 192, :] = o2.astype(BF)
    mv_ref[0, 0] = _dot_nt(wv_ref[...], ckv_ref[0]).astype(BF)


def _proj_feat(xs, mod_l, cosT, sinT, cq, ckv, p, n_b, n_t):
    x_lat, x_ctx, ctx_blk = xs
    tile_spec = lambda c: pl.BlockSpec((1, TILE, c), lambda b, t: (b, t, 0))
    mrow = lambda b, t: jnp.where(t == n_t - 1, n_b, b)
    fm_spec = lambda c: pl.BlockSpec((1, 1, c, TILE), lambda b, t: (b, t, 0, 0))
    fm_out = lambda c: jax.ShapeDtypeStruct((n_b, n_t, c, TILE), BF)
    return pl.pallas_call(
        functools.partial(_proj_feat_kernel, n_lat=n_t - 1),
        grid=(n_b, n_t),
        in_specs=_x_specs(n_t - 1, ctx_blk) + [
                  pl.BlockSpec((1, 1, D_MODEL), lambda b, t: (mrow(b, t), 0, 0)),
                  pl.BlockSpec((1, 1, D_MODEL), lambda b, t: (mrow(b, t), 0, 1)),
                  pl.BlockSpec((1, 32, TILE), lambda b, t: (t, 0, 0)),
                  pl.BlockSpec((1, 32, TILE), lambda b, t: (t, 0, 0)),
                  tile_spec(MLA_QR), tile_spec(MLA_KVR),
                  _resident((F_END, D_MODEL)), _resident((MLA_H * 192, MLA_QR)), _resident((1024, MLA_KVR))],
        out_specs=[fm_spec(1024), fm_spec(1024), fm_spec(3072), fm_spec(MLA_H * 192), fm_spec(1024)],
        out_shape=[fm_out(1024), fm_out(1024), fm_out(3072), fm_out(MLA_H * 192), fm_out(1024)],
        compiler_params=_cparams(("arbitrary", "arbitrary")),
        name="proj_feat",
    )(x_lat, x_ctx, mod_l, mod_l, cosT, sinT, cq, ckv, p["w_featT"], p["wuqT"], p["wukv_vT"])


def _flash_T(k_ref, v_ref, q2, m_ref, acc_ref, s_ref, n_lat, latent_keys):
    dv = v_ref.shape[2]
    m_ref[2] = jnp.full(m_ref.shape[1:], -jnp.inf, F32)
    acc_ref[...] = jnp.zeros(acc_ref.shape, F32)
    rows = KEY_TILES * TILE

    def scores(slot, tile0):
        s = _dot(k_ref[0, pl.ds(pl.multiple_of(tile0 * TILE, TILE), rows), :], q2)
        s_ref[slot] = s
        m_ref[slot] = jnp.max(s, axis=0, keepdims=True)

    def consume(s, s_max, tile0, ntiles):
        m_old = m_ref[2]
        m_new = jnp.maximum(m_old, s_max)
        alpha = jnp.exp2(m_old - m_new)
        p = jnp.exp2(s - m_new).astype(BF)
        v = jnp.concatenate([v_ref[0, tile0 + i] for i in range(ntiles)], axis=1)
        vext = jnp.concatenate([v, jnp.ones((ONES_ROWS, ntiles * TILE), BF)], axis=0)
        acc_ref[...] = acc_ref[...] * alpha + _dot(vext, p)
        m_ref[2] = m_new

    def ctx_step(tile):
        s = _dot(k_ref[0, pl.ds(tile * TILE, TILE), :], q2)
        return s, jnp.max(s, axis=0, keepdims=True)

    if latent_keys:
        n_steps = n_lat // KEY_TILES
        scores(0, 0)

        def body(jj, carry):
            for u in range(KEY_UNROLL):
                j = KEY_UNROLL * jj + u
                scores((u + 1) % 2, (j + 1) * KEY_TILES)
                consume(s_ref[u % 2], m_ref[u % 2], j * KEY_TILES, KEY_TILES)
            return carry
        lax.fori_loop(0, n_steps // KEY_UNROLL - 1, body, 0)
        for u in range(KEY_UNROLL):
            j = n_steps - KEY_UNROLL + u
            if u < KEY_UNROLL - 1:
                scores((u + 1) % 2, (j + 1) * KEY_TILES)
            else:
                s_ctx, mx_ctx = ctx_step(n_lat)
            consume(s_ref[u % 2], m_ref[u % 2], j * KEY_TILES, KEY_TILES)
        consume(s_ctx, mx_ctx, n_lat, 1)
    else:
        consume(*ctx_step(0), 0, 1)
    acc = acc_ref[...]
    return acc[0:dv] * (1.0 / acc[dv:dv + 1])


def _attn_call(kern, name, q_rows, k_spec_fn, n_b, n_heads, n_t, qs, scratch, qT, k, vT, extra, extra_specs,
               ctx_into=None):
    n_lat = n_t - 1
    if ctx_into is None:
        grid = (n_b, n_heads, n_lat // qs)
        q_spec = pl.BlockSpec((1, qs, q_rows, TILE), lambda b, h, q: (b, q, h, 0))
        k_spec = k_spec_fn(n_t * TILE, 0)
        v_spec = pl.BlockSpec((1, n_t, 128, TILE), lambda b, h, q: (b, 0, h, 0))
        o_spec = pl.BlockSpec((1, qs, 128, TILE), lambda b, h, q: (b, q, h, 0))
        alias_in, alias_specs, aliases = [], [], {}
    else:
        grid = (n_b, n_heads, 1)
        q_spec = pl.BlockSpec((1, 1, q_rows, TILE), lambda b, h, q: (b, n_lat, h, 0))
        k_spec = k_spec_fn(TILE, n_lat)
        v_spec = pl.BlockSpec((1, 1, 128, TILE), lambda b, h, q: (b, n_lat, h, 0))
        o_spec = pl.BlockSpec((1, 1, 128, TILE), lambda b, h, q: (b, n_lat, h, 0))
        alias_in, alias_specs = [ctx_into], [pl.BlockSpec(memory_space=pl.ANY)]
        aliases = {3 + len(extra): 0}
    return pl.pallas_call(
        kern,
        grid=grid,
        in_specs=[q_spec, k_spec, v_spec] + extra_specs + alias_specs,
        out_specs=o_spec,
        out_shape=jax.ShapeDtypeStruct((n_b, n_t, 1024, TILE), BF),
        scratch_shapes=scratch,
        input_output_aliases=aliases,
        compiler_params=_cparams(("arbitrary", "arbitrary", "arbitrary")),
        name=name,
    )(qT, k, vT, *extra, *alias_in)


def _diff_attn_kernel(q_ref, k_ref, v_ref, lam_ref, g_ref, *rest, n_lat, lam_init, qs, latent):
    o_ref, q2_ref, m_ref, acc_ref, s_ref = rest[-5:]
    nq = qs * TILE
    zeros = jnp.zeros((DIFF_DH, TILE), BF)
    for i in range(qs):
        q = q_ref[0, i]
        q2_ref[0:64, i * TILE:(i + 1) * TILE] = q[0:64]
        q2_ref[64:128, i * TILE:(i + 1) * TILE] = zeros
        q2_ref[0:64, nq + i * TILE:nq + (i + 1) * TILE] = zeros
        q2_ref[64:128, nq + i * TILE:nq + (i + 1) * TILE] = q[64:128]
    o = _flash_T(k_ref, v_ref, q2_ref[...], m_ref, acc_ref, s_ref, n_lat, latent)
    dl = lam_ref[...]
    lam = (jnp.exp(jnp.sum(dl[0:1] * dl[1:2], axis=1, keepdims=True))
           - jnp.exp(jnp.sum(dl[2:3] * dl[3:4], axis=1, keepdims=True)) + lam_init)
    od = o[:, 0:nq] - lam * o[:, nq:2 * nq]
    y = od * lax.rsqrt(jnp.mean(od * od, axis=0, keepdims=True) + EPS) * (1.0 - lam_init)
    for i in range(qs):
        o_ref[0, i] = (y[:, i * TILE:(i + 1) * TILE] * g_ref[...]).astype(BF)


def _diff_attn(dqT, dk, dvT, dlam, gcol, n_b, n_t, lam_init, ctx_into=None):
    qs = DIFF_QS if ctx_into is None else 1
    kern = functools.partial(_diff_attn_kernel, n_lat=n_t - 1, lam_init=lam_init, qs=qs, latent=ctx_into is None)
    k_spec_fn = lambda rows, blk: pl.BlockSpec((1, rows, 128), lambda b, h, q: (b, blk, h))
    scratch = [pltpu.VMEM((128, 2 * qs * TILE), BF), pltpu.VMEM((3, 1, 2 * qs * TILE), F32),
               pltpu.VMEM((128 + ONES_ROWS, 2 * qs * TILE), F32),
               pltpu.VMEM((2, KEY_TILES * TILE, 2 * qs * TILE), F32)]
    extra_specs = [pl.BlockSpec((4, DIFF_DH), lambda b, h, q: (0, 0)),
                   pl.BlockSpec((128, TILE), lambda b, h, q: (0, 0))]
    return _attn_call(kern, "diff_attn" if ctx_into is None else "diff_attn_ctx", 128, k_spec_fn, n_b, DIFF_H, n_t,
                      qs, scratch, dqT, dk, dvT, [dlam, gcol], extra_specs, ctx_into)


def _mla_attn_kernel(q_ref, k_ref, v_ref, *rest, n_lat, qs, latent):
    o_ref, q2_ref, m_ref, acc_ref, s_ref = rest[-5:]
    for i in range(qs):
        q2_ref[:, i * TILE:(i + 1) * TILE] = q_ref[0, i]
    o = _flash_T(k_ref.at[0], v_ref, q2_ref[...], m_ref, acc_ref, s_ref, n_lat, latent)
    for i in range(qs):
        o_ref[0, i] = o[:, i * TILE:(i + 1) * TILE].astype(BF)


def _mla_attn(mqT, mk, mvT, n_b, n_t, ctx_into=None):
    qs = MLA_QS if ctx_into is None else 1
    kern = functools.partial(_mla_attn_kernel, n_lat=n_t - 1, qs=qs, latent=ctx_into is None)
    k_spec_fn = lambda rows, blk: pl.BlockSpec((1, 1, rows, 192), lambda b, h, q: (b, h, blk, 0))
    scratch = [pltpu.VMEM((192, qs * TILE), BF), pltpu.VMEM((3, 1, qs * TILE), F32),
               pltpu.VMEM((128 + ONES_ROWS, qs * TILE), F32),
               pltpu.VMEM((2, KEY_TILES * TILE, qs * TILE), F32)]
    return _attn_call(kern, "mla_attn" if ctx_into is None else "mla_attn_ctx", 192, k_spec_fn, n_b, MLA_H, n_t,
                      qs, scratch, mqT, mk, mvT, [], [], ctx_into)


def _gla_tile(q_ref, k_ref, v_ref, la_ref, tri_ref, st_ref, reverse):
    g = la_ref[0]
    g1 = g.astype(BF)
    r1 = g - g1.astype(F32)
    g2 = r1.astype(BF)
    tri = tri_ref[...]
    bcum = _dot(tri, g1) + _dot(tri, g2)
    n_chunk = TILE // GLA_CHUNK
    n_sub = GLA_CHUNK // GLA_SUB
    rows64 = lax.broadcasted_iota(jnp.int32, (GLA_CHUNK, GLA_DK), 0)
    r16 = lax.broadcasted_iota(jnp.int32, (GLA_SUB, GLA_CHUNK), 0)
    c16 = lax.broadcasted_iota(jnp.int32, (GLA_SUB, GLA_CHUNK), 1)
    outs = []
    for hh in range(GLA_H):
        kc0, kc1 = GLA_DK * hh, GLA_DK * (hh + 1)
        o_chunks = [None] * n_chunk
        st = st_ref[hh]
        for c in (range(n_chunk - 1, -1, -1) if reverse else range(n_chunk)):
            r0 = GLA_CHUNK * c
            bc = bcum[r0:r0 + GLA_CHUNK, kc0:kc1]
            qc = q_ref[0, r0:r0 + GLA_CHUNK, kc0:kc1].astype(F32)
            kc = k_ref[0, r0:r0 + GLA_CHUNK, kc0:kc1].astype(F32)
            vc = v_ref[0, r0:r0 + GLA_CHUNK, GLA_DV * hh:GLA_DV * (hh + 1)]
            b_tot = bc[0:1] if reverse else bc[GLA_CHUNK - 1:GLA_CHUNK]
            a_rows = []
            for i in range(n_sub):
                s0 = GLA_SUB * i
                ref_row = bc[s0 + GLA_SUB - 1:s0 + GLA_SUB] if reverse else bc[s0:s0 + 1]
                qs = (qc[s0:s0 + GLA_SUB] * jnp.exp(bc[s0:s0 + GLA_SUB] - ref_row)).astype(BF)
                valid = (rows64 >= s0) if reverse else (rows64 < s0 + GLA_SUB)
                ks = (kc * jnp.exp(jnp.where(valid, ref_row - bc, -jnp.inf))).astype(BF)
                a = _dot_nt(qs, ks)
                keep = (c16 > r16 + s0) if reverse else (c16 <= r16 + s0)
                a_rows.append(jnp.where(keep, a, 0.0))
            a_mat = jnp.concatenate(a_rows, axis=0).astype(BF)
            q_in = (qc * jnp.exp(bc)).astype(BF)
            o_chunks[c] = _dot(a_mat, vc) + _dot_nt(q_in, st.astype(BF))
            k_d = (kc * jnp.exp(b_tot - bc)).astype(BF)
            st = st * jnp.exp(b_tot) + _dot_tn(vc, k_d)
        st_ref[hh] = st
        outs.append(jnp.concatenate(o_chunks, axis=0))
    return jnp.concatenate(outs, axis=1)


def _gla_fwd_kernel(q_ref, k_ref, v_ref, la_ref, tri_ref, o_ref, st_ref):
    @pl.when(pl.program_id(1) == 0)
    def _():
        st_ref[...] = jnp.zeros(st_ref.shape, F32)
    o_ref[0] = _gla_tile(q_ref, k_ref, v_ref, la_ref, tri_ref, st_ref, False)


def _gla_bwd_kernel(q_ref, k_ref, v_ref, la_ref, tri_ref, of_ref, r_ref, g_ref, y_ref, st_ref):
    @pl.when(pl.program_id(1) == 0)
    def _():
        st_ref[...] = jnp.zeros(st_ref.shape, F32)
    o = of_ref[0] + _gla_tile(q_ref, k_ref, v_ref, la_ref, tri_ref, st_ref, True)
    gn = g_ref[...]
    for hh in range(GLA_H):
        sl = slice(GLA_DV * hh, GLA_DV * (hh + 1))
        oh = o[:, sl]
        yh = oh * lax.rsqrt(jnp.mean(oh * oh, axis=-1, keepdims=True) + EPS) * gn
        y_ref[0, :, sl] = (yh * _silu(r_ref[0, :, sl].astype(F32))).astype(BF)


def _gla(gq, gk, gv, la, gr, tri_lo, tri_up, gnorm, n_b, n_t):
    ltot = n_t * TILE
    t_fwd = lambda s: (s + n_t - 1) % n_t
    t_bwd = lambda s: jnp.where(s == 0, n_t - 1, n_t - 1 - s)

    def specs(tmap, dirn):
        ts = lambda c, cb=0: pl.BlockSpec((1, TILE, c), lambda b, s: (b, tmap(s), cb))
        return [ts(512), ts(512), ts(1024), ts(512, dirn), pl.BlockSpec((TILE, TILE), lambda b, s: (0, 0))]

    o_f = pl.pallas_call(
        _gla_fwd_kernel,
        grid=(n_b, n_t),
        in_specs=specs(t_fwd, 0),
        out_specs=pl.BlockSpec((1, TILE, 1024), lambda b, s: (b, t_fwd(s), 0)),
        out_shape=jax.ShapeDtypeStruct((n_b, ltot, 1024), F32),
        scratch_shapes=[pltpu.VMEM((GLA_H, GLA_DV, GLA_DK), F32)],
        compiler_params=_cparams(("arbitrary", "arbitrary")),
        name="gla_fwd",
    )(gq, gk, gv, la, tri_lo)
    tsb = lambda c: pl.BlockSpec((1, TILE, c), lambda b, s: (b, t_bwd(s), 0))
    return pl.pallas_call(
        _gla_bwd_kernel,
        grid=(n_b, n_t),
        in_specs=specs(t_bwd, 1) + [tsb(1024), tsb(1024), pl.BlockSpec((1, GLA_DV), lambda b, s: (0, 0))],
        out_specs=tsb(1024),
        out_shape=jax.ShapeDtypeStruct((n_b, ltot, 1024), BF),
        scratch_shapes=[pltpu.VMEM((GLA_H, GLA_DV, GLA_DK), F32)],
        compiler_params=_cparams(("arbitrary", "arbitrary")),
        name="gla_bwd",
    )(gq, gk, gv, la, tri_up, o_f, gr, gnorm)


def _merge_kernel(x_ref, c_ref, ya_ref, yb_ref, yc_ref, g_ref, g1_ref, wb_ref, wo_ref, lg_ref, lb_ref, o_ref, *,
                  n_lat):
    zt = (_sigmoid(g_ref[0, 0, 0:1024, :].astype(F32)) * _dot_nt(wb_ref[0], ya_ref[0])
          + _sigmoid(g_ref[0, 0, 1024:2048, :].astype(F32)) * _dot(wb_ref[1], yb_ref[0, 0])
          + _sigmoid(g_ref[0, 0, 2048:3072, :].astype(F32)) * _dot(wb_ref[2], yc_ref[0, 0]))
    u = _dot(wo_ref[...], zt.astype(BF)).T
    y = _ln_rows(DN_ALPHA * _x_tile(x_ref, c_ref, n_lat) + g1_ref[0] * u)
    o_ref[0] = y * lg_ref[...] + lb_ref[...]


def _merge(xs, ya, ybT, ycT, gatesT, mod_l, p, n_b, n_t, n_q):
    x_lat, x_ctx, ctx_blk = xs
    tile_spec = lambda c: pl.BlockSpec((1, TILE, c), lambda b, t: (b, t, 0))
    fm_spec = lambda c: pl.BlockSpec((1, 1, c, TILE), lambda b, t: (b, t, 0, 0))
    mrow = lambda b, t: jnp.where(t == n_t - 1, n_b, b)
    return pl.pallas_call(
        functools.partial(_merge_kernel, n_lat=n_t - 1),
        grid=(n_b, n_q),
        in_specs=_x_specs(n_t - 1, ctx_blk) + [
                  tile_spec(1024), fm_spec(1024), fm_spec(1024), fm_spec(3072),
                  pl.BlockSpec((1, 1, D_MODEL), lambda b, t: (mrow(b, t), 0, 2)),
                  _resident((3, D_MODEL, D_MODEL)), _resident((D_MODEL, D_MODEL)),
                  _resident((1, D_MODEL)), _resident((1, D_MODEL))],
        out_specs=tile_spec(D_MODEL),
        out_shape=jax.ShapeDtypeStruct((n_b, n_q * TILE, D_MODEL), F32),
        compiler_params=_cparams(("arbitrary", "arbitrary")),
        name="merge",
    )(x_lat, x_ctx, ya, ybT, ycT, gatesT, mod_l, p["wbT"], p["woT"], p["ln1_g"], p["ln1_b"])


def _ffn_kernel(x_ref, sh_ref, sc_ref, g2_ref, wi_ref, wo_ref, lg_ref, lb_ref, o_ref):
    x = x_ref[0]
    h = (_ln_rows(x) * (1.0 + sc_ref[0]) + sh_ref[0]).astype(BF)
    acc = jnp.zeros((TILE, D_MODEL), F32)
    for a, b in FFN_CHUNKS:
        gate = _dot(h, wi_ref[:, a:b])
        up = _dot(h, wi_ref[:, FFN_H + a:FFN_H + b])
        acc = acc + _dot((_silu(gate) * up).astype(BF), wo_ref[a:b, :])
    y = _ln_rows(DN_ALPHA * x + g2_ref[0] * acc)
    o_ref[0] = y * lg_ref[...] + lb_ref[...]


def _ffn(x1, mod_l, p, n_b, n_t, n_q):
    tile_spec = lambda c: pl.BlockSpec((1, TILE, c), lambda b, t: (b, t, 0))
    mrow = lambda b, t: jnp.where(t == n_t - 1, n_b, b)
    mspec = lambda j: pl.BlockSpec((1, 1, D_MODEL), lambda b, t: (mrow(b, t), 0, j))
    return pl.pallas_call(
        _ffn_kernel,
        grid=(n_b, n_q),
        in_specs=[tile_spec(D_MODEL), mspec(3), mspec(4), mspec(5),
                  _resident((D_MODEL, 2 * FFN_H)), _resident((FFN_H, D_MODEL)),
                  _resident((1, D_MODEL)), _resident((1, D_MODEL))],
        out_specs=tile_spec(D_MODEL),
        out_shape=jax.ShapeDtypeStruct((n_b, n_q * TILE, D_MODEL), F32),
        compiler_params=_cparams(("arbitrary", "arbitrary")),
        name="ffn",
    )(x1, mod_l, mod_l, mod_l, p["ffn_wi"], p["ffn_wo"], p["ln2_g"], p["ln2_b"])


def _prep_layer(l, w_in, gla_w_a2, gla_b_a, mla_q_norm_g, mla_kv_norm_g, mla_w_uq, mla_w_ukv, w_branch, w_out,
                ln1_g, ln1_b, ffn_w_in, ffn_w_out, ln2_g, ln2_b):
    w = w_in[l]
    seg = lambda a, b: w[:, a:b]
    w_tok = jnp.concatenate([
        seg(O_GQ, O_GK), seg(O_GK, O_GV), seg(O_GV, O_GR), seg(O_GR, O_GA), seg(O_DK, O_DV),
        seg(O_MQ, O_MKV), seg(O_MKV, O_MKR),
        seg(O_MKR, O_GATES), seg(O_GA, O_DQ), jnp.zeros((D_MODEL, 32), F32)], axis=1).astype(BF)
    w_featT = jnp.concatenate([seg(O_DQ, O_DK) * DIFF_DH ** -0.5, seg(O_DV, O_MQ), seg(O_GATES, O_END)],
                              axis=1).T.astype(BF)
    wa2 = jnp.zeros((128, 1024), F32)
    wa2 = wa2.at[64:80, 0:512].set(gla_w_a2[l, 0]).at[80:96, 512:1024].set(gla_w_a2[l, 1]).astype(BF)
    ukv = mla_w_ukv[l].reshape(MLA_KVR, MLA_H, MLA_NOPE + MLA_DV)
    return dict(
        w_tok=w_tok, w_featT=w_featT, wa2=wa2, ba=gla_b_a[l].reshape(1, 1024),
        qg=mla_q_norm_g[l].reshape(1, MLA_QR), kvg=mla_kv_norm_g[l].reshape(1, MLA_KVR),
        wukv_k=ukv[:, :, :MLA_NOPE].reshape(MLA_KVR, 1024).astype(BF),
        wukv_vT=ukv[:, :, MLA_NOPE:].reshape(MLA_KVR, 1024).T.astype(BF),
        wuqT=mla_w_uq[l].T.astype(BF),
        wbT=jnp.swapaxes(w_branch[l], 1, 2).astype(BF), woT=w_out[l].T.astype(BF),
        ln1_g=ln1_g[l].reshape(1, D_MODEL), ln1_b=ln1_b[l].reshape(1, D_MODEL),
        ffn_wi=ffn_w_in[l].astype(BF), ffn_wo=ffn_w_out[l].astype(BF),
        ln2_g=ln2_g[l].reshape(1, D_MODEL), ln2_b=ln2_b[l].reshape(1, D_MODEL))


def _rope_tables(l_lat, l_ctx):
    rows = l_lat // GRID_W
    pos_row = jnp.broadcast_to(jnp.arange(rows, dtype=F32)[:, None], (rows, GRID_W)).reshape(l_lat)
    pos_col = jnp.broadcast_to(jnp.arange(GRID_W, dtype=F32)[None, :], (rows, GRID_W)).reshape(l_lat)
    d_axis = ROPE_DIM // 2
    inv = ROPE_BASE ** (-jnp.arange(0, d_axis, 2, dtype=F32) / d_axis)
    ang = jnp.concatenate([pos_row[:, None] * inv, pos_col[:, None] * inv], axis=-1)
    cos = jnp.concatenate([jnp.cos(ang), jnp.ones((l_ctx, 32), F32)], axis=0)
    sin = jnp.concatenate([jnp.sin(ang), jnp.zeros((l_ctx, 32), F32)], axis=0)
    ctok = jnp.tile(cos, (1, 4))
    stok = jnp.tile(jnp.concatenate([-sin, sin], axis=1), (1, 2))
    n_t = (l_lat + l_ctx) // TILE
    to_fm = lambda a: a.T.reshape(32, n_t, TILE).transpose(1, 0, 2)
    return ctok, stok, to_fm(cos), to_fm(sin)


def kernel(x, c, ctx, c_ctx, w_mod, b_mod, w_in, gla_w_a2, gla_b_a, gla_norm_g, diff_lam, diff_norm_g,
           mla_q_norm_g, mla_kv_norm_g, mla_w_uq, mla_w_ukv, w_branch, w_out, ln1_g, ln1_b, ffn_w_in, ffn_w_out,
           ln2_g, ln2_b):
    n_b, l_lat, _ = x.shape
    l_ctx = ctx.shape[1]
    assert l_ctx == TILE and n_b + 1 <= 16
    assert l_lat % (TILE * max(DIFF_QS, MLA_QS)) == 0 and l_lat % (TILE * KEY_TILES * KEY_UNROLL) == 0
    ltot = l_lat + l_ctx
    n_t = ltot // TILE
    ctok, stok, cosT, sinT = _rope_tables(l_lat, l_ctx)
    c_all = jnp.zeros((16, D_MODEL), F32).at[:n_b].set(c).at[n_b].set(c_ctx)
    mod = _modulation(c_all, w_mod, b_mod)
    ii = lax.broadcasted_iota(jnp.int32, (TILE, TILE), 0)
    jj = lax.broadcasted_iota(jnp.int32, (TILE, TILE), 1)
    same = (ii // GLA_CHUNK) == (jj // GLA_CHUNK)
    tri_lo = (same & (jj <= ii)).astype(BF)
    tri_up = (same & (jj >= ii)).astype(BF)
    xs = (x, ctx, 0)
    for l in range(N_LAYERS):
        last = l == N_LAYERS - 1
        n_q = n_t - 1 if last else n_t
        lam_init = 0.8 - 0.6 * math.exp(-0.3 * l)
        p = _prep_layer(l, w_in, gla_w_a2, gla_b_a, mla_q_norm_g, mla_kv_norm_g, mla_w_uq, mla_w_ukv, w_branch,
                        w_out, ln1_g, ln1_b, ffn_w_in, ffn_w_out, ln2_g, ln2_b)
        mod_l = mod[l].reshape(16, 1, 6 * D_MODEL)
        gq, gk, gv, gr, dk, la, mk, cq, ckv = _proj_tok(xs, mod_l, ctok, stok, p, n_b, n_t, ltot)
        dqT, dvT, gatesT, mqT, mvT = _proj_feat(xs, mod_l, cosT, sinT, cq, ckv, p, n_b, n_t)
        ya = _gla(gq, gk, gv, la, gr, tri_lo, tri_up, gla_norm_g[l].reshape(1, GLA_DV), n_b, n_t)
        gcol = jnp.broadcast_to(diff_norm_g[l].reshape(128, 1), (128, TILE))
        ybT = _diff_attn(dqT, dk, dvT, diff_lam[l], gcol, n_b, n_t, lam_init)
        ycT = _mla_attn(mqT, mk, mvT, n_b, n_t)
        if not last:
            ybT = _diff_attn(dqT, dk, dvT, diff_lam[l], gcol, n_b, n_t, lam_init, ctx_into=ybT)
            ycT = _mla_attn(mqT, mk, mvT, n_b, n_t, ctx_into=ycT)
        x1 = _merge(xs, ya, ybT, ycT, gatesT, mod_l, p, n_b, n_t, n_q)
        x2 = _ffn(x1, mod_l, p, n_b, n_t, n_q)
        xs = (x2, x2, n_t - 1)
    return x2
```

```python
import functools
import math

import jax
import jax.numpy as jnp
from jax import lax
from jax.experimental import pallas as pl
from jax.experimental.pallas import tpu as pltpu

BF = jnp.bfloat16
F32 = jnp.float32

D_MODEL = 1024
N_LAYERS = 2
GRID_W = 64
TILE = 256
GLA_H, GLA_DK, GLA_DV, GLA_RANK, GLA_TAU = 4, 128, 256, 16, 16.0
GLA_CHUNK, GLA_SUB = 64, 16
GLA_BATCH = 2
TOK_BATCH = 2
DIFF_H, DIFF_DH = 8, 64
MLA_H, MLA_QR, MLA_KVR, MLA_NOPE, MLA_ROPE, MLA_DV = 8, 256, 128, 128, 64, 128
MLA_SCALE = (MLA_NOPE + MLA_ROPE) ** -0.5
ROPE_DIM, ROPE_BASE = 64, 10000.0
FFN_H = 2816
FFN_CHUNKS = ((0, 1536), (1536, 2816))
DN_ALPHA = (2 * N_LAYERS) ** 0.25
EPS = 1e-6
DIFF_QS, MLA_QS = 2, 4
KEY_TILES = 2
KEY_UNROLL = 4
LOG2E = math.log2(math.e)
ONES_ROWS = 16
VMEM_LIMIT = 56 * 1024 * 1024

_SIZES = (512, 512, 1024, 1024, 32, 1024, 1024, 1024, 256, 128, 64, 3072)
_OFF = [0]
for _s in _SIZES:
    _OFF.append(_OFF[-1] + _s)
(O_GQ, O_GK, O_GV, O_GR, O_GA, O_DQ, O_DK, O_DV, O_MQ, O_MKV, O_MKR, O_GATES, O_END) = _OFF

T_GQ, T_GK, T_GV, T_GR, T_DK, T_MQ, T_MKV, T_SMA, T_END = (0, 512, 1024, 2048, 3072, 4096, 4352, 4480, 4608)
F_DQ, F_DV, F_GATES, F_END = 0, 1024, 2048, 5120


def _cparams(sem):
    return pltpu.CompilerParams(dimension_semantics=sem, vmem_limit_bytes=VMEM_LIMIT)


def _resident(shape):
    nd = len(shape)
    return pl.BlockSpec(shape, lambda *_: (0,) * nd, pipeline_mode=pl.Buffered(1))


def _dot(a, b):
    return jnp.dot(a, b, preferred_element_type=F32)


def _dot_nt(a, b):
    return lax.dot_general(a, b, (((1,), (1,)), ((), ())), preferred_element_type=F32)


def _dot_tn(a, b):
    return lax.dot_general(a, b, (((0,), (0,)), ((), ())), preferred_element_type=F32)


def _ln_rows(x):
    mu = jnp.mean(x, axis=-1, keepdims=True)
    xc = x - mu
    var = jnp.mean(xc * xc, axis=-1, keepdims=True)
    return xc * lax.rsqrt(var + EPS)


def _sigmoid(x):
    return 1.0 / (1.0 + jnp.exp(-x))


def _silu(x):
    return x * _sigmoid(x)


def _mod_kernel(c_ref, w_ref, b_ref, o_ref):
    ca = _silu(c_ref[...]).astype(BF)
    o_ref[0] = _dot(ca, w_ref[0].astype(BF)) + b_ref[0]


def _modulation(c_all, w_mod, b_mod):
    nl, d, n6 = w_mod.shape
    r = c_all.shape[0]
    nblk = 1536
    return pl.pallas_call(
        _mod_kernel,
        grid=(nl, n6 // nblk),
        in_specs=[pl.BlockSpec((r, d), lambda l, j: (0, 0)),
                  pl.BlockSpec((1, d, nblk), lambda l, j: (l, 0, j)),
                  pl.BlockSpec((1, 1, nblk), lambda l, j: (l, 0, j))],
        out_specs=pl.BlockSpec((1, r, nblk), lambda l, j: (l, 0, j)),
        out_shape=jax.ShapeDtypeStruct((nl, r, n6), F32),
        compiler_params=_cparams(("arbitrary", "arbitrary")),
        name="modulation",
    )(c_all, w_mod, b_mod.reshape(nl, 1, n6))


def _batch_rows(n_b):
    return TOK_BATCH if n_b % TOK_BATCH == 0 else 1


def _per_batch_row(body, n_rowed, n_shared):
    def kern(*refs, **kw):
        rowed, shared, outs = refs[:n_rowed], refs[n_rowed:n_rowed + n_shared], refs[n_rowed + n_shared:]
        for bi in range(rowed[0].shape[0]):
            row = lambda ref: ref.at[pl.ds(bi, 1)]
            body(*map(row, rowed), *shared, *map(row, outs), **kw)
    return kern


def _x_specs(rows, n_lat, ctx_blk):
    return [pl.BlockSpec((rows, TILE, D_MODEL), lambda b, t: (b, jnp.minimum(t, n_lat - 1), 0)),
            pl.BlockSpec((rows, TILE, D_MODEL), lambda b, t: (b, ctx_blk, 0))]


def _x_tile(x_ref, c_ref, n_lat):
    return jnp.where(pl.program_id(1) == n_lat, c_ref[0], x_ref[0])


def _proj_tok_kernel(x_ref, c_ref, sh_ref, sc_ref, ct_ref, st_ref, w_ref, wa2_ref, ba_ref, qg_ref, kvg_ref, wk_ref,
                     gq_ref, gk_ref, gv_ref, gr_ref, dk_ref, la_ref, mk_ref, cq_ref, ckv_ref, *, n_lat):
    h = (_ln_rows(_x_tile(x_ref, c_ref, n_lat)) * (1.0 + sc_ref[0]) + sh_ref[0]).astype(BF)

    def proj(a, b):
        return _dot(h, w_ref[:, a:b])

    gq_ref[0] = (proj(T_GQ, T_GK) * GLA_DK ** -0.5).astype(BF)
    gk_ref[0] = proj(T_GK, T_GV).astype(BF)
    gv_ref[0] = proj(T_GV, T_GR).astype(BF)
    gr_ref[0] = proj(T_GR, T_DK).astype(BF)
    ct = ct_ref[...]
    st = st_ref[...]
    first_half = lax.broadcasted_iota(jnp.int32, (TILE, 128), 1) % ROPE_DIM < ROPE_DIM // 2

    def rope(x):
        partner = jnp.where(first_half, pltpu.roll(x, 128 - ROPE_DIM // 2, 1), pltpu.roll(x, ROPE_DIM // 2, 1))
        return x * ct + partner * st

    dkx = proj(T_DK, T_MQ)
    for hh in range(DIFF_H):
        sl = slice(128 * hh, 128 * (hh + 1))
        dk_ref[0, :, sl] = rope(dkx[:, sl]).astype(BF)
    mq = proj(T_MQ, T_MKV)
    cq = mq * lax.rsqrt(jnp.mean(mq * mq, axis=-1, keepdims=True) + EPS) * qg_ref[...]
    cq_ref[0] = cq.astype(BF)
    mm = proj(T_MKV, T_END)
    mkv = mm[:, 0:128]
    sma = mm[:, 128:256]
    ckv = (mkv * lax.rsqrt(jnp.mean(mkv * mkv, axis=-1, keepdims=True) + EPS) * kvg_ref[...]).astype(BF)
    ckv_ref[0] = ckv
    kn = _dot(ckv, wk_ref[...]).astype(BF)
    kr = rope(sma)[:, 0:64].astype(BF)
    for hh in range(MLA_H):
        mk_ref[0, hh, :, 0:128] = kn[:, 128 * hh:128 * (hh + 1)]
        mk_ref[0, hh, :, 128:192] = kr
    z = _dot(sma.astype(BF), wa2_ref[...]) + ba_ref[...]
    la_ref[0] = (jnp.minimum(z, 0.0) - jnp.log(1.0 + jnp.exp(-jnp.abs(z)))) * (1.0 / GLA_TAU)


def _proj_tok(xs, mod_l, ctok, stok, p, n_b, n_t, ltot):
    x_lat, x_ctx, ctx_blk = xs
    rows = _batch_rows(n_b)
    tile_spec = lambda c: pl.BlockSpec((rows, TILE, c), lambda b, t: (b, t, 0))
    mrow = lambda b, t: jnp.where(t == n_t - 1, n_b // rows, b)
    bf_out = lambda c: jax.ShapeDtypeStruct((n_b, ltot, c), BF)
    return pl.pallas_call(
        functools.partial(_per_batch_row(_proj_tok_kernel, 4, 8), n_lat=n_t - 1),
        grid=(n_b // rows, n_t),
        in_specs=_x_specs(rows, n_t - 1, ctx_blk) + [
                  pl.BlockSpec((rows, 1, D_MODEL), lambda b, t: (mrow(b, t), 0, 0)),
                  pl.BlockSpec((rows, 1, D_MODEL), lambda b, t: (mrow(b, t), 0, 1)),
                  pl.BlockSpec((TILE, 128), lambda b, t: (t, 0)),
                  pl.BlockSpec((TILE, 128), lambda b, t: (t, 0)),
                  _resident((D_MODEL, T_END)), _resident((128, 1024)), _resident((1, 1024)),
                  _resident((1, MLA_QR)), _resident((1, MLA_KVR)), _resident((MLA_KVR, 1024))],
        out_specs=[tile_spec(512), tile_spec(512), tile_spec(1024), tile_spec(1024), tile_spec(1024),
                   tile_spec(1024),
                   pl.BlockSpec((rows, MLA_H, TILE, 192), lambda b, t: (b, 0, t, 0)),
                   tile_spec(MLA_QR), tile_spec(MLA_KVR)],
        out_shape=[bf_out(512), bf_out(512), bf_out(1024), bf_out(1024), bf_out(1024),
                   jax.ShapeDtypeStruct((n_b, ltot, 1024), F32),
                   jax.ShapeDtypeStruct((n_b, MLA_H, ltot, 192), BF),
                   bf_out(MLA_QR), bf_out(MLA_KVR)],
        compiler_params=_cparams(("arbitrary", "arbitrary")),
        name="proj_tok",
    )(x_lat, x_ctx, mod_l, mod_l, ctok, stok, p["w_tok"], p["wa2"], p["ba"], p["qg"], p["kvg"], p["wukv_k"])


def _rope_rows(x1, x2, cos, sin):
    return x1 * cos - x2 * sin, x1 * sin + x2 * cos


def _proj_feat_kernel(x_ref, c_ref, sh_ref, sc_ref, cq_ref, ckv_ref, cos_ref, sin_ref, w_ref, wuq_ref, wv_ref,
                      dq_ref, dv_ref, g_ref, mq_ref, mv_ref, *, n_lat):
    h = (_ln_rows(_x_tile(x_ref, c_ref, n_lat)) * (1.0 + sc_ref[0]) + sh_ref[0]).astype(BF)
    cos = cos_ref[0]
    sin = sin_ref[0]
    dq = _dot_nt(w_ref[F_DQ:F_DV, :], h) * LOG2E
    for g in range(2 * DIFF_H):
        o1, o2 = _rope_rows(dq[64 * g:64 * g + 32], dq[64 * g + 32:64 * (g + 1)], cos, sin)
        dq_ref[0, 0, 64 * g:64 * g + 32, :] = o1.astype(BF)
        dq_ref[0, 0, 64 * g + 32:64 * (g + 1), :] = o2.astype(BF)
    dv_ref[0, 0] = _dot_nt(w_ref[F_DV:F_GATES, :], h).astype(BF)
    for j in range(3):
        a = F_GATES + 1024 * j
        g_ref[0, 0, 1024 * j:1024 * (j + 1), :] = _dot_nt(w_ref[a:a + 1024, :], h).astype(BF)
    cq = cq_ref[0]
    mq = _dot_nt(wuq_ref[...], cq) * (MLA_SCALE * LOG2E)
    for hh in range(MLA_H):
        a = 192 * hh
        o1, o2 = _rope_rows(mq[a + 128:a + 160], mq[a + 160:a + 192], cos, sin)
        mq_ref[0, 0, a:a + 128, :] = mq[a:a + 128].astype(BF)
        mq_ref[0, 0, a + 128:a + 160, :] = o1.astype(BF)
        mq_ref[0, 0, a + 160:a + 192, :] = o2.astype(BF)
    mv_ref[0, 0] = _dot_nt(wv_ref[...], ckv_ref[0]).astype(BF)


def _proj_feat(xs, mod_l, cosT, sinT, cq, ckv, p, n_b, n_t):
    x_lat, x_ctx, ctx_blk = xs
    rows = _batch_rows(n_b)
    tile_spec = lambda c: pl.BlockSpec((rows, TILE, c), lambda b, t: (b, t, 0))
    mrow = lambda b, t: jnp.where(t == n_t - 1, n_b // rows, b)
    fm_spec = lambda c: pl.BlockSpec((rows, 1, c, TILE), lambda b, t: (b, t, 0, 0))
    fm_out = lambda c: jax.ShapeDtypeStruct((n_b, n_t, c, TILE), BF)
    return pl.pallas_call(
        functools.partial(_per_batch_row(_proj_feat_kernel, 6, 5), n_lat=n_t - 1),
        grid=(n_b // rows, n_t),
        in_specs=_x_specs(rows, n_t - 1, ctx_blk) + [
                  pl.BlockSpec((rows, 1, D_MODEL), lambda b, t: (mrow(b, t), 0, 0)),
                  pl.BlockSpec((rows, 1, D_MODEL), lambda b, t: (mrow(b, t), 0, 1)),
                  tile_spec(MLA_QR), tile_spec(MLA_KVR),
                  pl.BlockSpec((1, 32, TILE), lambda b, t: (t, 0, 0)),
                  pl.BlockSpec((1, 32, TILE), lambda b, t: (t, 0, 0)),
                  _resident((F_END, D_MODEL)), _resident((MLA_H * 192, MLA_QR)), _resident((1024, MLA_KVR))],
        out_specs=[fm_spec(1024), fm_spec(1024), fm_spec(3072), fm_spec(MLA_H * 192), fm_spec(1024)],
        out_shape=[fm_out(1024), fm_out(1024), fm_out(3072), fm_out(MLA_H * 192), fm_out(1024)],
        compiler_params=_cparams(("arbitrary", "arbitrary")),
        name="proj_feat",
    )(x_lat, x_ctx, mod_l, mod_l, cq, ckv, cosT, sinT, p["w_featT"], p["wuqT"], p["wukv_vT"])


def _flash_T(k_ref, v_ref, q2, m_ref, acc_ref, s_ref, n_lat, latent_keys):
    dv = v_ref.shape[2]
    m_ref[2] = jnp.full(m_ref.shape[1:], -jnp.inf, F32)
    acc_ref[...] = jnp.zeros(acc_ref.shape, F32)
    rows = KEY_TILES * TILE

    def scores(slot, tile0):
        s = _dot(k_ref[0, pl.ds(pl.multiple_of(tile0 * TILE, TILE), rows), :], q2)
        s_ref[slot] = s
        m_ref[slot] = jnp.max(s, axis=0, keepdims=True)

    def consume(s, s_max, tile0, ntiles):
        m_old = m_ref[2]
        m_new = jnp.maximum(m_old, s_max)
        alpha = jnp.exp2(m_old - m_new)
        p = jnp.exp2(s - m_new).astype(BF)
        v = jnp.concatenate([v_ref[0, tile0 + i] for i in range(ntiles)], axis=1)
        vext = jnp.concatenate([v, jnp.ones((ONES_ROWS, ntiles * TILE), BF)], axis=0)
        acc_ref[...] = acc_ref[...] * alpha + _dot(vext, p)
        m_ref[2] = m_new

    def ctx_step(tile):
        s = _dot(k_ref[0, pl.ds(tile * TILE, TILE), :], q2)
        return s, jnp.max(s, axis=0, keepdims=True)

    if latent_keys:
        n_steps = n_lat // KEY_TILES
        scores(0, 0)

        def body(jj, carry):
            for u in range(KEY_UNROLL):
                j = KEY_UNROLL * jj + u
                scores((u + 1) % 2, (j + 1) * KEY_TILES)
                consume(s_ref[u % 2], m_ref[u % 2], j * KEY_TILES, KEY_TILES)
            return carry
        lax.fori_loop(0, n_steps // KEY_UNROLL - 1, body, 0)
        for u in range(KEY_UNROLL):
            j = n_steps - KEY_UNROLL + u
            if u < KEY_UNROLL - 1:
                scores((u + 1) % 2, (j + 1) * KEY_TILES)
            else:
                s_ctx, mx_ctx = ctx_step(n_lat)
            consume(s_ref[u % 2], m_ref[u % 2], j * KEY_TILES, KEY_TILES)
        consume(s_ctx, mx_ctx, n_lat, 1)
    else:
        consume(*ctx_step(0), 0, 1)
    acc = acc_ref[...]
    return acc[0:dv] * (1.0 / acc[dv:dv + 1])


def _attn_call(kern, name, q_rows, k_spec_fn, n_b, n_heads, n_t, qs, scratch, qT, k, vT, extra, extra_specs,
               ctx_into=None):
    n_lat = n_t - 1
    if ctx_into is None:
        grid = (n_b, n_heads, n_lat // qs)
        q_spec = pl.BlockSpec((1, qs, q_rows, TILE), lambda b, h, q: (b, q, h, 0))
        k_spec = k_spec_fn(n_t * TILE, 0)
        v_spec = pl.BlockSpec((1, n_t, 128, TILE), lambda b, h, q: (b, 0, h, 0))
        o_spec = pl.BlockSpec((1, qs, 128, TILE), lambda b, h, q: (b, q, h, 0))
        alias_in, alias_specs, aliases = [], [], {}
    else:
        grid = (n_b, n_heads, 1)
        q_spec = pl.BlockSpec((1, 1, q_rows, TILE), lambda b, h, q: (b, n_lat, h, 0))
        k_spec = k_spec_fn(TILE, n_lat)
        v_spec = pl.BlockSpec((1, 1, 128, TILE), lambda b, h, q: (b, n_lat, h, 0))
        o_spec = pl.BlockSpec((1, 1, 128, TILE), lambda b, h, q: (b, n_lat, h, 0))
        alias_in, alias_specs = [ctx_into], [pl.BlockSpec(memory_space=pl.ANY)]
        aliases = {3 + len(extra): 0}
    return pl.pallas_call(
        kern,
        grid=grid,
        in_specs=[q_spec, k_spec, v_spec] + extra_specs + alias_specs,
        out_specs=o_spec,
        out_shape=jax.ShapeDtypeStruct((n_b, n_t, 1024, TILE), BF),
        scratch_shapes=scratch,
        input_output_aliases=aliases,
        compiler_params=_cparams(("arbitrary", "arbitrary", "arbitrary")),
        name=name,
    )(qT, k, vT, *extra, *alias_in)


def _diff_attn_kernel(q_ref, k_ref, v_ref, lam_ref, g_ref, *rest, n_lat, lam_init, qs, latent):
    o_ref, q2_ref, m_ref, acc_ref, s_ref = rest[-5:]
    nq = qs * TILE
    zeros = jnp.zeros((DIFF_DH, TILE), BF)
    for i in range(qs):
        q = q_ref[0, i]
        q2_ref[0:64, i * TILE:(i + 1) * TILE] = q[0:64]
        q2_ref[64:128, i * TILE:(i + 1) * TILE] = zeros
        q2_ref[0:64, nq + i * TILE:nq + (i + 1) * TILE] = zeros
        q2_ref[64:128, nq + i * TILE:nq + (i + 1) * TILE] = q[64:128]
    o = _flash_T(k_ref, v_ref, q2_ref[...], m_ref, acc_ref, s_ref, n_lat, latent)
    dl = lam_ref[...]
    lam = (jnp.exp(jnp.sum(dl[0:1] * dl[1:2], axis=1, keepdims=True))
           - jnp.exp(jnp.sum(dl[2:3] * dl[3:4], axis=1, keepdims=True)) + lam_init)
    od = o[:, 0:nq] - lam * o[:, nq:2 * nq]
    y = od * lax.rsqrt(jnp.mean(od * od, axis=0, keepdims=True) + EPS) * (1.0 - lam_init)
    for i in range(qs):
        o_ref[0, i] = (y[:, i * TILE:(i + 1) * TILE] * g_ref[...]).astype(BF)


def _diff_attn(dqT, dk, dvT, dlam, gcol, n_b, n_t, lam_init, ctx_into=None):
    qs = DIFF_QS if ctx_into is None else 1
    kern = functools.partial(_diff_attn_kernel, n_lat=n_t - 1, lam_init=lam_init, qs=qs, latent=ctx_into is None)
    k_spec_fn = lambda rows, blk: pl.BlockSpec((1, rows, 128), lambda b, h, q: (b, blk, h))
    scratch = [pltpu.VMEM((128, 2 * qs * TILE), BF), pltpu.VMEM((3, 1, 2 * qs * TILE), F32),
               pltpu.VMEM((128 + ONES_ROWS, 2 * qs * TILE), F32),
               pltpu.VMEM((2, KEY_TILES * TILE, 2 * qs * TILE), F32)]
    extra_specs = [pl.BlockSpec((4, DIFF_DH), lambda b, h, q: (0, 0)),
                   pl.BlockSpec((128, TILE), lambda b, h, q: (0, 0))]
    return _attn_call(kern, "diff_attn" if ctx_into is None else "diff_attn_ctx", 128, k_spec_fn, n_b, DIFF_H, n_t,
                      qs, scratch, dqT, dk, dvT, [dlam, gcol], extra_specs, ctx_into)


def _mla_attn_kernel(q_ref, k_ref, v_ref, *rest, n_lat, qs, latent):
    o_ref, q2_ref, m_ref, acc_ref, s_ref = rest[-5:]
    for i in range(qs):
        q2_ref[:, i * TILE:(i + 1) * TILE] = q_ref[0, i]
    o = _flash_T(k_ref.at[0], v_ref, q2_ref[...], m_ref, acc_ref, s_ref, n_lat, latent)
    for i in range(qs):
        o_ref[0, i] = o[:, i * TILE:(i + 1) * TILE].astype(BF)


def _mla_attn(mqT, mk, mvT, n_b, n_t, ctx_into=None):
    qs = MLA_QS if ctx_into is None else 1
    kern = functools.partial(_mla_attn_kernel, n_lat=n_t - 1, qs=qs, latent=ctx_into is None)
    k_spec_fn = lambda rows, blk: pl.BlockSpec((1, 1, rows, 192), lambda b, h, q: (b, h, blk, 0))
    scratch = [pltpu.VMEM((192, qs * TILE), BF), pltpu.VMEM((3, 1, qs * TILE), F32),
               pltpu.VMEM((128 + ONES_ROWS, qs * TILE), F32),
               pltpu.VMEM((2, KEY_TILES * TILE, qs * TILE), F32)]
    return _attn_call(kern, "mla_attn" if ctx_into is None else "mla_attn_ctx", 192, k_spec_fn, n_b, MLA_H, n_t,
                      qs, scratch, mqT, mk, mvT, [], [], ctx_into)


def _gla_tile(q_ref, k_ref, v_ref, la_ref, tri_ref, st_ref, bi, reverse):
    g = la_ref[bi]
    g1 = g.astype(BF)
    r1 = g - g1.astype(F32)
    g2 = r1.astype(BF)
    tri = tri_ref[...]
    bcum = _dot(tri, g1) + _dot(tri, g2)
    n_chunk = TILE // GLA_CHUNK
    n_sub = GLA_CHUNK // GLA_SUB
    rows64 = lax.broadcasted_iota(jnp.int32, (GLA_CHUNK, GLA_DK), 0)
    r16 = lax.broadcasted_iota(jnp.int32, (GLA_SUB, GLA_CHUNK), 0)
    c16 = lax.broadcasted_iota(jnp.int32, (GLA_SUB, GLA_CHUNK), 1)
    outs = []
    for hh in range(GLA_H):
        kc0, kc1 = GLA_DK * hh, GLA_DK * (hh + 1)
        o_chunks = [None] * n_chunk
        st = st_ref[bi, hh]
        for c in (range(n_chunk - 1, -1, -1) if reverse else range(n_chunk)):
            r0 = GLA_CHUNK * c
            bc = bcum[r0:r0 + GLA_CHUNK, kc0:kc1]
            qc = q_ref[bi, r0:r0 + GLA_CHUNK, kc0:kc1].astype(F32)
            kc = k_ref[bi, r0:r0 + GLA_CHUNK, kc0:kc1].astype(F32)
            vc = v_ref[bi, r0:r0 + GLA_CHUNK, GLA_DV * hh:GLA_DV * (hh + 1)]
            b_tot = bc[0:1] if reverse else bc[GLA_CHUNK - 1:GLA_CHUNK]
            a_rows = []
            for i in range(n_sub):
                s0 = GLA_SUB * i
                ref_row = bc[s0 + GLA_SUB - 1:s0 + GLA_SUB] if reverse else bc[s0:s0 + 1]
                qs = (qc[s0:s0 + GLA_SUB] * jnp.exp(bc[s0:s0 + GLA_SUB] - ref_row)).astype(BF)
                valid = (rows64 >= s0) if reverse else (rows64 < s0 + GLA_SUB)
                ks = (kc * jnp.exp(jnp.where(valid, ref_row - bc, -jnp.inf))).astype(BF)
                a = _dot_nt(qs, ks)
                keep = (c16 > r16 + s0) if reverse else (c16 <= r16 + s0)
                a_rows.append(jnp.where(keep, a, 0.0))
            a_mat = jnp.concatenate(a_rows, axis=0).astype(BF)
            q_in = (qc * jnp.exp(bc)).astype(BF)
            o_chunks[c] = _dot(a_mat, vc) + _dot_nt(q_in, st.astype(BF))
            k_d = (kc * jnp.exp(b_tot - bc)).astype(BF)
            st = st * jnp.exp(b_tot) + _dot_tn(vc, k_d)
        st_ref[bi, hh] = st
        outs.append(jnp.concatenate(o_chunks, axis=0))
    return jnp.concatenate(outs, axis=1)


def _gla_fwd_kernel(q_ref, k_ref, v_ref, la_ref, tri_ref, o_ref, st_ref):
    @pl.when(pl.program_id(1) == 0)
    def _():
        st_ref[...] = jnp.zeros(st_ref.shape, F32)
    for bi in range(o_ref.shape[0]):
        o_ref[bi] = _gla_tile(q_ref, k_ref, v_ref, la_ref, tri_ref, st_ref, bi, False)


def _gla_bwd_kernel(q_ref, k_ref, v_ref, la_ref, tri_ref, of_ref, r_ref, g_ref, y_ref, st_ref):
    @pl.when(pl.program_id(1) == 0)
    def _():
        st_ref[...] = jnp.zeros(st_ref.shape, F32)
    gn = g_ref[...]
    for bi in range(y_ref.shape[0]):
        o = of_ref[bi] + _gla_tile(q_ref, k_ref, v_ref, la_ref, tri_ref, st_ref, bi, True)
        for hh in range(GLA_H):
            sl = slice(GLA_DV * hh, GLA_DV * (hh + 1))
            oh = o[:, sl]
            yh = oh * lax.rsqrt(jnp.mean(oh * oh, axis=-1, keepdims=True) + EPS) * gn
            y_ref[bi, :, sl] = (yh * _silu(r_ref[bi, :, sl].astype(F32))).astype(BF)


def _gla(gq, gk, gv, la, gr, tri_lo, tri_up, gnorm, n_b, n_t):
    ltot = n_t * TILE
    rows = GLA_BATCH if n_b % GLA_BATCH == 0 else 1
    t_fwd = lambda s: (s + n_t - 1) % n_t
    t_bwd = lambda s: jnp.where(s == 0, n_t - 1, n_t - 1 - s)

    def specs(tmap, dirn):
        ts = lambda c, cb=0: pl.BlockSpec((rows, TILE, c), lambda b, s: (b, tmap(s), cb))
        return [ts(512), ts(512), ts(1024), ts(512, dirn), pl.BlockSpec((TILE, TILE), lambda b, s: (0, 0))]

    o_f = pl.pallas_call(
        _gla_fwd_kernel,
        grid=(n_b // rows, n_t),
        in_specs=specs(t_fwd, 0),
        out_specs=pl.BlockSpec((rows, TILE, 1024), lambda b, s: (b, t_fwd(s), 0)),
        out_shape=jax.ShapeDtypeStruct((n_b, ltot, 1024), F32),
        scratch_shapes=[pltpu.VMEM((rows, GLA_H, GLA_DV, GLA_DK), F32)],
        compiler_params=_cparams(("arbitrary", "arbitrary")),
        name="gla_fwd",
    )(gq, gk, gv, la, tri_lo)
    tsb = lambda c: pl.BlockSpec((rows, TILE, c), lambda b, s: (b, t_bwd(s), 0))
    return pl.pallas_call(
        _gla_bwd_kernel,
        grid=(n_b // rows, n_t),
        in_specs=specs(t_bwd, 1) + [tsb(1024), tsb(1024), pl.BlockSpec((1, GLA_DV), lambda b, s: (0, 0))],
        out_specs=tsb(1024),
        out_shape=jax.ShapeDtypeStruct((n_b, ltot, 1024), BF),
        scratch_shapes=[pltpu.VMEM((rows, GLA_H, GLA_DV, GLA_DK), F32)],
        compiler_params=_cparams(("arbitrary", "arbitrary")),
        name="gla_bwd",
    )(gq, gk, gv, la, tri_up, o_f, gr, gnorm)


def _merge_kernel(x_ref, c_ref, ya_ref, yb_ref, yc_ref, g_ref, g1_ref, wb_ref, wo_ref, lg_ref, lb_ref, o_ref, *,
                  n_lat):
    zt = (_sigmoid(g_ref[0, 0, 0:1024, :].astype(F32)) * _dot_nt(wb_ref[0], ya_ref[0])
          + _sigmoid(g_ref[0, 0, 1024:2048, :].astype(F32)) * _dot(wb_ref[1], yb_ref[0, 0])
          + _sigmoid(g_ref[0, 0, 2048:3072, :].astype(F32)) * _dot(wb_ref[2], yc_ref[0, 0]))
    u = _dot(wo_ref[...], zt.astype(BF)).T
    y = _ln_rows(DN_ALPHA * _x_tile(x_ref, c_ref, n_lat) + g1_ref[0] * u)
    o_ref[0] = y * lg_ref[...] + lb_ref[...]


def _merge(xs, ya, ybT, ycT, gatesT, mod_l, p, n_b, n_t, n_q):
    x_lat, x_ctx, ctx_blk = xs
    rows = _batch_rows(n_b)
    tile_spec = lambda c: pl.BlockSpec((rows, TILE, c), lambda b, t: (b, t, 0))
    fm_spec = lambda c: pl.BlockSpec((rows, 1, c, TILE), lambda b, t: (b, t, 0, 0))
    mrow = lambda b, t: jnp.where(t == n_t - 1, n_b // rows, b)
    return pl.pallas_call(
        functools.partial(_per_batch_row(_merge_kernel, 7, 4), n_lat=n_t - 1),
        grid=(n_b // rows, n_q),
        in_specs=_x_specs(rows, n_t - 1, ctx_blk) + [
                  tile_spec(1024), fm_spec(1024), fm_spec(1024), fm_spec(3072),
                  pl.BlockSpec((rows, 1, D_MODEL), lambda b, t: (mrow(b, t), 0, 2)),
                  _resident((3, D_MODEL, D_MODEL)), _resident((D_MODEL, D_MODEL)),
                  _resident((1, D_MODEL)), _resident((1, D_MODEL))],
        out_specs=tile_spec(D_MODEL),
        out_shape=jax.ShapeDtypeStruct((n_b, n_q * TILE, D_MODEL), F32),
        compiler_params=_cparams(("arbitrary", "arbitrary")),
        name="merge",
    )(x_lat, x_ctx, ya, ybT, ycT, gatesT, mod_l, p["wbT"], p["woT"], p["ln1_g"], p["ln1_b"])


def _ffn_kernel(x_ref, sh_ref, sc_ref, g2_ref, wi_ref, wo_ref, lg_ref, lb_ref, o_ref):
    x = x_ref[0]
    h = (_ln_rows(x) * (1.0 + sc_ref[0]) + sh_ref[0]).astype(BF)
    acc = jnp.zeros((TILE, D_MODEL), F32)
    for a, b in FFN_CHUNKS:
        gate = _dot(h, wi_ref[:, a:b])
        up = _dot(h, wi_ref[:, FFN_H + a:FFN_H + b])
        acc = acc + _dot((_silu(gate) * up).astype(BF), wo_ref[a:b, :])
    y = _ln_rows(DN_ALPHA * x + g2_ref[0] * acc)
    o_ref[0] = y * lg_ref[...] + lb_ref[...]


def _ffn(x1, mod_l, p, n_b, n_t, n_q):
    rows = _batch_rows(n_b)
    tile_spec = lambda c: pl.BlockSpec((rows, TILE, c), lambda b, t: (b, t, 0))
    mrow = lambda b, t: jnp.where(t == n_t - 1, n_b // rows, b)
    mspec = lambda j: pl.BlockSpec((rows, 1, D_MODEL), lambda b, t: (mrow(b, t), 0, j))
    return pl.pallas_call(
        _per_batch_row(_ffn_kernel, 4, 4),
        grid=(n_b // rows, n_q),
        in_specs=[tile_spec(D_MODEL), mspec(3), mspec(4), mspec(5),
                  _resident((D_MODEL, 2 * FFN_H)), _resident((FFN_H, D_MODEL)),
                  _resident((1, D_MODEL)), _resident((1, D_MODEL))],
        out_specs=tile_spec(D_MODEL),
        out_shape=jax.ShapeDtypeStruct((n_b, n_q * TILE, D_MODEL), F32),
        compiler_params=_cparams(("arbitrary", "arbitrary")),
        name="ffn",
    )(x1, mod_l, mod_l, mod_l, p["ffn_wi"], p["ffn_wo"], p["ln2_g"], p["ln2_b"])


def _prep_layer(l, w_in, gla_w_a2, gla_b_a, mla_q_norm_g, mla_kv_norm_g, mla_w_uq, mla_w_ukv, w_branch, w_out,
                ln1_g, ln1_b, ffn_w_in, ffn_w_out, ln2_g, ln2_b):
    w = w_in[l]
    seg = lambda a, b: w[:, a:b]
    w_tok = jnp.concatenate([
        seg(O_GQ, O_GK), seg(O_GK, O_GV), seg(O_GV, O_GR), seg(O_GR, O_GA), seg(O_DK, O_DV),
        seg(O_MQ, O_MKV), seg(O_MKV, O_MKR),
        seg(O_MKR, O_GATES), seg(O_GA, O_DQ), jnp.zeros((D_MODEL, 32), F32)], axis=1).astype(BF)
    w_featT = jnp.concatenate([seg(O_DQ, O_DK) * DIFF_DH ** -0.5, seg(O_DV, O_MQ), seg(O_GATES, O_END)],
                              axis=1).T.astype(BF)
    wa2 = jnp.zeros((128, 1024), F32)
    wa2 = wa2.at[64:80, 0:512].set(gla_w_a2[l, 0]).at[80:96, 512:1024].set(gla_w_a2[l, 1]).astype(BF)
    ukv = mla_w_ukv[l].reshape(MLA_KVR, MLA_H, MLA_NOPE + MLA_DV)
    return dict(
        w_tok=w_tok, w_featT=w_featT, wa2=wa2, ba=gla_b_a[l].reshape(1, 1024),
        qg=mla_q_norm_g[l].reshape(1, MLA_QR), kvg=mla_kv_norm_g[l].reshape(1, MLA_KVR),
        wukv_k=ukv[:, :, :MLA_NOPE].reshape(MLA_KVR, 1024).astype(BF),
        wukv_vT=ukv[:, :, MLA_NOPE:].reshape(MLA_KVR, 1024).T.astype(BF),
        wuqT=mla_w_uq[l].T.astype(BF),
        wbT=jnp.swapaxes(w_branch[l], 1, 2).astype(BF), woT=w_out[l].T.astype(BF),
        ln1_g=ln1_g[l].reshape(1, D_MODEL), ln1_b=ln1_b[l].reshape(1, D_MODEL),
        ffn_wi=ffn_w_in[l].astype(BF), ffn_wo=ffn_w_out[l].astype(BF),
        ln2_g=ln2_g[l].reshape(1, D_MODEL), ln2_b=ln2_b[l].reshape(1, D_MODEL))


def _rope_tables(l_lat, l_ctx):
    rows = l_lat // GRID_W
    pos_row = jnp.broadcast_to(jnp.arange(rows, dtype=F32)[:, None], (rows, GRID_W)).reshape(l_lat)
    pos_col = jnp.broadcast_to(jnp.arange(GRID_W, dtype=F32)[None, :], (rows, GRID_W)).reshape(l_lat)
    d_axis = ROPE_DIM // 2
    inv = ROPE_BASE ** (-jnp.arange(0, d_axis, 2, dtype=F32) / d_axis)
    ang = jnp.concatenate([pos_row[:, None] * inv, pos_col[:, None] * inv], axis=-1)
    cos = jnp.concatenate([jnp.cos(ang), jnp.ones((l_ctx, 32), F32)], axis=0)
    sin = jnp.concatenate([jnp.sin(ang), jnp.zeros((l_ctx, 32), F32)], axis=0)
    ctok = jnp.tile(cos, (1, 4))
    stok = jnp.tile(jnp.concatenate([-sin, sin], axis=1), (1, 2))
    n_t = (l_lat + l_ctx) // TILE
    to_fm = lambda a: a.T.reshape(32, n_t, TILE).transpose(1, 0, 2)
    return ctok, stok, to_fm(cos), to_fm(sin)


def kernel(x, c, ctx, c_ctx, w_mod, b_mod, w_in, gla_w_a2, gla_b_a, gla_norm_g, diff_lam, diff_norm_g,
           mla_q_norm_g, mla_kv_norm_g, mla_w_uq, mla_w_ukv, w_branch, w_out, ln1_g, ln1_b, ffn_w_in, ffn_w_out,
           ln2_g, ln2_b):
    n_b, l_lat, _ = x.shape
    l_ctx = ctx.shape[1]
    assert l_ctx == TILE and n_b + TOK_BATCH <= 16
    assert l_lat % (TILE * max(DIFF_QS, MLA_QS)) == 0 and l_lat % (TILE * KEY_TILES * KEY_UNROLL) == 0
    ltot = l_lat + l_ctx
    n_t = ltot // TILE
    ctok, stok, cosT, sinT = _rope_tables(l_lat, l_ctx)
    c_all = jnp.zeros((16, D_MODEL), F32).at[:n_b].set(c).at[n_b:n_b + TOK_BATCH].set(c_ctx)
    mod = _modulation(c_all, w_mod, b_mod)
    ii = lax.broadcasted_iota(jnp.int32, (TILE, TILE), 0)
    jj = lax.broadcasted_iota(jnp.int32, (TILE, TILE), 1)
    same = (ii // GLA_CHUNK) == (jj // GLA_CHUNK)
    tri_lo = (same & (jj <= ii)).astype(BF)
    tri_up = (same & (jj >= ii)).astype(BF)
    xs = (x, ctx, 0)
    for l in range(N_LAYERS):
        last = l == N_LAYERS - 1
        n_q = n_t - 1 if last else n_t
        lam_init = 0.8 - 0.6 * math.exp(-0.3 * l)
        p = _prep_layer(l, w_in, gla_w_a2, gla_b_a, mla_q_norm_g, mla_kv_norm_g, mla_w_uq, mla_w_ukv, w_branch,
                        w_out, ln1_g, ln1_b, ffn_w_in, ffn_w_out, ln2_g, ln2_b)
        mod_l = mod[l].reshape(16, 1, 6 * D_MODEL)
        gq, gk, gv, gr, dk, la, mk, cq, ckv = _proj_tok(xs, mod_l, ctok, stok, p, n_b, n_t, ltot)
        dqT, dvT, gatesT, mqT, mvT = _proj_feat(xs, mod_l, cosT, sinT, cq, ckv, p, n_b, n_t)
        ya = _gla(gq, gk, gv, la, gr, tri_lo, tri_up, gla_norm_g[l].reshape(1, GLA_DV), n_b, n_t)
        gcol = jnp.broadcast_to(diff_norm_g[l].reshape(128, 1), (128, TILE))
        ybT = _diff_attn(dqT, dk, dvT, diff_lam[l], gcol, n_b, n_t, lam_init)
        ycT = _mla_attn(mqT, mk, mvT, n_b, n_t)
        if not last:
            ybT = _diff_attn(dqT, dk, dvT, diff_lam[l], gcol, n_b, n_t, lam_init, ctx_into=ybT)
            ycT = _mla_attn(mqT, mk, mvT, n_b, n_t, ctx_into=ycT)
        x1 = _merge(xs, ya, ybT, ycT, gatesT, mod_l, p, n_b, n_t, n_q)
        x2 = _ffn(x1, mod_l, p, n_b, n_t, n_q)
        xs = (x2, x2, n_t - 1)
    return x2
```

```python
import functools
import math

import jax
import jax.numpy as jnp
from jax import lax
from jax.experimental import pallas as pl
from jax.experimental.pallas import tpu as pltpu

BF = jnp.bfloat16
F32 = jnp.float32

D_MODEL = 1024
N_LAYERS = 2
GRID_W = 64
TILE = 256
GLA_H, GLA_DK, GLA_DV, GLA_RANK, GLA_TAU = 4, 128, 256, 16, 16.0
GLA_CHUNK, GLA_SUB = 64, 16
GLA_BATCH = 2
TOK_BATCH = 2
DIFF_H, DIFF_DH = 8, 64
MLA_H, MLA_QR, MLA_KVR, MLA_NOPE, MLA_ROPE, MLA_DV = 8, 256, 128, 128, 64, 128
MLA_SCALE = (MLA_NOPE + MLA_ROPE) ** -0.5
ROPE_DIM, ROPE_BASE = 64, 10000.0
FFN_H = 2816
FFN_CHUNKS = ((0, 1536), (1536, 2816))
DN_ALPHA = (2 * N_LAYERS) ** 0.25
EPS = 1e-6
DIFF_QS, MLA_QS = 2, 4
KEY_TILES = 2
KEY_UNROLL = 4
LOG2E = math.log2(math.e)
ONES_ROWS = 16
VMEM_LIMIT = 56 * 1024 * 1024

_SIZES = (512, 512, 1024, 1024, 32, 1024, 1024, 1024, 256, 128, 64, 3072)
_OFF = [0]
for _s in _SIZES:
    _OFF.append(_OFF[-1] + _s)
(O_GQ, O_GK, O_GV, O_GR, O_GA, O_DQ, O_DK, O_DV, O_MQ, O_MKV, O_MKR, O_GATES, O_END) = _OFF

T_GQ, T_GK, T_GV, T_GR, T_DK, T_MQ, T_MKV, T_SMA, T_END = (0, 512, 1024, 2048, 3072, 4096, 4352, 4480, 4608)
F_DQ, F_DV, F_GATES, F_END = 0, 1024, 2048, 5120


def _cparams(sem):
    return pltpu.CompilerParams(dimension_semantics=sem, vmem_limit_bytes=VMEM_LIMIT)


def _resident(shape):
    nd = len(shape)
    return pl.BlockSpec(shape, lambda *_: (0,) * nd, pipeline_mode=pl.Buffered(1))


def _dot(a, b):
    return jnp.dot(a, b, preferred_element_type=F32)


def _dot_nt(a, b):
    return lax.dot_general(a, b, (((1,), (1,)), ((), ())), preferred_element_type=F32)


def _dot_tn(a, b):
    return lax.dot_general(a, b, (((0,), (0,)), ((), ())), preferred_element_type=F32)


def _ln_rows(x):
    mu = jnp.mean(x, axis=-1, keepdims=True)
    xc = x - mu
    var = jnp.mean(xc * xc, axis=-1, keepdims=True)
    return xc * lax.rsqrt(var + EPS)


def _sigmoid(x):
    return 1.0 / (1.0 + jnp.exp(-x))


def _silu(x):
    return x * _sigmoid(x)


def _mod_kernel(c_ref, w_ref, b_ref, o_ref):
    ca = _silu(c_ref[...]).astype(BF)
    o_ref[0] = _dot(ca, w_ref[0].astype(BF)) + b_ref[0]


def _modulation(c_all, w_mod, b_mod):
    nl, d, n6 = w_mod.shape
    r = c_all.shape[0]
    nblk = 1536
    return pl.pallas_call(
        _mod_kernel,
        grid=(nl, n6 // nblk),
        in_specs=[pl.BlockSpec((r, d), lambda l, j: (0, 0)),
                  pl.BlockSpec((1, d, nblk), lambda l, j: (l, 0, j)),
                  pl.BlockSpec((1, 1, nblk), lambda l, j: (l, 0, j))],
        out_specs=pl.BlockSpec((1, r, nblk), lambda l, j: (l, 0, j)),
        out_shape=jax.ShapeDtypeStruct((nl, r, n6), F32),
        compiler_params=_cparams(("arbitrary", "arbitrary")),
        name="modulation",
    )(c_all, w_mod, b_mod.reshape(nl, 1, n6))


def _batch_rows(n_b):
    return TOK_BATCH if n_b % TOK_BATCH == 0 else 1


_DONE = object()


def _per_batch_row(body, n_rowed, n_shared):
    def kern(*refs, **kw):
        rowed, shared, outs = refs[:n_rowed], refs[n_rowed:n_rowed + n_shared], refs[n_rowed + n_shared:]
        pending = []
        for bi in range(rowed[0].shape[0]):
            row = lambda ref, bi=bi: ref.at[pl.ds(bi, 1)]
            pending.append(body(*map(row, rowed), *shared, *map(row, outs), **kw))
        pending = [g for g in pending if g is not None]
        while pending:
            pending = [g for g in pending if next(g, _DONE) is not _DONE]
    return kern


def _x_specs(rows, n_lat, ctx_blk):
    return [pl.BlockSpec((rows, TILE, D_MODEL), lambda b, t: (b, jnp.minimum(t, n_lat - 1), 0)),
            pl.BlockSpec((rows, TILE, D_MODEL), lambda b, t: (b, ctx_blk, 0))]


def _x_tile(x_ref, c_ref, n_lat):
    return jnp.where(pl.program_id(1) == n_lat, c_ref[0], x_ref[0])


def _proj_tok_kernel(x_ref, c_ref, sh_ref, sc_ref, ct_ref, st_ref, w_ref, wa2_ref, ba_ref, qg_ref, kvg_ref, wk_ref,
                     gq_ref, gk_ref, gv_ref, gr_ref, dk_ref, la_ref, mk_ref, cq_ref, ckv_ref, *, n_lat):
    h = (_ln_rows(_x_tile(x_ref, c_ref, n_lat)) * (1.0 + sc_ref[0]) + sh_ref[0]).astype(BF)

    def proj(a, b):
        return _dot(h, w_ref[:, a:b])

    ct = ct_ref[...]
    st = st_ref[...]
    first_half = lax.broadcasted_iota(jnp.int32, (TILE, 128), 1) % ROPE_DIM < ROPE_DIM // 2

    def rope(x):
        partner = jnp.where(first_half, pltpu.roll(x, 128 - ROPE_DIM // 2, 1), pltpu.roll(x, ROPE_DIM // 2, 1))
        return x * ct + partner * st

    mm = proj(T_MKV, T_END)
    mkv = mm[:, 0:128]
    sma = mm[:, 128:256]
    ckv = (mkv * lax.rsqrt(jnp.mean(mkv * mkv, axis=-1, keepdims=True) + EPS) * kvg_ref[...]).astype(BF)
    ckv_ref[0] = ckv
    mq = proj(T_MQ, T_MKV)
    cq = mq * lax.rsqrt(jnp.mean(mq * mq, axis=-1, keepdims=True) + EPS) * qg_ref[...]
    cq_ref[0] = cq.astype(BF)
    gq_ref[0] = (proj(T_GQ, T_GK) * GLA_DK ** -0.5).astype(BF)
    gk_ref[0] = proj(T_GK, T_GV).astype(BF)
    gv_ref[0] = proj(T_GV, T_GR).astype(BF)
    gr_ref[0] = proj(T_GR, T_DK).astype(BF)
    dkx = proj(T_DK, T_MQ)
    for hh in range(DIFF_H):
        sl = slice(128 * hh, 128 * (hh + 1))
        dk_ref[0, :, sl] = rope(dkx[:, sl]).astype(BF)
    yield
    kn = _dot(ckv, wk_ref[...]).astype(BF)
    kr = rope(sma)[:, 0:64].astype(BF)
    for hh in range(MLA_H):
        mk_ref[0, hh, :, 0:128] = kn[:, 128 * hh:128 * (hh + 1)]
        mk_ref[0, hh, :, 128:192] = kr
    z = _dot(sma.astype(BF), wa2_ref[...]) + ba_ref[...]
    la_ref[0] = (jnp.minimum(z, 0.0) - jnp.log(1.0 + jnp.exp(-jnp.abs(z)))) * (1.0 / GLA_TAU)


def _proj_tok(xs, mod_l, ctok, stok, p, n_b, n_t, ltot):
    x_lat, x_ctx, ctx_blk = xs
    rows = _batch_rows(n_b)
    tile_spec = lambda c: pl.BlockSpec((rows, TILE, c), lambda b, t: (b, t, 0))
    mrow = lambda b, t: jnp.where(t == n_t - 1, n_b // rows, b)
    bf_out = lambda c: jax.ShapeDtypeStruct((n_b, ltot, c), BF)
    return pl.pallas_call(
        functools.partial(_per_batch_row(_proj_tok_kernel, 4, 8), n_lat=n_t - 1),
        grid=(n_b // rows, n_t),
        in_specs=_x_specs(rows, n_t - 1, ctx_blk) + [
                  pl.BlockSpec((rows, 1, D_MODEL), lambda b, t: (mrow(b, t), 0, 0)),
                  pl.BlockSpec((rows, 1, D_MODEL), lambda b, t: (mrow(b, t), 0, 1)),
                  pl.BlockSpec((TILE, 128), lambda b, t: (t, 0)),
                  pl.BlockSpec((TILE, 128), lambda b, t: (t, 0)),
                  _resident((D_MODEL, T_END)), _resident((128, 1024)), _resident((1, 1024)),
                  _resident((1, MLA_QR)), _resident((1, MLA_KVR)), _resident((MLA_KVR, 1024))],
        out_specs=[tile_spec(512), tile_spec(512), tile_spec(1024), tile_spec(1024), tile_spec(1024),
                   tile_spec(1024),
                   pl.BlockSpec((rows, MLA_H, TILE, 192), lambda b, t: (b, 0, t, 0)),
                   tile_spec(MLA_QR), tile_spec(MLA_KVR)],
        out_shape=[bf_out(512), bf_out(512), bf_out(1024), bf_out(1024), bf_out(1024),
                   jax.ShapeDtypeStruct((n_b, ltot, 1024), F32),
                   jax.ShapeDtypeStruct((n_b, MLA_H, ltot, 192), BF),
                   bf_out(MLA_QR), bf_out(MLA_KVR)],
        compiler_params=_cparams(("arbitrary", "arbitrary")),
        name="proj_tok",
    )(x_lat, x_ctx, mod_l, mod_l, ctok, stok, p["w_tok"], p["wa2"], p["ba"], p["qg"], p["kvg"], p["wukv_k"])


def _rope_rows(x1, x2, cos, sin):
    return x1 * cos - x2 * sin, x1 * sin + x2 * cos


def _proj_feat_kernel(x_ref, c_ref, sh_ref, sc_ref, cq_ref, ckv_ref, cos_ref, sin_ref, w_ref, wuq_ref, wv_ref,
                      dq_ref, dv_ref, g_ref, mq_ref, mv_ref, *, n_lat):
    h = (_ln_rows(_x_tile(x_ref, c_ref, n_lat)) * (1.0 + sc_ref[0]) + sh_ref[0]).astype(BF)
    cos = cos_ref[0]
    sin = sin_ref[0]
    dq = _dot_nt(w_ref[F_DQ:F_DV, :], h) * LOG2E
    for g in range(2 * DIFF_H):
        o1, o2 = _rope_rows(dq[64 * g:64 * g + 32], dq[64 * g + 32:64 * (g + 1)], cos, sin)
        dq_ref[0, 0, 64 * g:64 * g + 32, :] = o1.astype(BF)
        dq_ref[0, 0, 64 * g + 32:64 * (g + 1), :] = o2.astype(BF)
    dv_ref[0, 0] = _dot_nt(w_ref[F_DV:F_GATES, :], h).astype(BF)
    for j in range(3):
        a = F_GATES + 1024 * j
        g_ref[0, 0, 1024 * j:1024 * (j + 1), :] = _dot_nt(w_ref[a:a + 1024, :], h).astype(BF)
    cq = cq_ref[0]
    mq = _dot_nt(wuq_ref[...], cq) * (MLA_SCALE * LOG2E)
    for hh in range(MLA_H):
        a = 192 * hh
        o1, o2 = _rope_rows(mq[a + 128:a + 160], mq[a + 160:a + 192], cos, sin)
        mq_ref[0, 0, a:a + 128, :] = mq[a:a + 128].astype(BF)
        mq_ref[0, 0, a + 128:a + 160, :] = o1.astype(BF)
        mq_ref[0, 0, a + 160:a + 192, :] = o2.astype(BF)
    mv_ref[0, 0] = _dot_nt(wv_ref[...], ckv_ref[0]).astype(BF)


def _proj_feat(xs, mod_l, cosT, sinT, cq, ckv, p, n_b, n_t):
    x_lat, x_ctx, ctx_blk = xs
    rows = _batch_rows(n_b)
    tile_spec = lambda c: pl.BlockSpec((rows, TILE, c), lambda b, t: (b, t, 0))
    mrow = lambda b, t: jnp.where(t == n_t - 1, n_b // rows, b)
    fm_spec = lambda c: pl.BlockSpec((rows, 1, c, TILE), lambda b, t: (b, t, 0, 0))
    fm_out = lambda c: jax.ShapeDtypeStruct((n_b, n_t, c, TILE), BF)
    return pl.pallas_call(
        functools.partial(_per_batch_row(_proj_feat_kernel, 6, 5), n_lat=n_t - 1),
        grid=(n_b // rows, n_t),
        in_specs=_x_specs(rows, n_t - 1, ctx_blk) + [
                  pl.BlockSpec((rows, 1, D_MODEL), lambda b, t: (mrow(b, t), 0, 0)),
                  pl.BlockSpec((rows, 1, D_MODEL), lambda b, t: (mrow(b, t), 0, 1)),
                  tile_spec(MLA_QR), tile_spec(MLA_KVR),
                  pl.BlockSpec((1, 32, TILE), lambda b, t: (t, 0, 0)),
                  pl.BlockSpec((1, 32, TILE), lambda b, t: (t, 0, 0)),
                  _resident((F_END, D_MODEL)), _resident((MLA_H * 192, MLA_QR)), _resident((1024, MLA_KVR))],
        out_specs=[fm_spec(1024), fm_spec(1024), fm_spec(3072), fm_spec(MLA_H * 192), fm_spec(1024)],
        out_shape=[fm_out(1024), fm_out(1024), fm_out(3072), fm_out(MLA_H * 192), fm_out(1024)],
        compiler_params=_cparams(("arbitrary", "arbitrary")),
        name="proj_feat",
    )(x_lat, x_ctx, mod_l, mod_l, cq, ckv, cosT, sinT, p["w_featT"], p["wuqT"], p["wukv_vT"])


def _flash_T(k_ref, v_ref, q2, m_ref, acc_ref, s_ref, n_lat, latent_keys):
    dv = v_ref.shape[2]
    m_ref[2] = jnp.full(m_ref.shape[1:], -jnp.inf, F32)
    acc_ref[...] = jnp.zeros(acc_ref.shape, F32)
    rows = KEY_TILES * TILE

    def scores(slot, tile0):
        s = _dot(k_ref[0, pl.ds(pl.multiple_of(tile0 * TILE, TILE), rows), :], q2)
        s_ref[slot] = s
        m_ref[slot] = jnp.max(s, axis=0, keepdims=True)

    def consume(s, s_max, tile0, ntiles):
        m_old = m_ref[2]
        m_new = jnp.maximum(m_old, s_max)
        alpha = jnp.exp2(m_old - m_new)
        p = jnp.exp2(s - m_new).astype(BF)
        v = jnp.concatenate([v_ref[0, tile0 + i] for i in range(ntiles)], axis=1)
        vext = jnp.concatenate([v, jnp.ones((ONES_ROWS, ntiles * TILE), BF)], axis=0)
        acc_ref[...] = acc_ref[...] * alpha + _dot(vext, p)
        m_ref[2] = m_new

    def ctx_step(tile):
        s = _dot(k_ref[0, pl.ds(tile * TILE, TILE), :], q2)
        return s, jnp.max(s, axis=0, keepdims=True)

    if latent_keys:
        n_steps = n_lat // KEY_TILES
        scores(0, 0)

        def body(jj, carry):
            for u in range(KEY_UNROLL):
                j = KEY_UNROLL * jj + u
                scores((u + 1) % 2, (j + 1) * KEY_TILES)
                consume(s_ref[u % 2], m_ref[u % 2], j * KEY_TILES, KEY_TILES)
            return carry
        lax.fori_loop(0, n_steps // KEY_UNROLL - 1, body, 0)
        for u in range(KEY_UNROLL):
            j = n_steps - KEY_UNROLL + u
            if u < KEY_UNROLL - 1:
                scores((u + 1) % 2, (j + 1) * KEY_TILES)
            else:
                s_ctx, mx_ctx = ctx_step(n_lat)
            consume(s_ref[u % 2], m_ref[u % 2], j * KEY_TILES, KEY_TILES)
        consume(s_ctx, mx_ctx, n_lat, 1)
    else:
        consume(*ctx_step(0), 0, 1)
    acc = acc_ref[...]
    return acc[0:dv] * (1.0 / acc[dv:dv + 1])


def _attn_call(kern, name, q_rows, k_spec_fn, n_b, n_heads, n_t, qs, scratch, qT, k, vT, extra, extra_specs,
               ctx_into=None):
    n_lat = n_t - 1
    if ctx_into is None:
        grid = (n_b, n_heads, n_lat // qs)
        q_spec = pl.BlockSpec((1, qs, q_rows, TILE), lambda b, h, q: (b, q, h, 0))
        k_spec = k_spec_fn(n_t * TILE, 0)
        v_spec = pl.BlockSpec((1, n_t, 128, TILE), lambda b, h, q: (b, 0, h, 0))
        o_spec = pl.BlockSpec((1, qs, 128, TILE), lambda b, h, q: (b, q, h, 0))
        alias_in, alias_specs, aliases = [], [], {}
    else:
        grid = (n_b, n_heads, 1)
        q_spec = pl.BlockSpec((1, 1, q_rows, TILE), lambda b, h, q: (b, n_lat, h, 0))
        k_spec = k_spec_fn(TILE, n_lat)
        v_spec = pl.BlockSpec((1, 1, 128, TILE), lambda b, h, q: (b, n_lat, h, 0))
        o_spec = pl.BlockSpec((1, 1, 128, TILE), lambda b, h, q: (b, n_lat, h, 0))
        alias_in, alias_specs = [ctx_into], [pl.BlockSpec(memory_space=pl.ANY)]
        aliases = {3 + len(extra): 0}
    return pl.pallas_call(
        kern,
        grid=grid,
        in_specs=[q_spec, k_spec, v_spec] + extra_specs + alias_specs,
        out_specs=o_spec,
        out_shape=jax.ShapeDtypeStruct((n_b, n_t, 1024, TILE), BF),
        scratch_shapes=scratch,
        input_output_aliases=aliases,
        compiler_params=_cparams(("arbitrary", "arbitrary", "arbitrary")),
        name=name,
    )(qT, k, vT, *extra, *alias_in)


def _diff_attn_kernel(q_ref, k_ref, v_ref, lam_ref, g_ref, *rest, n_lat, lam_init, qs, latent):
    o_ref, q2_ref, m_ref, acc_ref, s_ref = rest[-5:]
    nq = qs * TILE
    zeros = jnp.zeros((DIFF_DH, TILE), BF)
    for i in range(qs):
        q = q_ref[0, i]
        q2_ref[0:64, i * TILE:(i + 1) * TILE] = q[0:64]
        q2_ref[64:128, i * TILE:(i + 1) * TILE] = zeros
        q2_ref[0:64, nq + i * TILE:nq + (i + 1) * TILE] = zeros
        q2_ref[64:128, nq + i * TILE:nq + (i + 1) * TILE] = q[64:128]
    o = _flash_T(k_ref, v_ref, q2_ref[...], m_ref, acc_ref, s_ref, n_lat, latent)
    dl = lam_ref[...]
    lam = (jnp.exp(jnp.sum(dl[0:1] * dl[1:2], axis=1, keepdims=True))
           - jnp.exp(jnp.sum(dl[2:3] * dl[3:4], axis=1, keepdims=True)) + lam_init)
    od = o[:, 0:nq] - lam * o[:, nq:2 * nq]
    y = od * lax.rsqrt(jnp.mean(od * od, axis=0, keepdims=True) + EPS) * (1.0 - lam_init)
    for i in range(qs):
        o_ref[0, i] = (y[:, i * TILE:(i + 1) * TILE] * g_ref[...]).astype(BF)


def _diff_attn(dqT, dk, dvT, dlam, gcol, n_b, n_t, lam_init, ctx_into=None):
    qs = DIFF_QS if ctx_into is None else 1
    kern = functools.partial(_diff_attn_kernel, n_lat=n_t - 1, lam_init=lam_init, qs=qs, latent=ctx_into is None)
    k_spec_fn = lambda rows, blk: pl.BlockSpec((1, rows, 128), lambda b, h, q: (b, blk, h))
    scratch = [pltpu.VMEM((128, 2 * qs * TILE), BF), pltpu.VMEM((3, 1, 2 * qs * TILE), F32),
               pltpu.VMEM((128 + ONES_ROWS, 2 * qs * TILE), F32),
               pltpu.VMEM((2, KEY_TILES * TILE, 2 * qs * TILE), F32)]
    extra_specs = [pl.BlockSpec((4, DIFF_DH), lambda b, h, q: (0, 0)),
                   pl.BlockSpec((128, TILE), lambda b, h, q: (0, 0))]
    return _attn_call(kern, "diff_attn" if ctx_into is None else "diff_attn_ctx", 128, k_spec_fn, n_b, DIFF_H, n_t,
                      qs, scratch, dqT, dk, dvT, [dlam, gcol], extra_specs, ctx_into)


def _mla_attn_kernel(q_ref, k_ref, v_ref, *rest, n_lat, qs, latent):
    o_ref, q2_ref, m_ref, acc_ref, s_ref = rest[-5:]
    for i in range(qs):
        q2_ref[:, i * TILE:(i + 1) * TILE] = q_ref[0, i]
    o = _flash_T(k_ref.at[0], v_ref, q2_ref[...], m_ref, acc_ref, s_ref, n_lat, latent)
    for i in range(qs):
        o_ref[0, i] = o[:, i * TILE:(i + 1) * TILE].astype(BF)


def _mla_attn(mqT, mk, mvT, n_b, n_t, ctx_into=None):
    qs = MLA_QS if ctx_into is None else 1
    kern = functools.partial(_mla_attn_kernel, n_lat=n_t - 1, qs=qs, latent=ctx_into is None)
    k_spec_fn = lambda rows, blk: pl.BlockSpec((1, 1, rows, 192), lambda b, h, q: (b, h, blk, 0))
    scratch = [pltpu.VMEM((192, qs * TILE), BF), pltpu.VMEM((3, 1, qs * TILE), F32),
               pltpu.VMEM((128 + ONES_ROWS, qs * TILE), F32),
               pltpu.VMEM((2, KEY_TILES * TILE, qs * TILE), F32)]
    return _attn_call(kern, "mla_attn" if ctx_into is None else "mla_attn_ctx", 192, k_spec_fn, n_b, MLA_H, n_t,
                      qs, scratch, mqT, mk, mvT, [], [], ctx_into)


def _gla_tile(q_ref, k_ref, v_ref, la_ref, tri_ref, st_ref, bi, reverse):
    g = la_ref[bi]
    g1 = g.astype(BF)
    r1 = g - g1.astype(F32)
    g2 = r1.astype(BF)
    tri = tri_ref[...]
    bcum = _dot(tri, g1) + _dot(tri, g2)
    n_chunk = TILE // GLA_CHUNK
    n_sub = GLA_CHUNK // GLA_SUB
    rows64 = lax.broadcasted_iota(jnp.int32, (GLA_CHUNK, GLA_DK), 0)
    r16 = lax.broadcasted_iota(jnp.int32, (GLA_SUB, GLA_CHUNK), 0)
    c16 = lax.broadcasted_iota(jnp.int32, (GLA_SUB, GLA_CHUNK), 1)
    order = range(n_chunk - 1, -1, -1) if reverse else range(n_chunk)
    work = {}
    for hh in range(GLA_H):
        kc0, kc1 = GLA_DK * hh, GLA_DK * (hh + 1)
        for c in order:
            r0 = GLA_CHUNK * c
            bc = bcum[r0:r0 + GLA_CHUNK, kc0:kc1]
            qc = q_ref[bi, r0:r0 + GLA_CHUNK, kc0:kc1].astype(F32)
            kc = k_ref[bi, r0:r0 + GLA_CHUNK, kc0:kc1].astype(F32)
            vc = v_ref[bi, r0:r0 + GLA_CHUNK, GLA_DV * hh:GLA_DV * (hh + 1)]
            b_tot = bc[0:1] if reverse else bc[GLA_CHUNK - 1:GLA_CHUNK]
            a_rows = []
            for i in range(n_sub):
                s0 = GLA_SUB * i
                ref_row = bc[s0 + GLA_SUB - 1:s0 + GLA_SUB] if reverse else bc[s0:s0 + 1]
                qs = (qc[s0:s0 + GLA_SUB] * jnp.exp(bc[s0:s0 + GLA_SUB] - ref_row)).astype(BF)
                valid = (rows64 >= s0) if reverse else (rows64 < s0 + GLA_SUB)
                ks = (kc * jnp.exp(jnp.where(valid, ref_row - bc, -jnp.inf))).astype(BF)
                a = _dot_nt(qs, ks)
                keep = (c16 > r16 + s0) if reverse else (c16 <= r16 + s0)
                a_rows.append(jnp.where(keep, a, 0.0))
            q_in = (qc * jnp.exp(bc)).astype(BF)
            k_d = (kc * jnp.exp(b_tot - bc)).astype(BF)
            work[hh, c] = (a_rows, q_in, vc, jnp.exp(b_tot), _dot_tn(vc, k_d))
    return work, order


def _gla_finish(work, order, st_ref, bi):
    outs = []
    for hh in range(GLA_H):
        o_chunks = [None] * (TILE // GLA_CHUNK)
        st = st_ref[bi, hh]
        for c in order:
            a_rows, q_in, vc, decay, st_inc = work[hh, c]
            a_mat = jnp.concatenate(a_rows, axis=0).astype(BF)
            o_chunks[c] = _dot(a_mat, vc) + _dot_nt(q_in, st.astype(BF))
            st = st * decay + st_inc
        st_ref[bi, hh] = st
        outs.append(jnp.concatenate(o_chunks, axis=0))
    return jnp.concatenate(outs, axis=1)


def _gla_rows(q_ref, k_ref, v_ref, la_ref, tri_ref, st_ref, reverse):
    staged = [_gla_tile(q_ref, k_ref, v_ref, la_ref, tri_ref, st_ref, bi, reverse) for bi in range(q_ref.shape[0])]
    return [_gla_finish(work, order, st_ref, bi) for bi, (work, order) in enumerate(staged)]


def _gla_fwd_kernel(q_ref, k_ref, v_ref, la_ref, tri_ref, o_ref, st_ref):
    @pl.when(pl.program_id(1) == 0)
    def _():
        st_ref[...] = jnp.zeros(st_ref.shape, F32)
    for bi, o in enumerate(_gla_rows(q_ref, k_ref, v_ref, la_ref, tri_ref, st_ref, False)):
        o_ref[bi] = o


def _gla_bwd_kernel(q_ref, k_ref, v_ref, la_ref, tri_ref, of_ref, r_ref, g_ref, y_ref, st_ref):
    @pl.when(pl.program_id(1) == 0)
    def _():
        st_ref[...] = jnp.zeros(st_ref.shape, F32)
    gn = g_ref[...]
    for bi, o_b in enumerate(_gla_rows(q_ref, k_ref, v_ref, la_ref, tri_ref, st_ref, True)):
        o = of_ref[bi] + o_b
        for hh in range(GLA_H):
            sl = slice(GLA_DV * hh, GLA_DV * (hh + 1))
            oh = o[:, sl]
            yh = oh * lax.rsqrt(jnp.mean(oh * oh, axis=-1, keepdims=True) + EPS) * gn
            y_ref[bi, :, sl] = (yh * _silu(r_ref[bi, :, sl].astype(F32))).astype(BF)


def _gla(gq, gk, gv, la, gr, tri_lo, tri_up, gnorm, n_b, n_t):
    ltot = n_t * TILE
    rows = GLA_BATCH if n_b % GLA_BATCH == 0 else 1
    t_fwd = lambda s: (s + n_t - 1) % n_t
    t_bwd = lambda s: jnp.where(s == 0, n_t - 1, n_t - 1 - s)

    def specs(tmap, dirn):
        ts = lambda c, cb=0: pl.BlockSpec((rows, TILE, c), lambda b, s: (b, tmap(s), cb))
        return [ts(512), ts(512), ts(1024), ts(512, dirn), pl.BlockSpec((TILE, TILE), lambda b, s: (0, 0))]

    o_f = pl.pallas_call(
        _gla_fwd_kernel,
        grid=(n_b // rows, n_t),
        in_specs=specs(t_fwd, 0),
        out_specs=pl.BlockSpec((rows, TILE, 1024), lambda b, s: (b, t_fwd(s), 0)),
        out_shape=jax.ShapeDtypeStruct((n_b, ltot, 1024), F32),
        scratch_shapes=[pltpu.VMEM((rows, GLA_H, GLA_DV, GLA_DK), F32)],
        compiler_params=_cparams(("arbitrary", "arbitrary")),
        name="gla_fwd",
    )(gq, gk, gv, la, tri_lo)
    tsb = lambda c: pl.BlockSpec((rows, TILE, c), lambda b, s: (b, t_bwd(s), 0))
    return pl.pallas_call(
        _gla_bwd_kernel,
        grid=(n_b // rows, n_t),
        in_specs=specs(t_bwd, 1) + [tsb(1024), tsb(1024), pl.BlockSpec((1, GLA_DV), lambda b, s: (0, 0))],
        out_specs=tsb(1024),
        out_shape=jax.ShapeDtypeStruct((n_b, ltot, 1024), BF),
        scratch_shapes=[pltpu.VMEM((rows, GLA_H, GLA_DV, GLA_DK), F32)],
        compiler_params=_cparams(("arbitrary", "arbitrary")),
        name="gla_bwd",
    )(gq, gk, gv, la, tri_up, o_f, gr, gnorm)


def _merge_kernel(x_ref, c_ref, ya_ref, yb_ref, yc_ref, g_ref, g1_ref, wb_ref, wo_ref, lg_ref, lb_ref, o_ref, *,
                  n_lat):
    zt = (_sigmoid(g_ref[0, 0, 0:1024, :].astype(F32)) * _dot_nt(wb_ref[0], ya_ref[0])
          + _sigmoid(g_ref[0, 0, 1024:2048, :].astype(F32)) * _dot(wb_ref[1], yb_ref[0, 0])
          + _sigmoid(g_ref[0, 0, 2048:3072, :].astype(F32)) * _dot(wb_ref[2], yc_ref[0, 0]))
    yield
    u = _dot(wo_ref[...], zt.astype(BF)).T
    y = _ln_rows(DN_ALPHA * _x_tile(x_ref, c_ref, n_lat) + g1_ref[0] * u)
    o_ref[0] = y * lg_ref[...] + lb_ref[...]


def _merge(xs, ya, ybT, ycT, gatesT, mod_l, p, n_b, n_t, n_q):
    x_lat, x_ctx, ctx_blk = xs
    rows = _batch_rows(n_b)
    tile_spec = lambda c: pl.BlockSpec((rows, TILE, c), lambda b, t: (b, t, 0))
    fm_spec = lambda c: pl.BlockSpec((rows, 1, c, TILE), lambda b, t: (b, t, 0, 0))
    mrow = lambda b, t: jnp.where(t == n_t - 1, n_b // rows, b)
    return pl.pallas_call(
        functools.partial(_per_batch_row(_merge_kernel, 7, 4), n_lat=n_t - 1),
        grid=(n_b // rows, n_q),
        in_specs=_x_specs(rows, n_t - 1, ctx_blk) + [
                  tile_spec(1024), fm_spec(1024), fm_spec(1024), fm_spec(3072),
                  pl.BlockSpec((rows, 1, D_MODEL), lambda b, t: (mrow(b, t), 0, 2)),
                  _resident((3, D_MODEL, D_MODEL)), _resident((D_MODEL, D_MODEL)),
                  _resident((1, D_MODEL)), _resident((1, D_MODEL))],
        out_specs=tile_spec(D_MODEL),
        out_shape=jax.ShapeDtypeStruct((n_b, n_q * TILE, D_MODEL), F32),
        compiler_params=_cparams(("arbitrary", "arbitrary")),
        name="merge",
    )(x_lat, x_ctx, ya, ybT, ycT, gatesT, mod_l, p["wbT"], p["woT"], p["ln1_g"], p["ln1_b"])


def _ffn_kernel(x_ref, sh_ref, sc_ref, g2_ref, wi_ref, wo_ref, lg_ref, lb_ref, o_ref):
    x = x_ref[0]
    h = (_ln_rows(x) * (1.0 + sc_ref[0]) + sh_ref[0]).astype(BF)
    acts = []
    for a, b in FFN_CHUNKS:
        gate = _dot(h, wi_ref[:, a:b])
        up = _dot(h, wi_ref[:, FFN_H + a:FFN_H + b])
        acts.append((_silu(gate) * up).astype(BF))
    yield
    acc = jnp.zeros((TILE, D_MODEL), F32)
    for (a, b), act in zip(FFN_CHUNKS, acts):
        acc = acc + _dot(act, wo_ref[a:b, :])
    y = _ln_rows(DN_ALPHA * x + g2_ref[0] * acc)
    o_ref[0] = y * lg_ref[...] + lb_ref[...]


def _ffn(x1, mod_l, p, n_b, n_t, n_q):
    rows = _batch_rows(n_b)
    tile_spec = lambda c: pl.BlockSpec((rows, TILE, c), lambda b, t: (b, t, 0))
    mrow = lambda b, t: jnp.where(t == n_t - 1, n_b // rows, b)
    mspec = lambda j: pl.BlockSpec((rows, 1, D_MODEL), lambda b, t: (mrow(b, t), 0, j))
    return pl.pallas_call(
        _per_batch_row(_ffn_kernel, 4, 4),
        grid=(n_b // rows, n_q),
        in_specs=[tile_spec(D_MODEL), mspec(3), mspec(4), mspec(5),
                  _resident((D_MODEL, 2 * FFN_H)), _resident((FFN_H, D_MODEL)),
                  _resident((1, D_MODEL)), _resident((1, D_MODEL))],
        out_specs=tile_spec(D_MODEL),
        out_shape=jax.ShapeDtypeStruct((n_b, n_q * TILE, D_MODEL), F32),
        compiler_params=_cparams(("arbitrary", "arbitrary")),
        name="ffn",
    )(x1, mod_l, mod_l, mod_l, p["ffn_wi"], p["ffn_wo"], p["ln2_g"], p["ln2_b"])


def _prep_layer(l, w_in, gla_w_a2, gla_b_a, mla_q_norm_g, mla_kv_norm_g, mla_w_uq, mla_w_ukv, w_branch, w_out,
                ln1_g, ln1_b, ffn_w_in, ffn_w_out, ln2_g, ln2_b):
    w = w_in[l]
    seg = lambda a, b: w[:, a:b]
    w_tok = jnp.concatenate([
        seg(O_GQ, O_GK), seg(O_GK, O_GV), seg(O_GV, O_GR), seg(O_GR, O_GA), seg(O_DK, O_DV),
        seg(O_MQ, O_MKV), seg(O_MKV, O_MKR),
        seg(O_MKR, O_GATES), seg(O_GA, O_DQ), jnp.zeros((D_MODEL, 32), F32)], axis=1).astype(BF)
    w_featT = jnp.concatenate([seg(O_DQ, O_DK) * DIFF_DH ** -0.5, seg(O_DV, O_MQ), seg(O_GATES, O_END)],
                              axis=1).T.astype(BF)
    wa2 = jnp.zeros((128, 1024), F32)
    wa2 = wa2.at[64:80, 0:512].set(gla_w_a2[l, 0]).at[80:96, 512:1024].set(gla_w_a2[l, 1]).astype(BF)
    ukv = mla_w_ukv[l].reshape(MLA_KVR, MLA_H, MLA_NOPE + MLA_DV)
    return dict(
        w_tok=w_tok, w_featT=w_featT, wa2=wa2, ba=gla_b_a[l].reshape(1, 1024),
        qg=mla_q_norm_g[l].reshape(1, MLA_QR), kvg=mla_kv_norm_g[l].reshape(1, MLA_KVR),
        wukv_k=ukv[:, :, :MLA_NOPE].reshape(MLA_KVR, 1024).astype(BF),
        wukv_vT=ukv[:, :, MLA_NOPE:].reshape(MLA_KVR, 1024).T.astype(BF),
        wuqT=mla_w_uq[l].T.astype(BF),
        wbT=jnp.swapaxes(w_branch[l], 1, 2).astype(BF), woT=w_out[l].T.astype(BF),
        ln1_g=ln1_g[l].reshape(1, D_MODEL), ln1_b=ln1_b[l].reshape(1, D_MODEL),
        ffn_wi=ffn_w_in[l].astype(BF), ffn_wo=ffn_w_out[l].astype(BF),
        ln2_g=ln2_g[l].reshape(1, D_MODEL), ln2_b=ln2_b[l].reshape(1, D_MODEL))


def _rope_tables(l_lat, l_ctx):
    rows = l_lat // GRID_W
    pos_row = jnp.broadcast_to(jnp.arange(rows, dtype=F32)[:, None], (rows, GRID_W)).reshape(l_lat)
    pos_col = jnp.broadcast_to(jnp.arange(GRID_W, dtype=F32)[None, :], (rows, GRID_W)).reshape(l_lat)
    d_axis = ROPE_DIM // 2
    inv = ROPE_BASE ** (-jnp.arange(0, d_axis, 2, dtype=F32) / d_axis)
    ang = jnp.concatenate([pos_row[:, None] * inv, pos_col[:, None] * inv], axis=-1)
    cos = jnp.concatenate([jnp.cos(ang), jnp.ones((l_ctx, 32), F32)], axis=0)
    sin = jnp.concatenate([jnp.sin(ang), jnp.zeros((l_ctx, 32), F32)], axis=0)
    ctok = jnp.tile(cos, (1, 4))
    stok = jnp.tile(jnp.concatenate([-sin, sin], axis=1), (1, 2))
    n_t = (l_lat + l_ctx) // TILE
    to_fm = lambda a: a.T.reshape(32, n_t, TILE).transpose(1, 0, 2)
    return ctok, stok, to_fm(cos), to_fm(sin)


def kernel(x, c, ctx, c_ctx, w_mod, b_mod, w_in, gla_w_a2, gla_b_a, gla_norm_g, diff_lam, diff_norm_g,
           mla_q_norm_g, mla_kv_norm_g, mla_w_uq, mla_w_ukv, w_branch, w_out, ln1_g, ln1_b, ffn_w_in, ffn_w_out,
           ln2_g, ln2_b):
    n_b, l_lat, _ = x.shape
    l_ctx = ctx.shape[1]
    assert l_ctx == TILE and n_b + TOK_BATCH <= 16
    assert l_lat % (TILE * max(DIFF_QS, MLA_QS)) == 0 and l_lat % (TILE * KEY_TILES * KEY_UNROLL) == 0
    ltot = l_lat + l_ctx
    n_t = ltot // TILE
    ctok, stok, cosT, sinT = _rope_tables(l_lat, l_ctx)
    c_all = jnp.zeros((16, D_MODEL), F32).at[:n_b].set(c).at[n_b:n_b + TOK_BATCH].set(c_ctx)
    mod = _modulation(c_all, w_mod, b_mod)
    ii = lax.broadcasted_iota(jnp.int32, (TILE, TILE), 0)
    jj = lax.broadcasted_iota(jnp.int32, (TILE, TILE), 1)
    same = (ii // GLA_CHUNK) == (jj // GLA_CHUNK)
    tri_lo = (same & (jj <= ii)).astype(BF)
    tri_up = (same & (jj >= ii)).astype(BF)
    xs = (x, ctx, 0)
    for l in range(N_LAYERS):
        last = l == N_LAYERS - 1
        n_q = n_t - 1 if last else n_t
        lam_init = 0.8 - 0.6 * math.exp(-0.3 * l)
        p = _prep_layer(l, w_in, gla_w_a2, gla_b_a, mla_q_norm_g, mla_kv_norm_g, mla_w_uq, mla_w_ukv, w_branch,
                        w_out, ln1_g, ln1_b, ffn_w_in, ffn_w_out, ln2_g, ln2_b)
        mod_l = mod[l].reshape(16, 1, 6 * D_MODEL)
        gq, gk, gv, gr, dk, la, mk, cq, ckv = _proj_tok(xs, mod_l, ctok, stok, p, n_b, n_t, ltot)
        dqT, dvT, gatesT, mqT, mvT = _proj_feat(xs, mod_l, cosT, sinT, cq, ckv, p, n_b, n_t)
        ya = _gla(gq, gk, gv, la, gr, tri_lo, tri_up, gla_norm_g[l].reshape(1, GLA_DV), n_b, n_t)
        gcol = jnp.broadcast_to(diff_norm_g[l].reshape(128, 1), (128, TILE))
        ybT = _diff_attn(dqT, dk, dvT, diff_lam[l], gcol, n_b, n_t, lam_init)
        ycT = _mla_attn(mqT, mk, mvT, n_b, n_t)
        if not last:
            ybT = _diff_attn(dqT, dk, dvT, diff_lam[l], gcol, n_b, n_t, lam_init, ctx_into=ybT)
            ycT = _mla_attn(mqT, mk, mvT, n_b, n_t, ctx_into=ycT)
        x1 = _merge(xs, ya, ybT, ycT, gatesT, mod_l, p, n_b, n_t, n_q)
        x2 = _ffn(x1, mod_l, p, n_b, n_t, n_q)
        xs = (x2, x2, n_t - 1)
    return x2
```

```python
import functools
import math

import jax
import jax.numpy as jnp
from jax import lax
from jax.experimental import pallas as pl
from jax.experimental.pallas import tpu as pltpu

BF = jnp.bfloat16
F32 = jnp.float32

D_MODEL = 1024
N_LAYERS = 2
GRID_W = 64
TILE = 256
GLA_H, GLA_DK, GLA_DV, GLA_RANK, GLA_TAU = 4, 128, 256, 16, 16.0
GLA_CHUNK, GLA_SUB = 64, 16
GLA_BATCH = 2
TOK_BATCH = 2
DIFF_H, DIFF_DH = 8, 64
MLA_H, MLA_QR, MLA_KVR, MLA_NOPE, MLA_ROPE, MLA_DV = 8, 256, 128, 128, 64, 128
MLA_SCALE = (MLA_NOPE + MLA_ROPE) ** -0.5
ROPE_DIM, ROPE_BASE = 64, 10000.0
FFN_H = 2816
FFN_CHUNKS = ((0, 1536), (1536, 2816))
DN_ALPHA = (2 * N_LAYERS) ** 0.25
EPS = 1e-6
DIFF_QS, MLA_QS = 2, 4
KEY_TILES = 2
KEY_UNROLL = 4
LOG2E = math.log2(math.e)
ONES_ROWS = 16
VMEM_LIMIT = 56 * 1024 * 1024

_SIZES = (512, 512, 1024, 1024, 32, 1024, 1024, 1024, 256, 128, 64, 3072)
_OFF = [0]
for _s in _SIZES:
    _OFF.append(_OFF[-1] + _s)
(O_GQ, O_GK, O_GV, O_GR, O_GA, O_DQ, O_DK, O_DV, O_MQ, O_MKV, O_MKR, O_GATES, O_END) = _OFF

T_GQ, T_GK, T_GV, T_GR, T_DK, T_MQ, T_MKV, T_SMA, T_END = (0, 512, 1024, 2048, 3072, 4096, 4352, 4480, 4608)
F_DQ, F_DV, F_GATES, F_END = 0, 1024, 2048, 5120


def _cparams(sem):
    return pltpu.CompilerParams(dimension_semantics=sem, vmem_limit_bytes=VMEM_LIMIT)


def _resident(shape):
    nd = len(shape)
    return pl.BlockSpec(shape, lambda *_: (0,) * nd, pipeline_mode=pl.Buffered(1))


def _dot(a, b):
    return jnp.dot(a, b, preferred_element_type=F32)


def _dot_nt(a, b):
    return lax.dot_general(a, b, (((1,), (1,)), ((), ())), preferred_element_type=F32)


def _dot_tn(a, b):
    return lax.dot_general(a, b, (((0,), (0,)), ((), ())), preferred_element_type=F32)


def _ln_rows(x):
    mu = jnp.mean(x, axis=-1, keepdims=True)
    xc = x - mu
    var = jnp.mean(xc * xc, axis=-1, keepdims=True)
    return xc * lax.rsqrt(var + EPS)


def _sigmoid(x):
    return 1.0 / (1.0 + jnp.exp(-x))


def _silu(x):
    return x * _sigmoid(x)


def _mod_kernel(c_ref, w_ref, b_ref, o_ref):
    ca = _silu(c_ref[...]).astype(BF)
    o_ref[0] = _dot(ca, w_ref[0].astype(BF)) + b_ref[0]


def _modulation(c_all, w_mod, b_mod):
    nl, d, n6 = w_mod.shape
    r = c_all.shape[0]
    nblk = 1536
    return pl.pallas_call(
        _mod_kernel,
        grid=(nl, n6 // nblk),
        in_specs=[pl.BlockSpec((r, d), lambda l, j: (0, 0)),
                  pl.BlockSpec((1, d, nblk), lambda l, j: (l, 0, j)),
                  pl.BlockSpec((1, 1, nblk), lambda l, j: (l, 0, j))],
        out_specs=pl.BlockSpec((1, r, nblk), lambda l, j: (l, 0, j)),
        out_shape=jax.ShapeDtypeStruct((nl, r, n6), F32),
        compiler_params=_cparams(("arbitrary", "arbitrary")),
        name="modulation",
    )(c_all, w_mod, b_mod.reshape(nl, 1, n6))


def _batch_rows(n_b):
    return TOK_BATCH if n_b % TOK_BATCH == 0 else 1


_DONE = object()


def _per_batch_row(body, n_rowed, n_shared):
    def kern(*refs, **kw):
        rowed, shared, outs = refs[:n_rowed], refs[n_rowed:n_rowed + n_shared], refs[n_rowed + n_shared:]
        pending = []
        for bi in range(rowed[0].shape[0]):
            row = lambda ref, bi=bi: ref.at[pl.ds(bi, 1)]
            pending.append(body(*map(row, rowed), *shared, *map(row, outs), **kw))
        pending = [g for g in pending if g is not None]
        while pending:
            pending = [g for g in pending if next(g, _DONE) is not _DONE]
    return kern


def _x_specs(rows, n_lat, ctx_blk):
    return [pl.BlockSpec((rows, TILE, D_MODEL), lambda b, t: (b, jnp.minimum(t, n_lat - 1), 0)),
            pl.BlockSpec((rows, TILE, D_MODEL), lambda b, t: (b, ctx_blk, 0))]


def _x_tile(x_ref, c_ref, n_lat):
    return jnp.where(pl.program_id(1) == n_lat, c_ref[0], x_ref[0])


def _proj_tok_kernel(x_ref, c_ref, sh_ref, sc_ref, ct_ref, st_ref, w_ref, wa2_ref, ba_ref, qg_ref, kvg_ref, wk_ref,
                     gq_ref, gk_ref, gv_ref, gr_ref, dk_ref, la_ref, mk_ref, cq_ref, ckv_ref, *, n_lat):
    h = (_ln_rows(_x_tile(x_ref, c_ref, n_lat)) * (1.0 + sc_ref[0]) + sh_ref[0]).astype(BF)

    def proj(a, b):
        return _dot(h, w_ref[:, a:b])

    ct = ct_ref[...]
    st = st_ref[...]
    first_half = lax.broadcasted_iota(jnp.int32, (TILE, 128), 1) % ROPE_DIM < ROPE_DIM // 2

    def rope(x):
        partner = jnp.where(first_half, pltpu.roll(x, 128 - ROPE_DIM // 2, 1), pltpu.roll(x, ROPE_DIM // 2, 1))
        return x * ct + partner * st

    mm = proj(T_MKV, T_END)
    mkv = mm[:, 0:128]
    sma = mm[:, 128:256]
    ckv = (mkv * lax.rsqrt(jnp.mean(mkv * mkv, axis=-1, keepdims=True) + EPS) * kvg_ref[...]).astype(BF)
    ckv_ref[0] = ckv
    mq = proj(T_MQ, T_MKV)
    cq = mq * lax.rsqrt(jnp.mean(mq * mq, axis=-1, keepdims=True) + EPS) * qg_ref[...]
    cq_ref[0] = cq.astype(BF)
    gq_ref[0] = (proj(T_GQ, T_GK) * GLA_DK ** -0.5).astype(BF)
    gk_ref[0] = proj(T_GK, T_GV).astype(BF)
    gv_ref[0] = proj(T_GV, T_GR).astype(BF)
    gr_ref[0] = proj(T_GR, T_DK).astype(BF)
    dkx = proj(T_DK, T_MQ)
    for hh in range(DIFF_H):
        sl = slice(128 * hh, 128 * (hh + 1))
        dk_ref[0, :, sl] = rope(dkx[:, sl]).astype(BF)
    yield
    kn = _dot(ckv, wk_ref[...]).astype(BF)
    kr = rope(sma)[:, 0:64].astype(BF)
    for hh in range(MLA_H):
        mk_ref[0, hh, :, 0:128] = kn[:, 128 * hh:128 * (hh + 1)]
        mk_ref[0, hh, :, 128:192] = kr
    z = _dot(sma.astype(BF), wa2_ref[...]) + ba_ref[...]
    la_ref[0] = (jnp.minimum(z, 0.0) - jnp.log(1.0 + jnp.exp(-jnp.abs(z)))) * (1.0 / GLA_TAU)


def _proj_tok(xs, mod_l, ctok, stok, p, n_b, n_t, ltot):
    x_lat, x_ctx, ctx_blk = xs
    rows = _batch_rows(n_b)
    tile_spec = lambda c: pl.BlockSpec((rows, TILE, c), lambda b, t: (b, t, 0))
    mrow = lambda b, t: jnp.where(t == n_t - 1, n_b // rows, b)
    bf_out = lambda c: jax.ShapeDtypeStruct((n_b, ltot, c), BF)
    return pl.pallas_call(
        functools.partial(_per_batch_row(_proj_tok_kernel, 4, 8), n_lat=n_t - 1),
        grid=(n_b // rows, n_t),
        in_specs=_x_specs(rows, n_t - 1, ctx_blk) + [
                  pl.BlockSpec((rows, 1, D_MODEL), lambda b, t: (mrow(b, t), 0, 0)),
                  pl.BlockSpec((rows, 1, D_MODEL), lambda b, t: (mrow(b, t), 0, 1)),
                  pl.BlockSpec((TILE, 128), lambda b, t: (t, 0)),
                  pl.BlockSpec((TILE, 128), lambda b, t: (t, 0)),
                  _resident((D_MODEL, T_END)), _resident((128, 1024)), _resident((1, 1024)),
                  _resident((1, MLA_QR)), _resident((1, MLA_KVR)), _resident((MLA_KVR, 1024))],
        out_specs=[tile_spec(512), tile_spec(512), tile_spec(1024), tile_spec(1024), tile_spec(1024),
                   tile_spec(1024),
                   pl.BlockSpec((rows, MLA_H, TILE, 192), lambda b, t: (b, 0, t, 0)),
                   tile_spec(MLA_QR), tile_spec(MLA_KVR)],
        out_shape=[bf_out(512), bf_out(512), bf_out(1024), bf_out(1024), bf_out(1024),
                   jax.ShapeDtypeStruct((n_b, ltot, 1024), F32),
                   jax.ShapeDtypeStruct((n_b, MLA_H, ltot, 192), BF),
                   bf_out(MLA_QR), bf_out(MLA_KVR)],
        compiler_params=_cparams(("arbitrary", "arbitrary")),
        name="proj_tok",
    )(x_lat, x_ctx, mod_l, mod_l, ctok, stok, p["w_tok"], p["wa2"], p["ba"], p["qg"], p["kvg"], p["wukv_k"])


def _rope_rows(x1, x2, cos, sin):
    return x1 * cos - x2 * sin, x1 * sin + x2 * cos


def _proj_feat_kernel(x_ref, c_ref, sh_ref, sc_ref, cq_ref, ckv_ref, cos_ref, sin_ref, w_ref, wuq_ref, wv_ref,
                      dq_ref, dv_ref, g_ref, mq_ref, mv_ref, *, n_lat):
    h = (_ln_rows(_x_tile(x_ref, c_ref, n_lat)) * (1.0 + sc_ref[0]) + sh_ref[0]).astype(BF)
    cos = cos_ref[0]
    sin = sin_ref[0]
    dq = _dot_nt(w_ref[F_DQ:F_DV, :], h) * LOG2E
    for g in range(2 * DIFF_H):
        o1, o2 = _rope_rows(dq[64 * g:64 * g + 32], dq[64 * g + 32:64 * (g + 1)], cos, sin)
        dq_ref[0, 0, 64 * g:64 * g + 32, :] = o1.astype(BF)
        dq_ref[0, 0, 64 * g + 32:64 * (g + 1), :] = o2.astype(BF)
    dv_ref[0, 0] = _dot_nt(w_ref[F_DV:F_GATES, :], h).astype(BF)
    for j in range(3):
        a = F_GATES + 1024 * j
        g_ref[0, 0, 1024 * j:1024 * (j + 1), :] = _dot_nt(w_ref[a:a + 1024, :], h).astype(BF)
    cq = cq_ref[0]
    mq = _dot_nt(wuq_ref[...], cq) * (MLA_SCALE * LOG2E)
    for hh in range(MLA_H):
        a = 192 * hh
        o1, o2 = _rope_rows(mq[a + 128:a + 160], mq[a + 160:a + 192], cos, sin)
        mq_ref[0, 0, a:a + 128, :] = mq[a:a + 128].astype(BF)
        mq_ref[0, 0, a + 128:a + 160, :] = o1.astype(BF)
        mq_ref[0, 0, a + 160:a + 192, :] = o2.astype(BF)
    mv_ref[0, 0] = _dot_nt(wv_ref[...], ckv_ref[0]).astype(BF)


def _proj_feat(xs, mod_l, cosT, sinT, cq, ckv, p, n_b, n_t):
    x_lat, x_ctx, ctx_blk = xs
    rows = _batch_rows(n_b)
    tile_spec = lambda c: pl.BlockSpec((rows, TILE, c), lambda b, t: (b, t, 0))
    mrow = lambda b, t: jnp.where(t == n_t - 1, n_b // rows, b)
    fm_spec = lambda c: pl.BlockSpec((rows, 1, c, TILE), lambda b, t: (b, t, 0, 0))
    fm_out = lambda c: jax.ShapeDtypeStruct((n_b, n_t, c, TILE), BF)
    return pl.pallas_call(
        functools.partial(_per_batch_row(_proj_feat_kernel, 6, 5), n_lat=n_t - 1),
        grid=(n_b // rows, n_t),
        in_specs=_x_specs(rows, n_t - 1, ctx_blk) + [
                  pl.BlockSpec((rows, 1, D_MODEL), lambda b, t: (mrow(b, t), 0, 0)),
                  pl.BlockSpec((rows, 1, D_MODEL), lambda b, t: (mrow(b, t), 0, 1)),
                  tile_spec(MLA_QR), tile_spec(MLA_KVR),
                  pl.BlockSpec((1, 32, TILE), lambda b, t: (t, 0, 0)),
                  pl.BlockSpec((1, 32, TILE), lambda b, t: (t, 0, 0)),
                  _resident((F_END, D_MODEL)), _resident((MLA_H * 192, MLA_QR)), _resident((1024, MLA_KVR))],
        out_specs=[fm_spec(1024), fm_spec(1024), fm_spec(3072), fm_spec(MLA_H * 192), fm_spec(1024)],
        out_shape=[fm_out(1024), fm_out(1024), fm_out(3072), fm_out(MLA_H * 192), fm_out(1024)],
        compiler_params=_cparams(("arbitrary", "arbitrary")),
        name="proj_feat",
    )(x_lat, x_ctx, mod_l, mod_l, cq, ckv, cosT, sinT, p["w_featT"], p["wuqT"], p["wukv_vT"])


def _flash_T(k_ref, v_ref, q2, m_ref, acc_ref, s_ref, n_lat, latent_keys):
    dv = v_ref.shape[2]
    m_ref[2] = jnp.full(m_ref.shape[1:], -jnp.inf, F32)
    acc_ref[...] = jnp.zeros(acc_ref.shape, F32)
    rows = KEY_TILES * TILE

    def scores(slot, tile0):
        s = _dot(k_ref[0, pl.ds(pl.multiple_of(tile0 * TILE, TILE), rows), :], q2)
        s_ref[slot] = s
        m_ref[slot] = jnp.max(s, axis=0, keepdims=True)

    def consume(s, s_max, tile0, ntiles):
        m_old = m_ref[2]
        m_new = jnp.maximum(m_old, s_max)
        alpha = jnp.exp2(m_old - m_new)
        p = jnp.exp2(s - m_new).astype(BF)
        v = jnp.concatenate([v_ref[0, tile0 + i] for i in range(ntiles)], axis=1)
        vext = jnp.concatenate([v, jnp.ones((ONES_ROWS, ntiles * TILE), BF)], axis=0)
        acc_ref[...] = acc_ref[...] * alpha + _dot(vext, p)
        m_ref[2] = m_new

    def ctx_step(tile):
        s = _dot(k_ref[0, pl.ds(tile * TILE, TILE), :], q2)
        return s, jnp.max(s, axis=0, keepdims=True)

    if latent_keys:
        n_steps = n_lat // KEY_TILES
        scores(0, 0)

        def body(jj, carry):
            for u in range(KEY_UNROLL):
                j = KEY_UNROLL * jj + u
                scores((u + 1) % 2, (j + 1) * KEY_TILES)
                consume(s_ref[u % 2], m_ref[u % 2], j * KEY_TILES, KEY_TILES)
            return carry
        lax.fori_loop(0, n_steps // KEY_UNROLL - 1, body, 0)
        for u in range(KEY_UNROLL):
            j = n_steps - KEY_UNROLL + u
            if u < KEY_UNROLL - 1:
                scores((u + 1) % 2, (j + 1) * KEY_TILES)
            else:
                s_ctx, mx_ctx = ctx_step(n_lat)
            consume(s_ref[u % 2], m_ref[u % 2], j * KEY_TILES, KEY_TILES)
        consume(s_ctx, mx_ctx, n_lat, 1)
    else:
        consume(*ctx_step(0), 0, 1)
    acc = acc_ref[...]
    return acc[0:dv] * (1.0 / acc[dv:dv + 1])


def _attn_call(kern, name, q_rows, k_spec_fn, n_b, n_heads, n_t, qs, scratch, qT, k, vT, extra, extra_specs,
               ctx_into=None):
    n_lat = n_t - 1
    if ctx_into is None:
        grid = (n_b, n_heads, n_lat // qs)
        q_spec = pl.BlockSpec((1, qs, q_rows, TILE), lambda b, h, q: (b, q, h, 0))
        k_spec = k_spec_fn(n_t * TILE, 0)
        v_spec = pl.BlockSpec((1, n_t, 128, TILE), lambda b, h, q: (b, 0, h, 0))
        o_spec = pl.BlockSpec((1, qs, 128, TILE), lambda b, h, q: (b, q, h, 0))
        alias_in, alias_specs, aliases = [], [], {}
    else:
        grid = (n_b, n_heads, 1)
        q_spec = pl.BlockSpec((1, 1, q_rows, TILE), lambda b, h, q: (b, n_lat, h, 0))
        k_spec = k_spec_fn(TILE, n_lat)
        v_spec = pl.BlockSpec((1, 1, 128, TILE), lambda b, h, q: (b, n_lat, h, 0))
        o_spec = pl.BlockSpec((1, 1, 128, TILE), lambda b, h, q: (b, n_lat, h, 0))
        alias_in, alias_specs = [ctx_into], [pl.BlockSpec(memory_space=pl.ANY)]
        aliases = {3 + len(extra): 0}
    return pl.pallas_call(
        kern,
        grid=grid,
        in_specs=[q_spec, k_spec, v_spec] + extra_specs + alias_specs,
        out_specs=o_spec,
        out_shape=jax.ShapeDtypeStruct((n_b, n_t, 1024, TILE), BF),
        scratch_shapes=scratch,
        input_output_aliases=aliases,
        compiler_params=_cparams(("arbitrary", "arbitrary", "arbitrary")),
        name=name,
    )(qT, k, vT, *extra, *alias_in)


def _diff_attn_kernel(q_ref, k_ref, v_ref, lam_ref, g_ref, *rest, n_lat, lam_init, qs, latent):
    o_ref, q2_ref, m_ref, acc_ref, s_ref = rest[-5:]
    nq = qs * TILE
    zeros = jnp.zeros((DIFF_DH, TILE), BF)
    for i in range(qs):
        q = q_ref[0, i]
        q2_ref[0:64, i * TILE:(i + 1) * TILE] = q[0:64]
        q2_ref[64:128, i * TILE:(i + 1) * TILE] = zeros
        q2_ref[0:64, nq + i * TILE:nq + (i + 1) * TILE] = zeros
        q2_ref[64:128, nq + i * TILE:nq + (i + 1) * TILE] = q[64:128]
    o = _flash_T(k_ref, v_ref, q2_ref[...], m_ref, acc_ref, s_ref, n_lat, latent)
    dl = lam_ref[...]
    lam = (jnp.exp(jnp.sum(dl[0:1] * dl[1:2], axis=1, keepdims=True))
           - jnp.exp(jnp.sum(dl[2:3] * dl[3:4], axis=1, keepdims=True)) + lam_init)
    od = o[:, 0:nq] - lam * o[:, nq:2 * nq]
    y = od * lax.rsqrt(jnp.mean(od * od, axis=0, keepdims=True) + EPS) * (1.0 - lam_init)
    for i in range(qs):
        o_ref[0, i] = (y[:, i * TILE:(i + 1) * TILE] * g_ref[...]).astype(BF)


def _diff_attn(dqT, dk, dvT, dlam, gcol, n_b, n_t, lam_init, ctx_into=None):
    qs = DIFF_QS if ctx_into is None else 1
    kern = functools.partial(_diff_attn_kernel, n_lat=n_t - 1, lam_init=lam_init, qs=qs, latent=ctx_into is None)
    k_spec_fn = lambda rows, blk: pl.BlockSpec((1, rows, 128), lambda b, h, q: (b, blk, h))
    scratch = [pltpu.VMEM((128, 2 * qs * TILE), BF), pltpu.VMEM((3, 1, 2 * qs * TILE), F32),
               pltpu.VMEM((128 + ONES_ROWS, 2 * qs * TILE), F32),
               pltpu.VMEM((2, KEY_TILES * TILE, 2 * qs * TILE), F32)]
    extra_specs = [pl.BlockSpec((4, DIFF_DH), lambda b, h, q: (0, 0)),
                   pl.BlockSpec((128, TILE), lambda b, h, q: (0, 0))]
    return _attn_call(kern, "diff_attn" if ctx_into is None else "diff_attn_ctx", 128, k_spec_fn, n_b, DIFF_H, n_t,
                      qs, scratch, dqT, dk, dvT, [dlam, gcol], extra_specs, ctx_into)


def _mla_attn_kernel(q_ref, k_ref, v_ref, *rest, n_lat, qs, latent):
    o_ref, q2_ref, m_ref, acc_ref, s_ref = rest[-5:]
    for i in range(qs):
        q2_ref[:, i * TILE:(i + 1) * TILE] = q_ref[0, i]
    o = _flash_T(k_ref.at[0], v_ref, q2_ref[...], m_ref, acc_ref, s_ref, n_lat, latent)
    for i in range(qs):
        o_ref[0, i] = o[:, i * TILE:(i + 1) * TILE].astype(BF)


def _mla_attn(mqT, mk, mvT, n_b, n_t, ctx_into=None):
    qs = MLA_QS if ctx_into is None else 1
    kern = functools.partial(_mla_attn_kernel, n_lat=n_t - 1, qs=qs, latent=ctx_into is None)
    k_spec_fn = lambda rows, blk: pl.BlockSpec((1, 1, rows, 192), lambda b, h, q: (b, h, blk, 0))
    scratch = [pltpu.VMEM((192, qs * TILE), BF), pltpu.VMEM((3, 1, qs * TILE), F32),
               pltpu.VMEM((128 + ONES_ROWS, qs * TILE), F32),
               pltpu.VMEM((2, KEY_TILES * TILE, qs * TILE), F32)]
    return _attn_call(kern, "mla_attn" if ctx_into is None else "mla_attn_ctx", 192, k_spec_fn, n_b, MLA_H, n_t,
                      qs, scratch, mqT, mk, mvT, [], [], ctx_into)


def _gla_chunk_order(reverse):
    n_chunk = TILE // GLA_CHUNK
    return range(n_chunk - 1, -1, -1) if reverse else range(n_chunk)


def _gla_tile(q_ref, k_ref, v_ref, la_ref, tri_ref, bi, reverse, work):
    g = la_ref[bi]
    g1 = g.astype(BF)
    r1 = g - g1.astype(F32)
    g2 = r1.astype(BF)
    tri = tri_ref[...]
    bcum = _dot(tri, g1) + _dot(tri, g2)
    n_sub = GLA_CHUNK // GLA_SUB
    r16 = lax.broadcasted_iota(jnp.int32, (GLA_SUB, GLA_CHUNK), 0)
    c16 = lax.broadcasted_iota(jnp.int32, (GLA_SUB, GLA_CHUNK), 1)
    for hh in range(GLA_H):
        kc0, kc1 = GLA_DK * hh, GLA_DK * (hh + 1)
        for c in _gla_chunk_order(reverse):
            r0 = GLA_CHUNK * c
            bc = bcum[r0:r0 + GLA_CHUNK, kc0:kc1]
            qc = q_ref[bi, r0:r0 + GLA_CHUNK, kc0:kc1].astype(F32)
            kc = k_ref[bi, r0:r0 + GLA_CHUNK, kc0:kc1].astype(F32)
            vc = v_ref[bi, r0:r0 + GLA_CHUNK, GLA_DV * hh:GLA_DV * (hh + 1)]
            b_tot = bc[0:1] if reverse else bc[GLA_CHUNK - 1:GLA_CHUNK]
            a_rows = []
            for i in range(n_sub):
                s0 = GLA_SUB * i
                ref_row = bc[s0 + GLA_SUB - 1:s0 + GLA_SUB] if reverse else bc[s0:s0 + 1]
                qs = (qc[s0:s0 + GLA_SUB] * jnp.exp(bc[s0:s0 + GLA_SUB] - ref_row)).astype(BF)
                lo, hi = (s0, GLA_CHUNK) if reverse else (0, s0 + GLA_SUB)
                ks = (kc[lo:hi] * jnp.exp(ref_row - bc[lo:hi])).astype(BF)
                pad = jnp.zeros((GLA_CHUNK - (hi - lo), GLA_DK), BF)
                ks = jnp.concatenate([pad, ks] if reverse else [ks, pad], axis=0) if hi - lo < GLA_CHUNK else ks
                a = _dot_nt(qs, ks)
                keep = (c16 > r16 + s0) if reverse else (c16 <= r16 + s0)
                a_rows.append(jnp.where(keep, a, 0.0))
            q_in = (qc * jnp.exp(bc)).astype(BF)
            k_d = (kc * jnp.exp(b_tot - bc)).astype(BF)
            work[hh, c] = (a_rows, q_in, vc, jnp.exp(b_tot), _dot_tn(vc, k_d))
            yield


def _gla_finish(work, reverse, st_ref, bi, outs):
    heads = []
    for hh in range(GLA_H):
        o_chunks = [None] * (TILE // GLA_CHUNK)
        st = st_ref[bi, hh]
        for c in _gla_chunk_order(reverse):
            a_rows, q_in, vc, decay, st_inc = work[hh, c]
            a_mat = jnp.concatenate(a_rows, axis=0).astype(BF)
            o_chunks[c] = _dot(a_mat, vc) + _dot_nt(q_in, st.astype(BF))
            st = st * decay + st_inc
            yield
        st_ref[bi, hh] = st
        heads.append(jnp.concatenate(o_chunks, axis=0))
    outs[bi] = jnp.concatenate(heads, axis=1)


def _gla_rows(q_ref, k_ref, v_ref, la_ref, tri_ref, st_ref, reverse):
    n = q_ref.shape[0]
    works = [{} for _ in range(n)]
    outs = [None] * n
    free = [_gla_tile(q_ref, k_ref, v_ref, la_ref, tri_ref, bi, reverse, works[bi]) for bi in range(n)]
    dep = [_gla_finish(works[bi], reverse, st_ref, bi, outs) for bi in range(n)]
    for _ in free[0]:
        pass
    for bi in range(1, n):
        pending = [free[bi], dep[bi - 1]]
        while pending:
            pending = [g for g in pending if next(g, _DONE) is not _DONE]
    for _ in dep[n - 1]:
        pass
    return outs


def _gla_fwd_kernel(q_ref, k_ref, v_ref, la_ref, tri_ref, o_ref, st_ref):
    @pl.when(pl.program_id(1) == 0)
    def _():
        st_ref[...] = jnp.zeros(st_ref.shape, F32)
    for bi, o in enumerate(_gla_rows(q_ref, k_ref, v_ref, la_ref, tri_ref, st_ref, False)):
        o_ref[bi] = o


def _gla_bwd_kernel(q_ref, k_ref, v_ref, la_ref, tri_ref, of_ref, r_ref, g_ref, y_ref, st_ref):
    @pl.when(pl.program_id(1) == 0)
    def _():
        st_ref[...] = jnp.zeros(st_ref.shape, F32)
    gn = g_ref[...]
    for bi, o_b in enumerate(_gla_rows(q_ref, k_ref, v_ref, la_ref, tri_ref, st_ref, True)):
        o = of_ref[bi] + o_b
        for hh in range(GLA_H):
            sl = slice(GLA_DV * hh, GLA_DV * (hh + 1))
            oh = o[:, sl]
            yh = oh * lax.rsqrt(jnp.mean(oh * oh, axis=-1, keepdims=True) + EPS) * gn
            y_ref[bi, :, sl] = (yh * _silu(r_ref[bi, :, sl].astype(F32))).astype(BF)


def _gla(gq, gk, gv, la, gr, tri_lo, tri_up, gnorm, n_b, n_t):
    ltot = n_t * TILE
    rows = GLA_BATCH if n_b % GLA_BATCH == 0 else 1
    t_fwd = lambda s: (s + n_t - 1) % n_t
    t_bwd = lambda s: jnp.where(s == 0, n_t - 1, n_t - 1 - s)

    def specs(tmap, dirn):
        ts = lambda c, cb=0: pl.BlockSpec((rows, TILE, c), lambda b, s: (b, tmap(s), cb))
        return [ts(512), ts(512), ts(1024), ts(512, dirn), pl.BlockSpec((TILE, TILE), lambda b, s: (0, 0))]

    o_f = pl.pallas_call(
        _gla_fwd_kernel,
        grid=(n_b // rows, n_t),
        in_specs=specs(t_fwd, 0),
        out_specs=pl.BlockSpec((rows, TILE, 1024), lambda b, s: (b, t_fwd(s), 0)),
        out_shape=jax.ShapeDtypeStruct((n_b, ltot, 1024), F32),
        scratch_shapes=[pltpu.VMEM((rows, GLA_H, GLA_DV, GLA_DK), F32)],
        compiler_params=_cparams(("arbitrary", "arbitrary")),
        name="gla_fwd",
    )(gq, gk, gv, la, tri_lo)
    tsb = lambda c: pl.BlockSpec((rows, TILE, c), lambda b, s: (b, t_bwd(s), 0))
    return pl.pallas_call(
        _gla_bwd_kernel,
        grid=(n_b // rows, n_t),
        in_specs=specs(t_bwd, 1) + [tsb(1024), tsb(1024), pl.BlockSpec((1, GLA_DV), lambda b, s: (0, 0))],
        out_specs=tsb(1024),
        out_shape=jax.ShapeDtypeStruct((n_b, ltot, 1024), BF),
        scratch_shapes=[pltpu.VMEM((rows, GLA_H, GLA_DV, GLA_DK), F32)],
        compiler_params=_cparams(("arbitrary", "arbitrary")),
        name="gla_bwd",
    )(gq, gk, gv, la, tri_up, o_f, gr, gnorm)


def _merge_kernel(x_ref, c_ref, ya_ref, yb_ref, yc_ref, g_ref, g1_ref, wb_ref, wo_ref, lg_ref, lb_ref, o_ref, *,
                  n_lat):
    zt = (_sigmoid(g_ref[0, 0, 0:1024, :].astype(F32)) * _dot_nt(wb_ref[0], ya_ref[0])
          + _sigmoid(g_ref[0, 0, 1024:2048, :].astype(F32)) * _dot(wb_ref[1], yb_ref[0, 0])
          + _sigmoid(g_ref[0, 0, 2048:3072, :].astype(F32)) * _dot(wb_ref[2], yc_ref[0, 0]))
    yield
    u = _dot(wo_ref[...], zt.astype(BF)).T
    y = _ln_rows(DN_ALPHA * _x_tile(x_ref, c_ref, n_lat) + g1_ref[0] * u)
    o_ref[0] = y * lg_ref[...] + lb_ref[...]


def _merge(xs, ya, ybT, ycT, gatesT, mod_l, p, n_b, n_t, n_q):
    x_lat, x_ctx, ctx_blk = xs
    rows = _batch_rows(n_b)
    tile_spec = lambda c: pl.BlockSpec((rows, TILE, c), lambda b, t: (b, t, 0))
    fm_spec = lambda c: pl.BlockSpec((rows, 1, c, TILE), lambda b, t: (b, t, 0, 0))
    mrow = lambda b, t: jnp.where(t == n_t - 1, n_b // rows, b)
    return pl.pallas_call(
        functools.partial(_per_batch_row(_merge_kernel, 7, 4), n_lat=n_t - 1),
        grid=(n_b // rows, n_q),
        in_specs=_x_specs(rows, n_t - 1, ctx_blk) + [
                  tile_spec(1024), fm_spec(1024), fm_spec(1024), fm_spec(3072),
                  pl.BlockSpec((rows, 1, D_MODEL), lambda b, t: (mrow(b, t), 0, 2)),
                  _resident((3, D_MODEL, D_MODEL)), _resident((D_MODEL, D_MODEL)),
                  _resident((1, D_MODEL)), _resident((1, D_MODEL))],
        out_specs=tile_spec(D_MODEL),
        out_shape=jax.ShapeDtypeStruct((n_b, n_q * TILE, D_MODEL), F32),
        compiler_params=_cparams(("arbitrary", "arbitrary")),
        name="merge",
    )(x_lat, x_ctx, ya, ybT, ycT, gatesT, mod_l, p["wbT"], p["woT"], p["ln1_g"], p["ln1_b"])


def _ffn_kernel(x_ref, sh_ref, sc_ref, g2_ref, wi_ref, wo_ref, lg_ref, lb_ref, o_ref):
    x = x_ref[0]
    h = (_ln_rows(x) * (1.0 + sc_ref[0]) + sh_ref[0]).astype(BF)
    acts = []
    for a, b in FFN_CHUNKS:
        gate = _dot(h, wi_ref[:, a:b])
        up = _dot(h, wi_ref[:, FFN_H + a:FFN_H + b])
        acts.append((_silu(gate) * up).astype(BF))
    yield
    acc = jnp.zeros((TILE, D_MODEL), F32)
    for (a, b), act in zip(FFN_CHUNKS, acts):
        acc = acc + _dot(act, wo_ref[a:b, :])
    y = _ln_rows(DN_ALPHA * x + g2_ref[0] * acc)
    o_ref[0] = y * lg_ref[...] + lb_ref[...]


def _ffn(x1, mod_l, p, n_b, n_t, n_q):
    rows = _batch_rows(n_b)
    tile_spec = lambda c: pl.BlockSpec((rows, TILE, c), lambda b, t: (b, t, 0))
    mrow = lambda b, t: jnp.where(t == n_t - 1, n_b // rows, b)
    mspec = lambda j: pl.BlockSpec((rows, 1, D_MODEL), lambda b, t: (mrow(b, t), 0, j))
    return pl.pallas_call(
        _per_batch_row(_ffn_kernel, 4, 4),
        grid=(n_b // rows, n_q),
        in_specs=[tile_spec(D_MODEL), mspec(3), mspec(4), mspec(5),
                  _resident((D_MODEL, 2 * FFN_H)), _resident((FFN_H, D_MODEL)),
                  _resident((1, D_MODEL)), _resident((1, D_MODEL))],
        out_specs=tile_spec(D_MODEL),
        out_shape=jax.ShapeDtypeStruct((n_b, n_q * TILE, D_MODEL), F32),
        compiler_params=_cparams(("arbitrary", "arbitrary")),
        name="ffn",
    )(x1, mod_l, mod_l, mod_l, p["ffn_wi"], p["ffn_wo"], p["ln2_g"], p["ln2_b"])


def _prep_layer(l, w_in, gla_w_a2, gla_b_a, mla_q_norm_g, mla_kv_norm_g, mla_w_uq, mla_w_ukv, w_branch, w_out,
                ln1_g, ln1_b, ffn_w_in, ffn_w_out, ln2_g, ln2_b):
    w = w_in[l]
    seg = lambda a, b: w[:, a:b]
    w_tok = jnp.concatenate([
        seg(O_GQ, O_GK), seg(O_GK, O_GV), seg(O_GV, O_GR), seg(O_GR, O_GA), seg(O_DK, O_DV),
        seg(O_MQ, O_MKV), seg(O_MKV, O_MKR),
        seg(O_MKR, O_GATES), seg(O_GA, O_DQ), jnp.zeros((D_MODEL, 32), F32)], axis=1).astype(BF)
    w_featT = jnp.concatenate([seg(O_DQ, O_DK) * DIFF_DH ** -0.5, seg(O_DV, O_MQ), seg(O_GATES, O_END)],
                              axis=1).T.astype(BF)
    wa2 = jnp.zeros((128, 1024), F32)
    wa2 = wa2.at[64:80, 0:512].set(gla_w_a2[l, 0]).at[80:96, 512:1024].set(gla_w_a2[l, 1]).astype(BF)
    ukv = mla_w_ukv[l].reshape(MLA_KVR, MLA_H, MLA_NOPE + MLA_DV)
    return dict(
        w_tok=w_tok, w_featT=w_featT, wa2=wa2, ba=gla_b_a[l].reshape(1, 1024),
        qg=mla_q_norm_g[l].reshape(1, MLA_QR), kvg=mla_kv_norm_g[l].reshape(1, MLA_KVR),
        wukv_k=ukv[:, :, :MLA_NOPE].reshape(MLA_KVR, 1024).astype(BF),
        wukv_vT=ukv[:, :, MLA_NOPE:].reshape(MLA_KVR, 1024).T.astype(BF),
        wuqT=mla_w_uq[l].T.astype(BF),
        wbT=jnp.swapaxes(w_branch[l], 1, 2).astype(BF), woT=w_out[l].T.astype(BF),
        ln1_g=ln1_g[l].reshape(1, D_MODEL), ln1_b=ln1_b[l].reshape(1, D_MODEL),
        ffn_wi=ffn_w_in[l].astype(BF), ffn_wo=ffn_w_out[l].astype(BF),
        ln2_g=ln2_g[l].reshape(1, D_MODEL), ln2_b=ln2_b[l].reshape(1, D_MODEL))


def _rope_tables(l_lat, l_ctx):
    rows = l_lat // GRID_W
    pos_row = jnp.broadcast_to(jnp.arange(rows, dtype=F32)[:, None], (rows, GRID_W)).reshape(l_lat)
    pos_col = jnp.broadcast_to(jnp.arange(GRID_W, dtype=F32)[None, :], (rows, GRID_W)).reshape(l_lat)
    d_axis = ROPE_DIM // 2
    inv = ROPE_BASE ** (-jnp.arange(0, d_axis, 2, dtype=F32) / d_axis)
    ang = jnp.concatenate([pos_row[:, None] * inv, pos_col[:, None] * inv], axis=-1)
    cos = jnp.concatenate([jnp.cos(ang), jnp.ones((l_ctx, 32), F32)], axis=0)
    sin = jnp.concatenate([jnp.sin(ang), jnp.zeros((l_ctx, 32), F32)], axis=0)
    ctok = jnp.tile(cos, (1, 4))
    stok = jnp.tile(jnp.concatenate([-sin, sin], axis=1), (1, 2))
    n_t = (l_lat + l_ctx) // TILE
    to_fm = lambda a: a.T.reshape(32, n_t, TILE).transpose(1, 0, 2)
    return ctok, stok, to_fm(cos), to_fm(sin)


def kernel(x, c, ctx, c_ctx, w_mod, b_mod, w_in, gla_w_a2, gla_b_a, gla_norm_g, diff_lam, diff_norm_g,
           mla_q_norm_g, mla_kv_norm_g, mla_w_uq, mla_w_ukv, w_branch, w_out, ln1_g, ln1_b, ffn_w_in, ffn_w_out,
           ln2_g, ln2_b):
    n_b, l_lat, _ = x.shape
    l_ctx = ctx.shape[1]
    assert l_ctx == TILE and n_b + TOK_BATCH <= 16
    assert l_lat % (TILE * max(DIFF_QS, MLA_QS)) == 0 and l_lat % (TILE * KEY_TILES * KEY_UNROLL) == 0
    ltot = l_lat + l_ctx
    n_t = ltot // TILE
    ctok, stok, cosT, sinT = _rope_tables(l_lat, l_ctx)
    c_all = jnp.zeros((16, D_MODEL), F32).at[:n_b].set(c).at[n_b:n_b + TOK_BATCH].set(c_ctx)
    mod = _modulation(c_all, w_mod, b_mod)
    ii = lax.broadcasted_iota(jnp.int32, (TILE, TILE), 0)
    jj = lax.broadcasted_iota(jnp.int32, (TILE, TILE), 1)
    same = (ii // GLA_CHUNK) == (jj // GLA_CHUNK)
    tri_lo = (same & (jj <= ii)).astype(BF)
    tri_up = (same & (jj >= ii)).astype(BF)
    xs = (x, ctx, 0)
    for l in range(N_LAYERS):
        last = l == N_LAYERS - 1
        n_q = n_t - 1 if last else n_t
        lam_init = 0.8 - 0.6 * math.exp(-0.3 * l)
        p = _prep_layer(l, w_in, gla_w_a2, gla_b_a, mla_q_norm_g, mla_kv_norm_g, mla_w_uq, mla_w_ukv, w_branch,
                        w_out, ln1_g, ln1_b, ffn_w_in, ffn_w_out, ln2_g, ln2_b)
        mod_l = mod[l].reshape(16, 1, 6 * D_MODEL)
        gq, gk, gv, gr, dk, la, mk, cq, ckv = _proj_tok(xs, mod_l, ctok, stok, p, n_b, n_t, ltot)
        dqT, dvT, gatesT, mqT, mvT = _proj_feat(xs, mod_l, cosT, sinT, cq, ckv, p, n_b, n_t)
        ya = _gla(gq, gk, gv, la, gr, tri_lo, tri_up, gla_norm_g[l].reshape(1, GLA_DV), n_b, n_t)
        gcol = jnp.broadcast_to(diff_norm_g[l].reshape(128, 1), (128, TILE))
        ybT = _diff_attn(dqT, dk, dvT, diff_lam[l], gcol, n_b, n_t, lam_init)
        ycT = _mla_attn(mqT, mk, mvT, n_b, n_t)
        if not last:
            ybT = _diff_attn(dqT, dk, dvT, diff_lam[l], gcol, n_b, n_t, lam_init, ctx_into=ybT)
            ycT = _mla_attn(mqT, mk, mvT, n_b, n_t, ctx_into=ycT)
        x1 = _merge(xs, ya, ybT, ycT, gatesT, mod_l, p, n_b, n_t, n_q)
        x2 = _ffn(x1, mod_l, p, n_b, n_t, n_q)
        xs = (x2, x2, n_t - 1)
    return x2
```

```python
import functools
import math

import jax
import jax.numpy as jnp
from jax import lax
from jax.experimental import pallas as pl
from jax.experimental.pallas import tpu as pltpu

BF = jnp.bfloat16
F32 = jnp.float32

D_MODEL = 1024
N_LAYERS = 2
GRID_W = 64
TILE = 256
GLA_H, GLA_DK, GLA_DV, GLA_RANK, GLA_TAU = 4, 128, 256, 16, 16.0
GLA_CHUNK, GLA_SUB = 64, 16
GLA_BATCH = 2
TOK_BATCH = 2
DIFF_H, DIFF_DH = 8, 64
MLA_H, MLA_QR, MLA_KVR, MLA_NOPE, MLA_ROPE, MLA_DV = 8, 256, 128, 128, 64, 128
MLA_SCALE = (MLA_NOPE + MLA_ROPE) ** -0.5
ROPE_DIM, ROPE_BASE = 64, 10000.0
FFN_H = 2816
FFN_CHUNKS = ((0, 1536), (1536, 2816))
DN_ALPHA = (2 * N_LAYERS) ** 0.25
EPS = 1e-6
DIFF_QS, MLA_QS = 8, 16
COL_BLOCK = 512
KEY_TILES = 2
KEY_UNROLL = 4
LOG2E = math.log2(math.e)
ONES_ROWS = 16
VMEM_LIMIT = 56 * 1024 * 1024

_SIZES = (512, 512, 1024, 1024, 32, 1024, 1024, 1024, 256, 128, 64, 3072)
_OFF = [0]
for _s in _SIZES:
    _OFF.append(_OFF[-1] + _s)
(O_GQ, O_GK, O_GV, O_GR, O_GA, O_DQ, O_DK, O_DV, O_MQ, O_MKV, O_MKR, O_GATES, O_END) = _OFF

T_GQ, T_GK, T_GV, T_GR, T_DK, T_MQ, T_MKV, T_SMA, T_END = (0, 512, 1024, 2048, 3072, 4096, 4352, 4480, 4608)
F_DQ, F_DV, F_GATES, F_END = 0, 1024, 2048, 5120


def _cparams(sem):
    return pltpu.CompilerParams(dimension_semantics=sem, vmem_limit_bytes=VMEM_LIMIT)


def _resident(shape):
    nd = len(shape)
    return pl.BlockSpec(shape, lambda *_: (0,) * nd, pipeline_mode=pl.Buffered(1))


def _dot(a, b):
    return jnp.dot(a, b, preferred_element_type=F32)


def _dot_nt(a, b):
    return lax.dot_general(a, b, (((1,), (1,)), ((), ())), preferred_element_type=F32)


def _dot_tn(a, b):
    return lax.dot_general(a, b, (((0,), (0,)), ((), ())), preferred_element_type=F32)


def _ln_rows(x):
    mu = jnp.mean(x, axis=-1, keepdims=True)
    xc = x - mu
    var = jnp.mean(xc * xc, axis=-1, keepdims=True)
    return xc * lax.rsqrt(var + EPS)


def _sigmoid(x):
    return 1.0 / (1.0 + jnp.exp(-x))


def _silu(x):
    return x * _sigmoid(x)


def _mod_kernel(c_ref, w_ref, b_ref, o_ref):
    ca = _silu(c_ref[...]).astype(BF)
    o_ref[0] = _dot(ca, w_ref[0].astype(BF)) + b_ref[0]


def _modulation(c_all, w_mod, b_mod):
    nl, d, n6 = w_mod.shape
    r = c_all.shape[0]
    nblk = 1536
    return pl.pallas_call(
        _mod_kernel,
        grid=(nl, n6 // nblk),
        in_specs=[pl.BlockSpec((r, d), lambda l, j: (0, 0)),
                  pl.BlockSpec((1, d, nblk), lambda l, j: (l, 0, j)),
                  pl.BlockSpec((1, 1, nblk), lambda l, j: (l, 0, j))],
        out_specs=pl.BlockSpec((1, r, nblk), lambda l, j: (l, 0, j)),
        out_shape=jax.ShapeDtypeStruct((nl, r, n6), F32),
        compiler_params=_cparams(("arbitrary", "arbitrary")),
        name="modulation",
    )(c_all, w_mod, b_mod.reshape(nl, 1, n6))


def _batch_rows(n_b):
    return TOK_BATCH if n_b % TOK_BATCH == 0 else 1


_DONE = object()


def _per_batch_row(body, n_rowed, n_shared):
    def kern(*refs, **kw):
        rowed, shared, outs = refs[:n_rowed], refs[n_rowed:n_rowed + n_shared], refs[n_rowed + n_shared:]
        pending = []
        for bi in range(rowed[0].shape[0]):
            row = lambda ref, bi=bi: ref.at[pl.ds(bi, 1)]
            pending.append(body(*map(row, rowed), *shared, *map(row, outs), **kw))
        pending = [g for g in pending if g is not None]
        while pending:
            pending = [g for g in pending if next(g, _DONE) is not _DONE]
    return kern


def _x_specs(rows, n_lat, ctx_blk):
    return [pl.BlockSpec((rows, TILE, D_MODEL), lambda b, t: (b, jnp.minimum(t, n_lat - 1), 0)),
            pl.BlockSpec((rows, TILE, D_MODEL), lambda b, t: (b, ctx_blk, 0))]


def _x_tile(x_ref, c_ref, n_lat):
    return jnp.where(pl.program_id(1) == n_lat, c_ref[0], x_ref[0])


def _proj_tok_kernel(x_ref, c_ref, sh_ref, sc_ref, ct_ref, st_ref, w_ref, wa2_ref, ba_ref, qg_ref, kvg_ref, wk_ref,
                     gq_ref, gk_ref, gv_ref, gr_ref, dk_ref, la_ref, mk_ref, cq_ref, ckv_ref, *, n_lat):
    h = (_ln_rows(_x_tile(x_ref, c_ref, n_lat)) * (1.0 + sc_ref[0]) + sh_ref[0]).astype(BF)

    def proj(a, b):
        return _dot(h, w_ref[:, a:b])

    ct = ct_ref[...]
    st = st_ref[...]
    first_half = lax.broadcasted_iota(jnp.int32, (TILE, 128), 1) % ROPE_DIM < ROPE_DIM // 2

    def rope(x):
        partner = jnp.where(first_half, pltpu.roll(x, 128 - ROPE_DIM // 2, 1), pltpu.roll(x, ROPE_DIM // 2, 1))
        return x * ct + partner * st

    mm = proj(T_MKV, T_END)
    mkv = mm[:, 0:128]
    sma = mm[:, 128:256]
    ckv = (mkv * lax.rsqrt(jnp.mean(mkv * mkv, axis=-1, keepdims=True) + EPS) * kvg_ref[...]).astype(BF)
    ckv_ref[0] = ckv
    mq = proj(T_MQ, T_MKV)
    cq = mq * lax.rsqrt(jnp.mean(mq * mq, axis=-1, keepdims=True) + EPS) * qg_ref[...]
    cq_ref[0] = cq.astype(BF)
    gq_ref[0] = (proj(T_GQ, T_GK) * GLA_DK ** -0.5).astype(BF)
    gk_ref[0] = proj(T_GK, T_GV).astype(BF)
    gv_ref[0] = proj(T_GV, T_GR).astype(BF)
    gr_ref[0] = proj(T_GR, T_DK).astype(BF)
    dkx = proj(T_DK, T_MQ)
    for hh in range(DIFF_H):
        sl = slice(128 * hh, 128 * (hh + 1))
        dk_ref[0, :, sl] = rope(dkx[:, sl]).astype(BF)
    yield
    kn = _dot(ckv, wk_ref[...]).astype(BF)
    kr = rope(sma)[:, 0:64].astype(BF)
    for hh in range(MLA_H):
        mk_ref[0, hh, :, 0:128] = kn[:, 128 * hh:128 * (hh + 1)]
        mk_ref[0, hh, :, 128:192] = kr
    z = _dot(sma.astype(BF), wa2_ref[...]) + ba_ref[...]
    la_ref[0] = (jnp.minimum(z, 0.0) - jnp.log(1.0 + jnp.exp(-jnp.abs(z)))) * (1.0 / GLA_TAU)


def _proj_tok(xs, mod_l, ctok, stok, p, n_b, n_t, ltot):
    x_lat, x_ctx, ctx_blk = xs
    rows = _batch_rows(n_b)
    tile_spec = lambda c: pl.BlockSpec((rows, TILE, c), lambda b, t: (b, t, 0))
    mrow = lambda b, t: jnp.where(t == n_t - 1, n_b // rows, b)
    bf_out = lambda c: jax.ShapeDtypeStruct((n_b, ltot, c), BF)
    return pl.pallas_call(
        functools.partial(_per_batch_row(_proj_tok_kernel, 4, 8), n_lat=n_t - 1),
        grid=(n_b // rows, n_t),
        in_specs=_x_specs(rows, n_t - 1, ctx_blk) + [
                  pl.BlockSpec((rows, 1, D_MODEL), lambda b, t: (mrow(b, t), 0, 0)),
                  pl.BlockSpec((rows, 1, D_MODEL), lambda b, t: (mrow(b, t), 0, 1)),
                  pl.BlockSpec((TILE, 128), lambda b, t: (t, 0)),
                  pl.BlockSpec((TILE, 128), lambda b, t: (t, 0)),
                  _resident((D_MODEL, T_END)), _resident((128, 1024)), _resident((1, 1024)),
                  _resident((1, MLA_QR)), _resident((1, MLA_KVR)), _resident((MLA_KVR, 1024))],
        out_specs=[tile_spec(512), tile_spec(512), tile_spec(1024), tile_spec(1024), tile_spec(1024),
                   tile_spec(1024),
                   pl.BlockSpec((rows, MLA_H, TILE, 192), lambda b, t: (b, 0, t, 0)),
                   tile_spec(MLA_QR), tile_spec(MLA_KVR)],
        out_shape=[bf_out(512), bf_out(512), bf_out(1024), bf_out(1024), bf_out(1024),
                   jax.ShapeDtypeStruct((n_b, ltot, 1024), F32),
                   jax.ShapeDtypeStruct((n_b, MLA_H, ltot, 192), BF),
                   bf_out(MLA_QR), bf_out(MLA_KVR)],
        compiler_params=_cparams(("arbitrary", "arbitrary")),
        name="proj_tok",
    )(x_lat, x_ctx, mod_l, mod_l, ctok, stok, p["w_tok"], p["wa2"], p["ba"], p["qg"], p["kvg"], p["wukv_k"])


def _rope_rows(x1, x2, cos, sin):
    return x1 * cos - x2 * sin, x1 * sin + x2 * cos


def _proj_feat_kernel(x_ref, c_ref, sh_ref, sc_ref, cq_ref, ckv_ref, cos_ref, sin_ref, w_ref, wuq_ref, wv_ref,
                      dq_ref, dv_ref, g_ref, mq_ref, mv_ref, *, n_lat):
    h = (_ln_rows(_x_tile(x_ref, c_ref, n_lat)) * (1.0 + sc_ref[0]) + sh_ref[0]).astype(BF)
    cos = cos_ref[0]
    sin = sin_ref[0]
    dq = _dot_nt(w_ref[F_DQ:F_DV, :], h) * LOG2E
    for g in range(2 * DIFF_H):
        o1, o2 = _rope_rows(dq[64 * g:64 * g + 32], dq[64 * g + 32:64 * (g + 1)], cos, sin)
        dq_ref[0, 0, 64 * g:64 * g + 32, :] = o1.astype(BF)
        dq_ref[0, 0, 64 * g + 32:64 * (g + 1), :] = o2.astype(BF)
    dv_ref[0, 0] = _dot_nt(w_ref[F_DV:F_GATES, :], h).astype(BF)
    for j in range(3):
        a = F_GATES + 1024 * j
        g_ref[0, 0, 1024 * j:1024 * (j + 1), :] = _dot_nt(w_ref[a:a + 1024, :], h).astype(BF)
    cq = cq_ref[0]
    mq = _dot_nt(wuq_ref[...], cq) * (MLA_SCALE * LOG2E)
    for hh in range(MLA_H):
        a = 192 * hh
        o1, o2 = _rope_rows(mq[a + 128:a + 160], mq[a + 160:a + 192], cos, sin)
        mq_ref[0, 0, a:a + 128, :] = mq[a:a + 128].astype(BF)
        mq_ref[0, 0, a + 128:a + 160, :] = o1.astype(BF)
        mq_ref[0, 0, a + 160:a + 192, :] = o2.astype(BF)
    mv_ref[0, 0] = _dot_nt(wv_ref[...], ckv_ref[0]).astype(BF)


def _proj_feat(xs, mod_l, cosT, sinT, cq, ckv, p, n_b, n_t):
    x_lat, x_ctx, ctx_blk = xs
    rows = _batch_rows(n_b)
    tile_spec = lambda c: pl.BlockSpec((rows, TILE, c), lambda b, t: (b, t, 0))
    mrow = lambda b, t: jnp.where(t == n_t - 1, n_b // rows, b)
    fm_spec = lambda c: pl.BlockSpec((rows, 1, c, TILE), lambda b, t: (b, t, 0, 0))
    fm_out = lambda c: jax.ShapeDtypeStruct((n_b, n_t, c, TILE), BF)
    return pl.pallas_call(
        functools.partial(_per_batch_row(_proj_feat_kernel, 6, 5), n_lat=n_t - 1),
        grid=(n_b // rows, n_t),
        in_specs=_x_specs(rows, n_t - 1, ctx_blk) + [
                  pl.BlockSpec((rows, 1, D_MODEL), lambda b, t: (mrow(b, t), 0, 0)),
                  pl.BlockSpec((rows, 1, D_MODEL), lambda b, t: (mrow(b, t), 0, 1)),
                  tile_spec(MLA_QR), tile_spec(MLA_KVR),
                  pl.BlockSpec((1, 32, TILE), lambda b, t: (t, 0, 0)),
                  pl.BlockSpec((1, 32, TILE), lambda b, t: (t, 0, 0)),
                  _resident((F_END, D_MODEL)), _resident((MLA_H * 192, MLA_QR)), _resident((1024, MLA_KVR))],
        out_specs=[fm_spec(1024), fm_spec(1024), fm_spec(3072), fm_spec(MLA_H * 192), fm_spec(1024)],
        out_shape=[fm_out(1024), fm_out(1024), fm_out(3072), fm_out(MLA_H * 192), fm_out(1024)],
        compiler_params=_cparams(("arbitrary", "arbitrary")),
        name="proj_feat",
    )(x_lat, x_ctx, mod_l, mod_l, cq, ckv, cosT, sinT, p["w_featT"], p["wuqT"], p["wukv_vT"])


def _flash_T(k_ref, v_ref, q2_ref, m_ref, acc_ref, s_ref, n_lat, latent_keys):
    dv = v_ref.shape[2]
    nq = q2_ref.shape[1]
    cb = min(COL_BLOCK, nq)
    blocks = [slice(c, c + cb) for c in range(0, nq, cb)]
    m_ref[2] = jnp.full(m_ref.shape[1:], -jnp.inf, F32)
    acc_ref[...] = jnp.zeros(acc_ref.shape, F32)
    rows = KEY_TILES * TILE

    def scores(slot, tile0, cs):
        s = _dot(k_ref[0, pl.ds(pl.multiple_of(tile0 * TILE, TILE), rows), :], q2_ref[:, cs])
        s_ref[slot, :, cs] = s
        m_ref[slot, :, cs] = jnp.max(s, axis=0, keepdims=True)

    def consume(s, s_max, tile0, ntiles, cs):
        m_old = m_ref[2, :, cs]
        m_new = jnp.maximum(m_old, s_max)
        alpha = jnp.exp2(m_old - m_new)
        p = jnp.exp2(s - m_new).astype(BF)
        v = jnp.concatenate([v_ref[0, tile0 + i] for i in range(ntiles)], axis=1)
        vext = jnp.concatenate([v, jnp.ones((ONES_ROWS, ntiles * TILE), BF)], axis=0)
        acc_ref[:, cs] = acc_ref[:, cs] * alpha + _dot(vext, p)
        m_ref[2, :, cs] = m_new

    def ctx_step(tile, cs):
        s = _dot(k_ref[0, pl.ds(tile * TILE, TILE), :], q2_ref[:, cs])
        consume(s, jnp.max(s, axis=0, keepdims=True), tile, 1, cs)

    if latent_keys:
        n_steps = n_lat // KEY_TILES
        for cs in blocks:
            scores(0, 0, cs)

        def body(jj, carry):
            for u in range(KEY_UNROLL):
                j = KEY_UNROLL * jj + u
                for cs in blocks:
                    scores((u + 1) % 2, (j + 1) * KEY_TILES, cs)
                    consume(s_ref[u % 2, :, cs], m_ref[u % 2, :, cs], j * KEY_TILES, KEY_TILES, cs)
            return carry
        lax.fori_loop(0, n_steps // KEY_UNROLL - 1, body, 0)
        for u in range(KEY_UNROLL):
            j = n_steps - KEY_UNROLL + u
            for cs in blocks:
                if u < KEY_UNROLL - 1:
                    scores((u + 1) % 2, (j + 1) * KEY_TILES, cs)
                consume(s_ref[u % 2, :, cs], m_ref[u % 2, :, cs], j * KEY_TILES, KEY_TILES, cs)
        for cs in blocks:
            ctx_step(n_lat, cs)
    else:
        for cs in blocks:
            ctx_step(0, cs)
    acc = acc_ref[...]
    return acc[0:dv] * (1.0 / acc[dv:dv + 1])


def _attn_call(kern, name, q_rows, k_spec_fn, n_b, n_heads, n_t, qs, scratch, qT, k, vT, extra, extra_specs,
               ctx_into=None):
    n_lat = n_t - 1
    if ctx_into is None:
        grid = (n_b, n_heads, n_lat // qs)
        q_spec = pl.BlockSpec((1, qs, q_rows, TILE), lambda b, h, q: (b, q, h, 0))
        k_spec = k_spec_fn(n_t * TILE, 0)
        v_spec = pl.BlockSpec((1, n_t, 128, TILE), lambda b, h, q: (b, 0, h, 0))
        o_spec = pl.BlockSpec((1, qs, 128, TILE), lambda b, h, q: (b, q, h, 0))
        alias_in, alias_specs, aliases = [], [], {}
    else:
        grid = (n_b, n_heads, 1)
        q_spec = pl.BlockSpec((1, 1, q_rows, TILE), lambda b, h, q: (b, n_lat, h, 0))
        k_spec = k_spec_fn(TILE, n_lat)
        v_spec = pl.BlockSpec((1, 1, 128, TILE), lambda b, h, q: (b, n_lat, h, 0))
        o_spec = pl.BlockSpec((1, 1, 128, TILE), lambda b, h, q: (b, n_lat, h, 0))
        alias_in, alias_specs = [ctx_into], [pl.BlockSpec(memory_space=pl.ANY)]
        aliases = {3 + len(extra): 0}
    return pl.pallas_call(
        kern,
        grid=grid,
        in_specs=[q_spec, k_spec, v_spec] + extra_specs + alias_specs,
        out_specs=o_spec,
        out_shape=jax.ShapeDtypeStruct((n_b, n_t, 1024, TILE), BF),
        scratch_shapes=scratch,
        input_output_aliases=aliases,
        compiler_params=_cparams(("arbitrary", "arbitrary", "arbitrary")),
        name=name,
    )(qT, k, vT, *extra, *alias_in)


def _diff_attn_kernel(q_ref, k_ref, v_ref, lam_ref, g_ref, *rest, n_lat, lam_init, qs, latent):
    o_ref, q2_ref, m_ref, acc_ref, s_ref = rest[-5:]
    nq = qs * TILE
    zeros = jnp.zeros((DIFF_DH, TILE), BF)
    for i in range(qs):
        q = q_ref[0, i]
        q2_ref[0:64, i * TILE:(i + 1) * TILE] = q[0:64]
        q2_ref[64:128, i * TILE:(i + 1) * TILE] = zeros
        q2_ref[0:64, nq + i * TILE:nq + (i + 1) * TILE] = zeros
        q2_ref[64:128, nq + i * TILE:nq + (i + 1) * TILE] = q[64:128]
    o = _flash_T(k_ref, v_ref, q2_ref, m_ref, acc_ref, s_ref, n_lat, latent)
    dl = lam_ref[...]
    lam = (jnp.exp(jnp.sum(dl[0:1] * dl[1:2], axis=1, keepdims=True))
           - jnp.exp(jnp.sum(dl[2:3] * dl[3:4], axis=1, keepdims=True)) + lam_init)
    od = o[:, 0:nq] - lam * o[:, nq:2 * nq]
    y = od * lax.rsqrt(jnp.mean(od * od, axis=0, keepdims=True) + EPS) * (1.0 - lam_init)
    for i in range(qs):
        o_ref[0, i] = (y[:, i * TILE:(i + 1) * TILE] * g_ref[...]).astype(BF)


def _diff_attn(dqT, dk, dvT, dlam, gcol, n_b, n_t, lam_init, ctx_into=None):
    qs = DIFF_QS if ctx_into is None else 1
    kern = functools.partial(_diff_attn_kernel, n_lat=n_t - 1, lam_init=lam_init, qs=qs, latent=ctx_into is None)
    k_spec_fn = lambda rows, blk: pl.BlockSpec((1, rows, 128), lambda b, h, q: (b, blk, h))
    scratch = [pltpu.VMEM((128, 2 * qs * TILE), BF), pltpu.VMEM((3, 1, 2 * qs * TILE), F32),
               pltpu.VMEM((128 + ONES_ROWS, 2 * qs * TILE), F32),
               pltpu.VMEM((2, KEY_TILES * TILE, 2 * qs * TILE), F32)]
    extra_specs = [pl.BlockSpec((4, DIFF_DH), lambda b, h, q: (0, 0)),
                   pl.BlockSpec((128, TILE), lambda b, h, q: (0, 0))]
    return _attn_call(kern, "diff_attn" if ctx_into is None else "diff_attn_ctx", 128, k_spec_fn, n_b, DIFF_H, n_t,
                      qs, scratch, dqT, dk, dvT, [dlam, gcol], extra_specs, ctx_into)


def _mla_attn_kernel(q_ref, k_ref, v_ref, *rest, n_lat, qs, latent):
    o_ref, q2_ref, m_ref, acc_ref, s_ref = rest[-5:]
    for i in range(qs):
        q2_ref[:, i * TILE:(i + 1) * TILE] = q_ref[0, i]
    o = _flash_T(k_ref.at[0], v_ref, q2_ref, m_ref, acc_ref, s_ref, n_lat, latent)
    for i in range(qs):
        o_ref[0, i] = o[:, i * TILE:(i + 1) * TILE].astype(BF)


def _mla_attn(mqT, mk, mvT, n_b, n_t, ctx_into=None):
    qs = MLA_QS if ctx_into is None else 1
    kern = functools.partial(_mla_attn_kernel, n_lat=n_t - 1, qs=qs, latent=ctx_into is None)
    k_spec_fn = lambda rows, blk: pl.BlockSpec((1, 1, rows, 192), lambda b, h, q: (b, h, blk, 0))
    scratch = [pltpu.VMEM((192, qs * TILE), BF), pltpu.VMEM((3, 1, qs * TILE), F32),
               pltpu.VMEM((128 + ONES_ROWS, qs * TILE), F32),
               pltpu.VMEM((2, KEY_TILES * TILE, qs * TILE), F32)]
    return _attn_call(kern, "mla_attn" if ctx_into is None else "mla_attn_ctx", 192, k_spec_fn, n_b, MLA_H, n_t,
                      qs, scratch, mqT, mk, mvT, [], [], ctx_into)


def _gla_chunk_order(reverse):
    n_chunk = TILE // GLA_CHUNK
    return range(n_chunk - 1, -1, -1) if reverse else range(n_chunk)


def _gla_tile(q_ref, k_ref, v_ref, la_ref, tri_ref, bi, reverse, work):
    g = la_ref[bi]
    g1 = g.astype(BF)
    r1 = g - g1.astype(F32)
    g2 = r1.astype(BF)
    tri = tri_ref[...]
    bcum = _dot(tri, g1) + _dot(tri, g2)
    n_sub = GLA_CHUNK // GLA_SUB
    r16 = lax.broadcasted_iota(jnp.int32, (GLA_SUB, GLA_CHUNK), 0)
    c16 = lax.broadcasted_iota(jnp.int32, (GLA_SUB, GLA_CHUNK), 1)
    for hh in range(GLA_H):
        kc0, kc1 = GLA_DK * hh, GLA_DK * (hh + 1)
        for c in _gla_chunk_order(reverse):
            r0 = GLA_CHUNK * c
            bc = bcum[r0:r0 + GLA_CHUNK, kc0:kc1]
            qc = q_ref[bi, r0:r0 + GLA_CHUNK, kc0:kc1].astype(F32)
            kc = k_ref[bi, r0:r0 + GLA_CHUNK, kc0:kc1].astype(F32)
            vc = v_ref[bi, r0:r0 + GLA_CHUNK, GLA_DV * hh:GLA_DV * (hh + 1)]
            b_tot = bc[0:1] if reverse else bc[GLA_CHUNK - 1:GLA_CHUNK]
            a_rows = []
            for i in range(n_sub):
                s0 = GLA_SUB * i
                ref_row = bc[s0 + GLA_SUB - 1:s0 + GLA_SUB] if reverse else bc[s0:s0 + 1]
                qs = (qc[s0:s0 + GLA_SUB] * jnp.exp(bc[s0:s0 + GLA_SUB] - ref_row)).astype(BF)
                lo, hi = (s0, GLA_CHUNK) if reverse else (0, s0 + GLA_SUB)
                ks = (kc[lo:hi] * jnp.exp(ref_row - bc[lo:hi])).astype(BF)
                pad = jnp.zeros((GLA_CHUNK - (hi - lo), GLA_DK), BF)
                ks = jnp.concatenate([pad, ks] if reverse else [ks, pad], axis=0) if hi - lo < GLA_CHUNK else ks
                a = _dot_nt(qs, ks)
                keep = (c16 > r16 + s0) if reverse else (c16 <= r16 + s0)
                a_rows.append(jnp.where(keep, a, 0.0))
            q_in = (qc * jnp.exp(bc)).astype(BF)
            k_d = (kc * jnp.exp(b_tot - bc)).astype(BF)
            work[hh, c] = (a_rows, q_in, vc, jnp.exp(b_tot), _dot_tn(vc, k_d))
            yield


def _gla_finish(work, reverse, st_ref, bi, outs):
    heads = []
    for hh in range(GLA_H):
        o_chunks = [None] * (TILE // GLA_CHUNK)
        st = st_ref[bi, hh]
        for c in _gla_chunk_order(reverse):
            a_rows, q_in, vc, decay, st_inc = work[hh, c]
            a_mat = jnp.concatenate(a_rows, axis=0).astype(BF)
            o_chunks[c] = _dot(a_mat, vc) + _dot_nt(q_in, st.astype(BF))
            st = st * decay + st_inc
            yield
        st_ref[bi, hh] = st
        heads.append(jnp.concatenate(o_chunks, axis=0))
    outs[bi] = jnp.concatenate(heads, axis=1)


def _gla_rows(q_ref, k_ref, v_ref, la_ref, tri_ref, st_ref, reverse):
    n = q_ref.shape[0]
    works = [{} for _ in range(n)]
    outs = [None] * n
    free = [_gla_tile(q_ref, k_ref, v_ref, la_ref, tri_ref, bi, reverse, works[bi]) for bi in range(n)]
    dep = [_gla_finish(works[bi], reverse, st_ref, bi, outs) for bi in range(n)]
    for _ in free[0]:
        pass
    for bi in range(1, n):
        pending = [free[bi], dep[bi - 1]]
        while pending:
            pending = [g for g in pending if next(g, _DONE) is not _DONE]
    for _ in dep[n - 1]:
        pass
    return outs


def _gla_fwd_kernel(q_ref, k_ref, v_ref, la_ref, tri_ref, o_ref, st_ref):
    @pl.when(pl.program_id(1) == 0)
    def _():
        st_ref[...] = jnp.zeros(st_ref.shape, F32)
    for bi, o in enumerate(_gla_rows(q_ref, k_ref, v_ref, la_ref, tri_ref, st_ref, False)):
        o_ref[bi] = o


def _gla_bwd_kernel(q_ref, k_ref, v_ref, la_ref, tri_ref, of_ref, r_ref, g_ref, y_ref, st_ref):
    @pl.when(pl.program_id(1) == 0)
    def _():
        st_ref[...] = jnp.zeros(st_ref.shape, F32)
    gn = g_ref[...]
    for bi, o_b in enumerate(_gla_rows(q_ref, k_ref, v_ref, la_ref, tri_ref, st_ref, True)):
        o = of_ref[bi] + o_b
        for hh in range(GLA_H):
            sl = slice(GLA_DV * hh, GLA_DV * (hh + 1))
            oh = o[:, sl]
            yh = oh * lax.rsqrt(jnp.mean(oh * oh, axis=-1, keepdims=True) + EPS) * gn
            y_ref[bi, :, sl] = (yh * _silu(r_ref[bi, :, sl].astype(F32))).astype(BF)


def _gla(gq, gk, gv, la, gr, tri_lo, tri_up, gnorm, n_b, n_t):
    ltot = n_t * TILE
    rows = GLA_BATCH if n_b % GLA_BATCH == 0 else 1
    t_fwd = lambda s: (s + n_t - 1) % n_t
    t_bwd = lambda s: jnp.where(s == 0, n_t - 1, n_t - 1 - s)

    def specs(tmap, dirn):
        ts = lambda c, cb=0: pl.BlockSpec((rows, TILE, c), lambda b, s: (b, tmap(s), cb))
        return [ts(512), ts(512), ts(1024), ts(512, dirn), pl.BlockSpec((TILE, TILE), lambda b, s: (0, 0))]

    o_f = pl.pallas_call(
        _gla_fwd_kernel,
        grid=(n_b // rows, n_t),
        in_specs=specs(t_fwd, 0),
        out_specs=pl.BlockSpec((rows, TILE, 1024), lambda b, s: (b, t_fwd(s), 0)),
        out_shape=jax.ShapeDtypeStruct((n_b, ltot, 1024), F32),
        scratch_shapes=[pltpu.VMEM((rows, GLA_H, GLA_DV, GLA_DK), F32)],
        compiler_params=_cparams(("arbitrary", "arbitrary")),
        name="gla_fwd",
    )(gq, gk, gv, la, tri_lo)
    tsb = lambda c: pl.BlockSpec((rows, TILE, c), lambda b, s: (b, t_bwd(s), 0))
    return pl.pallas_call(
        _gla_bwd_kernel,
        grid=(n_b // rows, n_t),
        in_specs=specs(t_bwd, 1) + [tsb(1024), tsb(1024), pl.BlockSpec((1, GLA_DV), lambda b, s: (0, 0))],
        out_specs=tsb(1024),
        out_shape=jax.ShapeDtypeStruct((n_b, ltot, 1024), BF),
        scratch_shapes=[pltpu.VMEM((rows, GLA_H, GLA_DV, GLA_DK), F32)],
        compiler_params=_cparams(("arbitrary", "arbitrary")),
        name="gla_bwd",
    )(gq, gk, gv, la, tri_up, o_f, gr, gnorm)


def _merge_kernel(x_ref, c_ref, ya_ref, yb_ref, yc_ref, g_ref, g1_ref, wb_ref, wo_ref, lg_ref, lb_ref, o_ref, *,
                  n_lat):
    zt = (_sigmoid(g_ref[0, 0, 0:1024, :].astype(F32)) * _dot_nt(wb_ref[0], ya_ref[0])
          + _sigmoid(g_ref[0, 0, 1024:2048, :].astype(F32)) * _dot(wb_ref[1], yb_ref[0, 0])
          + _sigmoid(g_ref[0, 0, 2048:3072, :].astype(F32)) * _dot(wb_ref[2], yc_ref[0, 0]))
    yield
    u = _dot(wo_ref[...], zt.astype(BF)).T
    y = _ln_rows(DN_ALPHA * _x_tile(x_ref, c_ref, n_lat) + g1_ref[0] * u)
    o_ref[0] = y * lg_ref[...] + lb_ref[...]


def _merge(xs, ya, ybT, ycT, gatesT, mod_l, p, n_b, n_t, n_q):
    x_lat, x_ctx, ctx_blk = xs
    rows = _batch_rows(n_b)
    tile_spec = lambda c: pl.BlockSpec((rows, TILE, c), lambda b, t: (b, t, 0))
    fm_spec = lambda c: pl.BlockSpec((rows, 1, c, TILE), lambda b, t: (b, t, 0, 0))
    mrow = lambda b, t: jnp.where(t == n_t - 1, n_b // rows, b)
    return pl.pallas_call(
        functools.partial(_per_batch_row(_merge_kernel, 7, 4), n_lat=n_t - 1),
        grid=(n_b // rows, n_q),
        in_specs=_x_specs(rows, n_t - 1, ctx_blk) + [
                  tile_spec(1024), fm_spec(1024), fm_spec(1024), fm_spec(3072),
                  pl.BlockSpec((rows, 1, D_MODEL), lambda b, t: (mrow(b, t), 0, 2)),
                  _resident((3, D_MODEL, D_MODEL)), _resident((D_MODEL, D_MODEL)),
                  _resident((1, D_MODEL)), _resident((1, D_MODEL))],
        out_specs=tile_spec(D_MODEL),
        out_shape=jax.ShapeDtypeStruct((n_b, n_q * TILE, D_MODEL), F32),
        compiler_params=_cparams(("arbitrary", "arbitrary")),
        name="merge",
    )(x_lat, x_ctx, ya, ybT, ycT, gatesT, mod_l, p["wbT"], p["woT"], p["ln1_g"], p["ln1_b"])


def _ffn_kernel(x_ref, sh_ref, sc_ref, g2_ref, wi_ref, wo_ref, lg_ref, lb_ref, o_ref):
    x = x_ref[0]
    h = (_ln_rows(x) * (1.0 + sc_ref[0]) + sh_ref[0]).astype(BF)
    acts = []
    for a, b in FFN_CHUNKS:
        gate = _dot(h, wi_ref[:, a:b])
        up = _dot(h, wi_ref[:, FFN_H + a:FFN_H + b])
        acts.append((_silu(gate) * up).astype(BF))
    yield
    acc = jnp.zeros((TILE, D_MODEL), F32)
    for (a, b), act in zip(FFN_CHUNKS, acts):
        acc = acc + _dot(act, wo_ref[a:b, :])
    y = _ln_rows(DN_ALPHA * x + g2_ref[0] * acc)
    o_ref[0] = y * lg_ref[...] + lb_ref[...]


def _ffn(x1, mod_l, p, n_b, n_t, n_q):
    rows = _batch_rows(n_b)
    tile_spec = lambda c: pl.BlockSpec((rows, TILE, c), lambda b, t: (b, t, 0))
    mrow = lambda b, t: jnp.where(t == n_t - 1, n_b // rows, b)
    mspec = lambda j: pl.BlockSpec((rows, 1, D_MODEL), lambda b, t: (mrow(b, t), 0, j))
    return pl.pallas_call(
        _per_batch_row(_ffn_kernel, 4, 4),
        grid=(n_b // rows, n_q),
        in_specs=[tile_spec(D_MODEL), mspec(3), mspec(4), mspec(5),
                  _resident((D_MODEL, 2 * FFN_H)), _resident((FFN_H, D_MODEL)),
                  _resident((1, D_MODEL)), _resident((1, D_MODEL))],
        out_specs=tile_spec(D_MODEL),
        out_shape=jax.ShapeDtypeStruct((n_b, n_q * TILE, D_MODEL), F32),
        compiler_params=_cparams(("arbitrary", "arbitrary")),
        name="ffn",
    )(x1, mod_l, mod_l, mod_l, p["ffn_wi"], p["ffn_wo"], p["ln2_g"], p["ln2_b"])


def _prep_layer(l, w_in, gla_w_a2, gla_b_a, mla_q_norm_g, mla_kv_norm_g, mla_w_uq, mla_w_ukv, w_branch, w_out,
                ln1_g, ln1_b, ffn_w_in, ffn_w_out, ln2_g, ln2_b):
    w = w_in[l]
    seg = lambda a, b: w[:, a:b]
    w_tok = jnp.concatenate([
        seg(O_GQ, O_GK), seg(O_GK, O_GV), seg(O_GV, O_GR), seg(O_GR, O_GA), seg(O_DK, O_DV),
        seg(O_MQ, O_MKV), seg(O_MKV, O_MKR),
        seg(O_MKR, O_GATES), seg(O_GA, O_DQ), jnp.zeros((D_MODEL, 32), F32)], axis=1).astype(BF)
    w_featT = jnp.concatenate([seg(O_DQ, O_DK) * DIFF_DH ** -0.5, seg(O_DV, O_MQ), seg(O_GATES, O_END)],
                              axis=1).T.astype(BF)
    wa2 = jnp.zeros((128, 1024), F32)
    wa2 = wa2.at[64:80, 0:512].set(gla_w_a2[l, 0]).at[80:96, 512:1024].set(gla_w_a2[l, 1]).astype(BF)
    ukv = mla_w_ukv[l].reshape(MLA_KVR, MLA_H, MLA_NOPE + MLA_DV)
    return dict(
        w_tok=w_tok, w_featT=w_featT, wa2=wa2, ba=gla_b_a[l].reshape(1, 1024),
        qg=mla_q_norm_g[l].reshape(1, MLA_QR), kvg=mla_kv_norm_g[l].reshape(1, MLA_KVR),
        wukv_k=ukv[:, :, :MLA_NOPE].reshape(MLA_KVR, 1024).astype(BF),
        wukv_vT=ukv[:, :, MLA_NOPE:].reshape(MLA_KVR, 1024).T.astype(BF),
        wuqT=mla_w_uq[l].T.astype(BF),
        wbT=jnp.swapaxes(w_branch[l], 1, 2).astype(BF), woT=w_out[l].T.astype(BF),
        ln1_g=ln1_g[l].reshape(1, D_MODEL), ln1_b=ln1_b[l].reshape(1, D_MODEL),
        ffn_wi=ffn_w_in[l].astype(BF), ffn_wo=ffn_w_out[l].astype(BF),
        ln2_g=ln2_g[l].reshape(1, D_MODEL), ln2_b=ln2_b[l].reshape(1, D_MODEL))


def _rope_tables(l_lat, l_ctx):
    rows = l_lat // GRID_W
    pos_row = jnp.broadcast_to(jnp.arange(rows, dtype=F32)[:, None], (rows, GRID_W)).reshape(l_lat)
    pos_col = jnp.broadcast_to(jnp.arange(GRID_W, dtype=F32)[None, :], (rows, GRID_W)).reshape(l_lat)
    d_axis = ROPE_DIM // 2
    inv = ROPE_BASE ** (-jnp.arange(0, d_axis, 2, dtype=F32) / d_axis)
    ang = jnp.concatenate([pos_row[:, None] * inv, pos_col[:, None] * inv], axis=-1)
    cos = jnp.concatenate([jnp.cos(ang), jnp.ones((l_ctx, 32), F32)], axis=0)
    sin = jnp.concatenate([jnp.sin(ang), jnp.zeros((l_ctx, 32), F32)], axis=0)
    ctok = jnp.tile(cos, (1, 4))
    stok = jnp.tile(jnp.concatenate([-sin, sin], axis=1), (1, 2))
    n_t = (l_lat + l_ctx) // TILE
    to_fm = lambda a: a.T.reshape(32, n_t, TILE).transpose(1, 0, 2)
    return ctok, stok, to_fm(cos), to_fm(sin)


def kernel(x, c, ctx, c_ctx, w_mod, b_mod, w_in, gla_w_a2, gla_b_a, gla_norm_g, diff_lam, diff_norm_g,
           mla_q_norm_g, mla_kv_norm_g, mla_w_uq, mla_w_ukv, w_branch, w_out, ln1_g, ln1_b, ffn_w_in, ffn_w_out,
           ln2_g, ln2_b):
    n_b, l_lat, _ = x.shape
    l_ctx = ctx.shape[1]
    assert l_ctx == TILE and n_b + TOK_BATCH <= 16
    assert l_lat % (TILE * max(DIFF_QS, MLA_QS)) == 0 and l_lat % (TILE * KEY_TILES * KEY_UNROLL) == 0
    ltot = l_lat + l_ctx
    n_t = ltot // TILE
    ctok, stok, cosT, sinT = _rope_tables(l_lat, l_ctx)
    c_all = jnp.zeros((16, D_MODEL), F32).at[:n_b].set(c).at[n_b:n_b + TOK_BATCH].set(c_ctx)
    mod = _modulation(c_all, w_mod, b_mod)
    ii = lax.broadcasted_iota(jnp.int32, (TILE, TILE), 0)
    jj = lax.broadcasted_iota(jnp.int32, (TILE, TILE), 1)
    same = (ii // GLA_CHUNK) == (jj // GLA_CHUNK)
    tri_lo = (same & (jj <= ii)).astype(BF)
    tri_up = (same & (jj >= ii)).astype(BF)
    xs = (x, ctx, 0)
    for l in range(N_LAYERS):
        last = l == N_LAYERS - 1
        n_q = n_t - 1 if last else n_t
        lam_init = 0.8 - 0.6 * math.exp(-0.3 * l)
        p = _prep_layer(l, w_in, gla_w_a2, gla_b_a, mla_q_norm_g, mla_kv_norm_g, mla_w_uq, mla_w_ukv, w_branch,
                        w_out, ln1_g, ln1_b, ffn_w_in, ffn_w_out, ln2_g, ln2_b)
        mod_l = mod[l].reshape(16, 1, 6 * D_MODEL)
        gq, gk, gv, gr, dk, la, mk, cq, ckv = _proj_tok(xs, mod_l, ctok, stok, p, n_b, n_t, ltot)
        dqT, dvT, gatesT, mqT, mvT = _proj_feat(xs, mod_l, cosT, sinT, cq, ckv, p, n_b, n_t)
        ya = _gla(gq, gk, gv, la, gr, tri_lo, tri_up, gla_norm_g[l].reshape(1, GLA_DV), n_b, n_t)
        gcol = jnp.broadcast_to(diff_norm_g[l].reshape(128, 1), (128, TILE))
        ybT = _diff_attn(dqT, dk, dvT, diff_lam[l], gcol, n_b, n_t, lam_init)
        ycT = _mla_attn(mqT, mk, mvT, n_b, n_t)
        if not last:
            ybT = _diff_attn(dqT, dk, dvT, diff_lam[l], gcol, n_b, n_t, lam_init, ctx_into=ybT)
            ycT = _mla_attn(mqT, mk, mvT, n_b, n_t, ctx_into=ycT)
        x1 = _merge(xs, ya, ybT, ycT, gatesT, mod_l, p, n_b, n_t, n_q)
        x2 = _ffn(x1, mod_l, p, n_b, n_t, n_q)
        xs = (x2, x2, n_t - 1)
    return x2
```

```python
import functools
import math

import jax
import jax.numpy as jnp
from jax import lax
from jax.experimental import pallas as pl
from jax.experimental.pallas import tpu as pltpu

BF = jnp.bfloat16
F32 = jnp.float32

D_MODEL = 1024
N_LAYERS = 2
GRID_W = 64
TILE = 256
GLA_H, GLA_DK, GLA_DV, GLA_TAU = 4, 128, 256, 16.0
GLA_CHUNK, GLA_SUB = 64, 16
GLA_BATCH = 4
TOK_BATCH = 2
DIFF_H, DIFF_DH = 8, 64
MLA_H, MLA_QR, MLA_KVR, MLA_NOPE, MLA_ROPE, MLA_DV = 8, 256, 128, 128, 64, 128
MLA_QK = MLA_NOPE + MLA_ROPE
MLA_SCALE = MLA_QK ** -0.5
ROPE_DIM, ROPE_BASE = 64, 10000.0
FFN_H = 2816
FFN_CHUNKS = ((0, 1536), (1536, 2816))
DN_ALPHA = (2 * N_LAYERS) ** 0.25
EPS = 1e-6
DIFF_QS, MLA_QS = 8, 16
COL_BLOCK = 512
KEY_TILES = 2
KEY_UNROLL = 4
LOG2E = math.log2(math.e)
ONES_ROWS = 16
VMEM_LIMIT = 56 * 1024 * 1024

_SIZES = (512, 512, 1024, 1024, 32, 1024, 1024, 1024, 256, 128, 64, 3072)
_OFF = [0]
for _s in _SIZES:
    _OFF.append(_OFF[-1] + _s)
(O_GQ, O_GK, O_GV, O_GR, O_GA, O_DQ, O_DK, O_DV, O_MQ, O_MKV, O_MKR, O_GATES, O_END) = _OFF

T_GQ, T_GK, T_GV, T_GR, T_DK, T_MQ, T_MKV, T_SMA, T_END = (0, 512, 1024, 2048, 3072, 4096, 4352, 4480, 4608)
F_DQ, F_DV, F_GATES, F_END = 0, 1024, 2048, 5120


def _cparams(sem):
    return pltpu.CompilerParams(dimension_semantics=sem, vmem_limit_bytes=VMEM_LIMIT)


def _resident(shape):
    nd = len(shape)
    return pl.BlockSpec(shape, lambda *_: (0,) * nd, pipeline_mode=pl.Buffered(1))


def _dot(a, b):
    return jnp.dot(a, b, preferred_element_type=F32)


def _dot_nt(a, b):
    return lax.dot_general(a, b, (((1,), (1,)), ((), ())), preferred_element_type=F32)


def _dot_tn(a, b):
    return lax.dot_general(a, b, (((0,), (0,)), ((), ())), preferred_element_type=F32)


def _ln_rows(x):
    mu = jnp.mean(x, axis=-1, keepdims=True)
    xc = x - mu
    var = jnp.mean(xc * xc, axis=-1, keepdims=True)
    return xc * lax.rsqrt(var + EPS)


def _sigmoid(x):
    return 1.0 / (1.0 + jnp.exp(-x))


def _silu(x):
    return x * _sigmoid(x)


def _mod_kernel(c_ref, w_ref, b_ref, o_ref):
    ca = _silu(c_ref[...]).astype(BF)
    o_ref[0] = _dot(ca, w_ref[0].astype(BF)) + b_ref[0]


def _modulation(c_all, w_mod, b_mod):
    nl, d, n6 = w_mod.shape
    r = c_all.shape[0]
    nblk = 1536
    return pl.pallas_call(
        _mod_kernel,
        grid=(nl, n6 // nblk),
        in_specs=[pl.BlockSpec((r, d), lambda l, j: (0, 0)),
                  pl.BlockSpec((1, d, nblk), lambda l, j: (l, 0, j)),
                  pl.BlockSpec((1, 1, nblk), lambda l, j: (l, 0, j))],
        out_specs=pl.BlockSpec((1, r, nblk), lambda l, j: (l, 0, j)),
        out_shape=jax.ShapeDtypeStruct((nl, r, n6), F32),
        compiler_params=_cparams(("arbitrary", "arbitrary")),
        name="modulation",
    )(c_all, w_mod, b_mod.reshape(nl, 1, n6))


def _batch_rows(n_b):
    return TOK_BATCH if n_b % TOK_BATCH == 0 else 1


_DONE = object()


def _per_batch_row(body, n_rowed, n_shared):
    def kern(*refs, **kw):
        rowed, shared, outs = refs[:n_rowed], refs[n_rowed:n_rowed + n_shared], refs[n_rowed + n_shared:]
        pending = []
        for bi in range(rowed[0].shape[0]):
            row = lambda ref, bi=bi: ref.at[pl.ds(bi, 1)]
            pending.append(body(*map(row, rowed), *shared, *map(row, outs), **kw))
        pending = [g for g in pending if g is not None]
        while pending:
            pending = [g for g in pending if next(g, _DONE) is not _DONE]
    return kern


def _x_specs(rows, n_lat, ctx_blk):
    return [pl.BlockSpec((rows, TILE, D_MODEL), lambda b, t: (b, jnp.minimum(t, n_lat - 1), 0)),
            pl.BlockSpec((rows, TILE, D_MODEL), lambda b, t: (b, ctx_blk, 0))]


def _x_tile(x_ref, c_ref, n_lat):
    return jnp.where(pl.program_id(1) == n_lat, c_ref[0], x_ref[0])


def _proj_tok_kernel(x_ref, c_ref, sh_ref, sc_ref, ct_ref, st_ref, w_ref, wa2_ref, ba_ref, qg_ref, kvg_ref, wk_ref,
                     gq_ref, gk_ref, gv_ref, gr_ref, dk_ref, la_ref, mk_ref, cq_ref, ckv_ref, *, n_lat):
    h = (_ln_rows(_x_tile(x_ref, c_ref, n_lat)) * (1.0 + sc_ref[0]) + sh_ref[0]).astype(BF)

    def proj(a, b):
        return _dot(h, w_ref[:, a:b])

    ct = ct_ref[...]
    st = st_ref[...]
    first_half = lax.broadcasted_iota(jnp.int32, (TILE, 128), 1) % ROPE_DIM < ROPE_DIM // 2

    def rope(x):
        partner = jnp.where(first_half, pltpu.roll(x, 128 - ROPE_DIM // 2, 1), pltpu.roll(x, ROPE_DIM // 2, 1))
        return x * ct + partner * st

    mm = proj(T_MKV, T_END)
    mkv = mm[:, 0:128]
    sma = mm[:, 128:256]
    ckv = (mkv * lax.rsqrt(jnp.mean(mkv * mkv, axis=-1, keepdims=True) + EPS) * kvg_ref[...]).astype(BF)
    ckv_ref[0] = ckv
    mq = proj(T_MQ, T_MKV)
    cq = mq * lax.rsqrt(jnp.mean(mq * mq, axis=-1, keepdims=True) + EPS) * qg_ref[...]
    cq_ref[0] = cq.astype(BF)
    gq_ref[0] = (proj(T_GQ, T_GK) * GLA_DK ** -0.5).astype(BF)
    gk_ref[0] = proj(T_GK, T_GV).astype(BF)
    gv_ref[0] = proj(T_GV, T_GR).astype(BF)
    gr_ref[0] = proj(T_GR, T_DK).astype(BF)
    dkx = proj(T_DK, T_MQ)
    for hh in range(DIFF_H):
        sl = slice(128 * hh, 128 * (hh + 1))
        dk_ref[0, :, sl] = rope(dkx[:, sl]).astype(BF)
    yield
    kn = _dot(ckv, wk_ref[...]).astype(BF)
    kr = rope(sma)[:, 0:64].astype(BF)
    for hh in range(MLA_H):
        mk_ref[0, hh, :, 0:128] = kn[:, 128 * hh:128 * (hh + 1)]
        mk_ref[0, hh, :, 128:192] = kr
    z = _dot(sma.astype(BF), wa2_ref[...]) + ba_ref[...]
    la_ref[0] = (jnp.minimum(z, 0.0) - jnp.log(1.0 + jnp.exp(-jnp.abs(z)))) * (1.0 / GLA_TAU)


def _proj_tok(xs, mod_l, ctok, stok, p, n_b, n_t, ltot):
    x_lat, x_ctx, ctx_blk = xs
    rows = _batch_rows(n_b)
    tile_spec = lambda c: pl.BlockSpec((rows, TILE, c), lambda b, t: (b, t, 0))
    mrow = lambda b, t: jnp.where(t == n_t - 1, n_b // rows, b)
    bf_out = lambda c: jax.ShapeDtypeStruct((n_b, ltot, c), BF)
    return pl.pallas_call(
        functools.partial(_per_batch_row(_proj_tok_kernel, 4, 8), n_lat=n_t - 1),
        grid=(n_b // rows, n_t),
        in_specs=_x_specs(rows, n_t - 1, ctx_blk) + [
                  pl.BlockSpec((rows, 1, D_MODEL), lambda b, t: (mrow(b, t), 0, 0)),
                  pl.BlockSpec((rows, 1, D_MODEL), lambda b, t: (mrow(b, t), 0, 1)),
                  pl.BlockSpec((TILE, 128), lambda b, t: (t, 0)),
                  pl.BlockSpec((TILE, 128), lambda b, t: (t, 0)),
                  _resident((D_MODEL, T_END)), _resident((128, 1024)), _resident((1, 1024)),
                  _resident((1, MLA_QR)), _resident((1, MLA_KVR)), _resident((MLA_KVR, 1024))],
        out_specs=[tile_spec(512), tile_spec(512), tile_spec(1024), tile_spec(1024), tile_spec(1024),
                   tile_spec(1024),
                   pl.BlockSpec((rows, MLA_H, TILE, MLA_QK), lambda b, t: (b, 0, t, 0)),
                   tile_spec(MLA_QR), tile_spec(MLA_KVR)],
        out_shape=[bf_out(512), bf_out(512), bf_out(1024), bf_out(1024), bf_out(1024),
                   jax.ShapeDtypeStruct((n_b, ltot, 1024), F32),
                   jax.ShapeDtypeStruct((n_b, MLA_H, ltot, MLA_QK), BF),
                   bf_out(MLA_QR), bf_out(MLA_KVR)],
        compiler_params=_cparams(("arbitrary", "arbitrary")),
        name="proj_tok",
    )(x_lat, x_ctx, mod_l, mod_l, ctok, stok, p["w_tok"], p["wa2"], p["ba"], p["qg"], p["kvg"], p["wukv_k"])


def _rope_rows(x1, x2, cos, sin):
    return x1 * cos - x2 * sin, x1 * sin + x2 * cos


def _proj_feat_kernel(x_ref, c_ref, sh_ref, sc_ref, cq_ref, ckv_ref, cos_ref, sin_ref, w_ref, wuq_ref, wv_ref,
                      dq_ref, dv_ref, g_ref, mq_ref, mv_ref, *, n_lat):
    h = (_ln_rows(_x_tile(x_ref, c_ref, n_lat)) * (1.0 + sc_ref[0]) + sh_ref[0]).astype(BF)
    cos = cos_ref[0]
    sin = sin_ref[0]
    dq = _dot_nt(w_ref[F_DQ:F_DV, :], h) * LOG2E
    for g in range(2 * DIFF_H):
        o1, o2 = _rope_rows(dq[64 * g:64 * g + 32], dq[64 * g + 32:64 * (g + 1)], cos, sin)
        dq_ref[0, 0, 64 * g:64 * g + 32, :] = o1.astype(BF)
        dq_ref[0, 0, 64 * g + 32:64 * (g + 1), :] = o2.astype(BF)
    dv_ref[0, 0] = _dot_nt(w_ref[F_DV:F_GATES, :], h).astype(BF)
    for j in range(3):
        a = F_GATES + 1024 * j
        g_ref[0, 0, 1024 * j:1024 * (j + 1), :] = _dot_nt(w_ref[a:a + 1024, :], h).astype(BF)
    cq = cq_ref[0]
    mq = _dot_nt(wuq_ref[...], cq) * (MLA_SCALE * LOG2E)
    for hh in range(MLA_H):
        a = MLA_QK * hh
        o1, o2 = _rope_rows(mq[a + 128:a + 160], mq[a + 160:a + 192], cos, sin)
        mq_ref[0, 0, a:a + 128, :] = mq[a:a + 128].astype(BF)
        mq_ref[0, 0, a + 128:a + 160, :] = o1.astype(BF)
        mq_ref[0, 0, a + 160:a + 192, :] = o2.astype(BF)
    mv_ref[0, 0] = _dot_nt(wv_ref[...], ckv_ref[0]).astype(BF)


def _proj_feat(xs, mod_l, cosT, sinT, cq, ckv, p, n_b, n_t):
    x_lat, x_ctx, ctx_blk = xs
    rows = _batch_rows(n_b)
    tile_spec = lambda c: pl.BlockSpec((rows, TILE, c), lambda b, t: (b, t, 0))
    mrow = lambda b, t: jnp.where(t == n_t - 1, n_b // rows, b)
    fm_spec = lambda c: pl.BlockSpec((rows, 1, c, TILE), lambda b, t: (b, t, 0, 0))
    fm_out = lambda c: jax.ShapeDtypeStruct((n_b, n_t, c, TILE), BF)
    return pl.pallas_call(
        functools.partial(_per_batch_row(_proj_feat_kernel, 6, 5), n_lat=n_t - 1),
        grid=(n_b // rows, n_t),
        in_specs=_x_specs(rows, n_t - 1, ctx_blk) + [
                  pl.BlockSpec((rows, 1, D_MODEL), lambda b, t: (mrow(b, t), 0, 0)),
                  pl.BlockSpec((rows, 1, D_MODEL), lambda b, t: (mrow(b, t), 0, 1)),
                  tile_spec(MLA_QR), tile_spec(MLA_KVR),
                  pl.BlockSpec((1, 32, TILE), lambda b, t: (t, 0, 0)),
                  pl.BlockSpec((1, 32, TILE), lambda b, t: (t, 0, 0)),
                  _resident((F_END, D_MODEL)), _resident((MLA_H * MLA_QK, MLA_QR)), _resident((1024, MLA_KVR))],
        out_specs=[fm_spec(1024), fm_spec(1024), fm_spec(3072), fm_spec(MLA_H * MLA_QK), fm_spec(1024)],
        out_shape=[fm_out(1024), fm_out(1024), fm_out(3072), fm_out(MLA_H * MLA_QK), fm_out(1024)],
        compiler_params=_cparams(("arbitrary", "arbitrary")),
        name="proj_feat",
    )(x_lat, x_ctx, mod_l, mod_l, cq, ckv, cosT, sinT, p["w_featT"], p["wuqT"], p["wukv_vT"])


def _flash_T(k_ref, v_ref, q2_ref, m_ref, acc_ref, s_ref, n_lat, latent_keys):
    dv = v_ref.shape[2]
    nq = q2_ref.shape[1]
    cb = min(COL_BLOCK, nq)
    blocks = [slice(c, c + cb) for c in range(0, nq, cb)]
    m_ref[2] = jnp.full(m_ref.shape[1:], -jnp.inf, F32)
    acc_ref[...] = jnp.zeros(acc_ref.shape, F32)
    rows = KEY_TILES * TILE

    def scores(slot, tile0, cs):
        s = _dot(k_ref[0, pl.ds(pl.multiple_of(tile0 * TILE, TILE), rows), :], q2_ref[:, cs])
        s_ref[slot, :, cs] = s
        m_ref[slot, :, cs] = jnp.max(s, axis=0, keepdims=True)

    def values(tile0, ntiles):
        v = jnp.concatenate([v_ref[0, tile0 + i] for i in range(ntiles)], axis=1)
        return jnp.concatenate([v, jnp.ones((ONES_ROWS, ntiles * TILE), BF)], axis=0)

    def consume(s, s_max, vext, cs):
        m_old = m_ref[2, :, cs]
        m_new = jnp.maximum(m_old, s_max)
        alpha = jnp.exp2(m_old - m_new)
        p = jnp.exp2(s - m_new).astype(BF)
        acc_ref[:, cs] = acc_ref[:, cs] * alpha + _dot(vext, p)
        m_ref[2, :, cs] = m_new

    def ctx_step(tile):
        vext = values(tile, 1)
        for cs in blocks:
            s = _dot(k_ref[0, pl.ds(tile * TILE, TILE), :], q2_ref[:, cs])
            consume(s, jnp.max(s, axis=0, keepdims=True), vext, cs)

    def step(j, slot, prefetch):
        vext = values(j * KEY_TILES, KEY_TILES)
        for cs in blocks:
            if prefetch:
                scores(1 - slot, (j + 1) * KEY_TILES, cs)
            consume(s_ref[slot, :, cs], m_ref[slot, :, cs], vext, cs)

    if latent_keys:
        n_steps = n_lat // KEY_TILES
        for cs in blocks:
            scores(0, 0, cs)

        def body(jj, carry):
            for u in range(KEY_UNROLL):
                step(KEY_UNROLL * jj + u, u % 2, True)
            return carry
        lax.fori_loop(0, n_steps // KEY_UNROLL - 1, body, 0)
        for u in range(KEY_UNROLL):
            step(n_steps - KEY_UNROLL + u, u % 2, u < KEY_UNROLL - 1)
        ctx_step(n_lat)
    else:
        ctx_step(0)
    acc = acc_ref[...]
    return acc[0:dv] * (1.0 / acc[dv:dv + 1])


def _attn_call(kern, name, q_rows, k_spec_fn, n_b, n_heads, n_t, qs, scratch, qT, k, vT, extra, extra_specs,
               ctx_into=None):
    n_lat = n_t - 1
    if ctx_into is None:
        grid = (n_b, n_heads, n_lat // qs)
        q_spec = pl.BlockSpec((1, qs, q_rows, TILE), lambda b, h, q: (b, q, h, 0))
        k_spec = k_spec_fn(n_t * TILE, 0)
        v_spec = pl.BlockSpec((1, n_t, 128, TILE), lambda b, h, q: (b, 0, h, 0))
        o_spec = pl.BlockSpec((1, qs, 128, TILE), lambda b, h, q: (b, q, h, 0))
        alias_in, alias_specs, aliases = [], [], {}
    else:
        grid = (n_b, n_heads, 1)
        q_spec = pl.BlockSpec((1, 1, q_rows, TILE), lambda b, h, q: (b, n_lat, h, 0))
        k_spec = k_spec_fn(TILE, n_lat)
        v_spec = pl.BlockSpec((1, 1, 128, TILE), lambda b, h, q: (b, n_lat, h, 0))
        o_spec = pl.BlockSpec((1, 1, 128, TILE), lambda b, h, q: (b, n_lat, h, 0))
        alias_in, alias_specs = [ctx_into], [pl.BlockSpec(memory_space=pl.ANY)]
        aliases = {3 + len(extra): 0}
    return pl.pallas_call(
        kern,
        grid=grid,
        in_specs=[q_spec, k_spec, v_spec] + extra_specs + alias_specs,
        out_specs=o_spec,
        out_shape=jax.ShapeDtypeStruct((n_b, n_t, 1024, TILE), BF),
        scratch_shapes=scratch,
        input_output_aliases=aliases,
        compiler_params=_cparams(("arbitrary", "arbitrary", "arbitrary")),
        name=name,
    )(qT, k, vT, *extra, *alias_in)


def _diff_attn_kernel(q_ref, k_ref, v_ref, lam_ref, g_ref, *rest, n_lat, lam_init, qs, latent):
    o_ref, q2_ref, m_ref, acc_ref, s_ref = rest[-5:]
    nq = qs * TILE
    zeros = jnp.zeros((DIFF_DH, TILE), BF)
    for i in range(qs):
        q = q_ref[0, i]
        q2_ref[0:64, i * TILE:(i + 1) * TILE] = q[0:64]
        q2_ref[64:128, i * TILE:(i + 1) * TILE] = zeros
        q2_ref[0:64, nq + i * TILE:nq + (i + 1) * TILE] = zeros
        q2_ref[64:128, nq + i * TILE:nq + (i + 1) * TILE] = q[64:128]
    o = _flash_T(k_ref, v_ref, q2_ref, m_ref, acc_ref, s_ref, n_lat, latent)
    dl = lam_ref[...]
    lam = (jnp.exp(jnp.sum(dl[0:1] * dl[1:2], axis=1, keepdims=True))
           - jnp.exp(jnp.sum(dl[2:3] * dl[3:4], axis=1, keepdims=True)) + lam_init)
    od = o[:, 0:nq] - lam * o[:, nq:2 * nq]
    y = od * lax.rsqrt(jnp.mean(od * od, axis=0, keepdims=True) + EPS) * (1.0 - lam_init)
    for i in range(qs):
        o_ref[0, i] = (y[:, i * TILE:(i + 1) * TILE] * g_ref[...]).astype(BF)


def _diff_attn(dqT, dk, dvT, dlam, gcol, n_b, n_t, lam_init, ctx_into=None):
    qs = DIFF_QS if ctx_into is None else 1
    kern = functools.partial(_diff_attn_kernel, n_lat=n_t - 1, lam_init=lam_init, qs=qs, latent=ctx_into is None)
    k_spec_fn = lambda rows, blk: pl.BlockSpec((1, rows, 128), lambda b, h, q: (b, blk, h))
    scratch = [pltpu.VMEM((128, 2 * qs * TILE), BF), pltpu.VMEM((3, 1, 2 * qs * TILE), F32),
               pltpu.VMEM((128 + ONES_ROWS, 2 * qs * TILE), F32),
               pltpu.VMEM((2, KEY_TILES * TILE, 2 * qs * TILE), F32)]
    extra_specs = [pl.BlockSpec((4, DIFF_DH), lambda b, h, q: (0, 0)),
                   pl.BlockSpec((128, TILE), lambda b, h, q: (0, 0))]
    return _attn_call(kern, "diff_attn" if ctx_into is None else "diff_attn_ctx", 128, k_spec_fn, n_b, DIFF_H, n_t,
                      qs, scratch, dqT, dk, dvT, [dlam, gcol], extra_specs, ctx_into)


def _mla_attn_kernel(q_ref, k_ref, v_ref, *rest, n_lat, qs, latent):
    o_ref, q2_ref, m_ref, acc_ref, s_ref = rest[-5:]
    for i in range(qs):
        q2_ref[:, i * TILE:(i + 1) * TILE] = q_ref[0, i]
    o = _flash_T(k_ref.at[0], v_ref, q2_ref, m_ref, acc_ref, s_ref, n_lat, latent)
    for i in range(qs):
        o_ref[0, i] = o[:, i * TILE:(i + 1) * TILE].astype(BF)


def _mla_attn(mqT, mk, mvT, n_b, n_t, ctx_into=None):
    qs = MLA_QS if ctx_into is None else 1
    kern = functools.partial(_mla_attn_kernel, n_lat=n_t - 1, qs=qs, latent=ctx_into is None)
    k_spec_fn = lambda rows, blk: pl.BlockSpec((1, 1, rows, MLA_QK), lambda b, h, q: (b, h, blk, 0))
    scratch = [pltpu.VMEM((MLA_QK, qs * TILE), BF), pltpu.VMEM((3, 1, qs * TILE), F32),
               pltpu.VMEM((128 + ONES_ROWS, qs * TILE), F32),
               pltpu.VMEM((2, KEY_TILES * TILE, qs * TILE), F32)]
    return _attn_call(kern, "mla_attn" if ctx_into is None else "mla_attn_ctx", MLA_QK, k_spec_fn, n_b, MLA_H, n_t,
                      qs, scratch, mqT, mk, mvT, [], [], ctx_into)


def _gla_chunk_order(reverse):
    n_chunk = TILE // GLA_CHUNK
    return range(n_chunk - 1, -1, -1) if reverse else range(n_chunk)


def _gla_tile(q_ref, k_ref, v_ref, la_ref, tri_ref, bi, reverse, work):
    g = la_ref[bi]
    g1 = g.astype(BF)
    r1 = g - g1.astype(F32)
    g2 = r1.astype(BF)
    tri = tri_ref[...]
    bcum = _dot(tri, g1) + _dot(tri, g2)
    n_sub = GLA_CHUNK // GLA_SUB
    r16 = lax.broadcasted_iota(jnp.int32, (GLA_SUB, GLA_CHUNK), 0)
    c16 = lax.broadcasted_iota(jnp.int32, (GLA_SUB, GLA_CHUNK), 1)
    for hh in range(GLA_H):
        kc0, kc1 = GLA_DK * hh, GLA_DK * (hh + 1)
        for c in _gla_chunk_order(reverse):
            r0 = GLA_CHUNK * c
            bc = bcum[r0:r0 + GLA_CHUNK, kc0:kc1]
            qc = q_ref[bi, r0:r0 + GLA_CHUNK, kc0:kc1].astype(F32)
            kc = k_ref[bi, r0:r0 + GLA_CHUNK, kc0:kc1].astype(F32)
            vc = v_ref[bi, r0:r0 + GLA_CHUNK, GLA_DV * hh:GLA_DV * (hh + 1)]
            b_tot = bc[0:1] if reverse else bc[GLA_CHUNK - 1:GLA_CHUNK]
            a_rows = []
            for i in range(n_sub):
                s0 = GLA_SUB * i
                ref_row = bc[s0 + GLA_SUB - 1:s0 + GLA_SUB] if reverse else bc[s0:s0 + 1]
                qs = (qc[s0:s0 + GLA_SUB] * jnp.exp(bc[s0:s0 + GLA_SUB] - ref_row)).astype(BF)
                lo, hi = (s0, GLA_CHUNK) if reverse else (0, s0 + GLA_SUB)
                ks = (kc[lo:hi] * jnp.exp(ref_row - bc[lo:hi])).astype(BF)
                pad = jnp.zeros((GLA_CHUNK - (hi - lo), GLA_DK), BF)
                ks = jnp.concatenate([pad, ks] if reverse else [ks, pad], axis=0) if hi - lo < GLA_CHUNK else ks
                a = _dot_nt(qs, ks)
                keep = (c16 > r16 + s0) if reverse else (c16 <= r16 + s0)
                a_rows.append(jnp.where(keep, a, 0.0))
            q_in = (qc * jnp.exp(bc)).astype(BF)
            k_d = (kc * jnp.exp(b_tot - bc)).astype(BF)
            work[hh, c] = (a_rows, q_in, vc, jnp.exp(b_tot), _dot_tn(vc, k_d))
            yield


def _gla_finish(work, reverse, st_ref, bi, outs):
    heads = []
    for hh in range(GLA_H):
        o_chunks = [None] * (TILE // GLA_CHUNK)
        st = st_ref[bi, hh]
        for c in _gla_chunk_order(reverse):
            a_rows, q_in, vc, decay, st_inc = work[hh, c]
            a_mat = jnp.concatenate(a_rows, axis=0).astype(BF)
            o_chunks[c] = _dot(a_mat, vc) + _dot_nt(q_in, st.astype(BF))
            st = st * decay + st_inc
            yield
        st_ref[bi, hh] = st
        heads.append(jnp.concatenate(o_chunks, axis=0))
    outs[bi] = jnp.concatenate(heads, axis=1)


def _gla_rows(q_ref, k_ref, v_ref, la_ref, tri_ref, st_ref, reverse):
    n = q_ref.shape[0]
    works = [{} for _ in range(n)]
    outs = [None] * n
    free = [_gla_tile(q_ref, k_ref, v_ref, la_ref, tri_ref, bi, reverse, works[bi]) for bi in range(n)]
    dep = [_gla_finish(works[bi], reverse, st_ref, bi, outs) for bi in range(n)]
    for _ in free[0]:
        pass
    for bi in range(1, n):
        pending = [free[bi], dep[bi - 1]]
        while pending:
            pending = [g for g in pending if next(g, _DONE) is not _DONE]
    for _ in dep[n - 1]:
        pass
    return outs


def _gla_fwd_kernel(q_ref, k_ref, v_ref, la_ref, tri_ref, o_ref, st_ref):
    @pl.when(pl.program_id(1) == 0)
    def _():
        st_ref[...] = jnp.zeros(st_ref.shape, F32)
    for bi, o in enumerate(_gla_rows(q_ref, k_ref, v_ref, la_ref, tri_ref, st_ref, False)):
        o_ref[bi] = o


def _gla_bwd_kernel(q_ref, k_ref, v_ref, la_ref, tri_ref, of_ref, r_ref, g_ref, y_ref, st_ref):
    @pl.when(pl.program_id(1) == 0)
    def _():
        st_ref[...] = jnp.zeros(st_ref.shape, F32)
    gn = g_ref[...]
    for bi, o_b in enumerate(_gla_rows(q_ref, k_ref, v_ref, la_ref, tri_ref, st_ref, True)):
        o = of_ref[bi] + o_b
        for hh in range(GLA_H):
            sl = slice(GLA_DV * hh, GLA_DV * (hh + 1))
            oh = o[:, sl]
            yh = oh * lax.rsqrt(jnp.mean(oh * oh, axis=-1, keepdims=True) + EPS) * gn
            y_ref[bi, :, sl] = (yh * _silu(r_ref[bi, :, sl].astype(F32))).astype(BF)


def _gla(gq, gk, gv, la, gr, tri_lo, tri_up, gnorm, n_b, n_t):
    ltot = n_t * TILE
    rows = GLA_BATCH if n_b % GLA_BATCH == 0 else 1
    t_fwd = lambda s: (s + n_t - 1) % n_t
    t_bwd = lambda s: jnp.where(s == 0, n_t - 1, n_t - 1 - s)

    def specs(tmap, dirn):
        ts = lambda c, cb=0: pl.BlockSpec((rows, TILE, c), lambda b, s: (b, tmap(s), cb))
        return [ts(512), ts(512), ts(1024), ts(512, dirn), pl.BlockSpec((TILE, TILE), lambda b, s: (0, 0))]

    o_f = pl.pallas_call(
        _gla_fwd_kernel,
        grid=(n_b // rows, n_t),
        in_specs=specs(t_fwd, 0),
        out_specs=pl.BlockSpec((rows, TILE, 1024), lambda b, s: (b, t_fwd(s), 0)),
        out_shape=jax.ShapeDtypeStruct((n_b, ltot, 1024), F32),
        scratch_shapes=[pltpu.VMEM((rows, GLA_H, GLA_DV, GLA_DK), F32)],
        compiler_params=_cparams(("arbitrary", "arbitrary")),
        name="gla_fwd",
    )(gq, gk, gv, la, tri_lo)
    tsb = lambda c: pl.BlockSpec((rows, TILE, c), lambda b, s: (b, t_bwd(s), 0))
    return pl.pallas_call(
        _gla_bwd_kernel,
        grid=(n_b // rows, n_t),
        in_specs=specs(t_bwd, 1) + [tsb(1024), tsb(1024), pl.BlockSpec((1, GLA_DV), lambda b, s: (0, 0))],
        out_specs=tsb(1024),
        out_shape=jax.ShapeDtypeStruct((n_b, ltot, 1024), BF),
        scratch_shapes=[pltpu.VMEM((rows, GLA_H, GLA_DV, GLA_DK), F32)],
        compiler_params=_cparams(("arbitrary", "arbitrary")),
        name="gla_bwd",
    )(gq, gk, gv, la, tri_up, o_f, gr, gnorm)


def _merge_kernel(x_ref, c_ref, ya_ref, yb_ref, yc_ref, g_ref, g1_ref, wb_ref, wo_ref, lg_ref, lb_ref, o_ref, *,
                  n_lat):
    zt = (_sigmoid(g_ref[0, 0, 0:1024, :].astype(F32)) * _dot_nt(wb_ref[0], ya_ref[0])
          + _sigmoid(g_ref[0, 0, 1024:2048, :].astype(F32)) * _dot(wb_ref[1], yb_ref[0, 0])
          + _sigmoid(g_ref[0, 0, 2048:3072, :].astype(F32)) * _dot(wb_ref[2], yc_ref[0, 0]))
    yield
    u = _dot(wo_ref[...], zt.astype(BF)).T
    y = _ln_rows(DN_ALPHA * _x_tile(x_ref, c_ref, n_lat) + g1_ref[0] * u)
    o_ref[0] = y * lg_ref[...] + lb_ref[...]


def _merge(xs, ya, ybT, ycT, gatesT, mod_l, p, n_b, n_t, n_q):
    x_lat, x_ctx, ctx_blk = xs
    rows = _batch_rows(n_b)
    tile_spec = lambda c: pl.BlockSpec((rows, TILE, c), lambda b, t: (b, t, 0))
    fm_spec = lambda c: pl.BlockSpec((rows, 1, c, TILE), lambda b, t: (b, t, 0, 0))
    mrow = lambda b, t: jnp.where(t == n_t - 1, n_b // rows, b)
    return pl.pallas_call(
        functools.partial(_per_batch_row(_merge_kernel, 7, 4), n_lat=n_t - 1),
        grid=(n_b // rows, n_q),
        in_specs=_x_specs(rows, n_t - 1, ctx_blk) + [
                  tile_spec(1024), fm_spec(1024), fm_spec(1024), fm_spec(3072),
                  pl.BlockSpec((rows, 1, D_MODEL), lambda b, t: (mrow(b, t), 0, 2)),
                  _resident((3, D_MODEL, D_MODEL)), _resident((D_MODEL, D_MODEL)),
                  _resident((1, D_MODEL)), _resident((1, D_MODEL))],
        out_specs=tile_spec(D_MODEL),
        out_shape=jax.ShapeDtypeStruct((n_b, n_q * TILE, D_MODEL), F32),
        compiler_params=_cparams(("arbitrary", "arbitrary")),
        name="merge",
    )(x_lat, x_ctx, ya, ybT, ycT, gatesT, mod_l, p["wbT"], p["woT"], p["ln1_g"], p["ln1_b"])


def _ffn_kernel(x_ref, sh_ref, sc_ref, g2_ref, wi_ref, wo_ref, lg_ref, lb_ref, o_ref):
    x = x_ref[0]
    h = (_ln_rows(x) * (1.0 + sc_ref[0]) + sh_ref[0]).astype(BF)
    acts = []
    for a, b in FFN_CHUNKS:
        gate = _dot(h, wi_ref[:, a:b])
        up = _dot(h, wi_ref[:, FFN_H + a:FFN_H + b])
        acts.append((_silu(gate) * up).astype(BF))
    yield
    acc = jnp.zeros((TILE, D_MODEL), F32)
    for (a, b), act in zip(FFN_CHUNKS, acts):
        acc = acc + _dot(act, wo_ref[a:b, :])
    y = _ln_rows(DN_ALPHA * x + g2_ref[0] * acc)
    o_ref[0] = y * lg_ref[...] + lb_ref[...]


def _ffn(x1, mod_l, p, n_b, n_t, n_q):
    rows = _batch_rows(n_b)
    tile_spec = lambda c: pl.BlockSpec((rows, TILE, c), lambda b, t: (b, t, 0))
    mrow = lambda b, t: jnp.where(t == n_t - 1, n_b // rows, b)
    mspec = lambda j: pl.BlockSpec((rows, 1, D_MODEL), lambda b, t: (mrow(b, t), 0, j))
    return pl.pallas_call(
        _per_batch_row(_ffn_kernel, 4, 4),
        grid=(n_b // rows, n_q),
        in_specs=[tile_spec(D_MODEL), mspec(3), mspec(4), mspec(5),
                  _resident((D_MODEL, 2 * FFN_H)), _resident((FFN_H, D_MODEL)),
                  _resident((1, D_MODEL)), _resident((1, D_MODEL))],
        out_specs=tile_spec(D_MODEL),
        out_shape=jax.ShapeDtypeStruct((n_b, n_q * TILE, D_MODEL), F32),
        compiler_params=_cparams(("arbitrary", "arbitrary")),
        name="ffn",
    )(x1, mod_l, mod_l, mod_l, p["ffn_wi"], p["ffn_wo"], p["ln2_g"], p["ln2_b"])


def _prep_layer(l, w_in, gla_w_a2, gla_b_a, mla_q_norm_g, mla_kv_norm_g, mla_w_uq, mla_w_ukv, w_branch, w_out,
                ln1_g, ln1_b, ffn_w_in, ffn_w_out, ln2_g, ln2_b):
    w = w_in[l]
    seg = lambda a, b: w[:, a:b]
    w_tok = jnp.concatenate([
        seg(O_GQ, O_GK), seg(O_GK, O_GV), seg(O_GV, O_GR), seg(O_GR, O_GA), seg(O_DK, O_DV),
        seg(O_MQ, O_MKV), seg(O_MKV, O_MKR),
        seg(O_MKR, O_GATES), seg(O_GA, O_DQ), jnp.zeros((D_MODEL, 32), F32)], axis=1).astype(BF)
    w_featT = jnp.concatenate([seg(O_DQ, O_DK) * DIFF_DH ** -0.5, seg(O_DV, O_MQ), seg(O_GATES, O_END)],
                              axis=1).T.astype(BF)
    wa2 = jnp.zeros((128, 1024), F32)
    wa2 = wa2.at[64:80, 0:512].set(gla_w_a2[l, 0]).at[80:96, 512:1024].set(gla_w_a2[l, 1]).astype(BF)
    ukv = mla_w_ukv[l].reshape(MLA_KVR, MLA_H, MLA_NOPE + MLA_DV)
    return dict(
        w_tok=w_tok, w_featT=w_featT, wa2=wa2, ba=gla_b_a[l].reshape(1, 1024),
        qg=mla_q_norm_g[l].reshape(1, MLA_QR), kvg=mla_kv_norm_g[l].reshape(1, MLA_KVR),
        wukv_k=ukv[:, :, :MLA_NOPE].reshape(MLA_KVR, 1024).astype(BF),
        wukv_vT=ukv[:, :, MLA_NOPE:].reshape(MLA_KVR, 1024).T.astype(BF),
        wuqT=mla_w_uq[l].T.astype(BF),
        wbT=jnp.swapaxes(w_branch[l], 1, 2).astype(BF), woT=w_out[l].T.astype(BF),
        ln1_g=ln1_g[l].reshape(1, D_MODEL), ln1_b=ln1_b[l].reshape(1, D_MODEL),
        ffn_wi=ffn_w_in[l].astype(BF), ffn_wo=ffn_w_out[l].astype(BF),
        ln2_g=ln2_g[l].reshape(1, D_MODEL), ln2_b=ln2_b[l].reshape(1, D_MODEL))


def _rope_tables(l_lat, l_ctx):
    rows = l_lat // GRID_W
    pos_row = jnp.broadcast_to(jnp.arange(rows, dtype=F32)[:, None], (rows, GRID_W)).reshape(l_lat)
    pos_col = jnp.broadcast_to(jnp.arange(GRID_W, dtype=F32)[None, :], (rows, GRID_W)).reshape(l_lat)
    d_axis = ROPE_DIM // 2
    inv = ROPE_BASE ** (-jnp.arange(0, d_axis, 2, dtype=F32) / d_axis)
    ang = jnp.concatenate([pos_row[:, None] * inv, pos_col[:, None] * inv], axis=-1)
    cos = jnp.concatenate([jnp.cos(ang), jnp.ones((l_ctx, 32), F32)], axis=0)
    sin = jnp.concatenate([jnp.sin(ang), jnp.zeros((l_ctx, 32), F32)], axis=0)
    ctok = jnp.tile(cos, (1, 4))
    stok = jnp.tile(jnp.concatenate([-sin, sin], axis=1), (1, 2))
    n_t = (l_lat + l_ctx) // TILE
    to_fm = lambda a: a.T.reshape(32, n_t, TILE).transpose(1, 0, 2)
    return ctok, stok, to_fm(cos), to_fm(sin)


def kernel(x, c, ctx, c_ctx, w_mod, b_mod, w_in, gla_w_a2, gla_b_a, gla_norm_g, diff_lam, diff_norm_g,
           mla_q_norm_g, mla_kv_norm_g, mla_w_uq, mla_w_ukv, w_branch, w_out, ln1_g, ln1_b, ffn_w_in, ffn_w_out,
           ln2_g, ln2_b):
    n_b, l_lat, _ = x.shape
    l_ctx = ctx.shape[1]
    assert l_ctx == TILE and n_b + TOK_BATCH <= 16
    assert l_lat % (TILE * max(DIFF_QS, MLA_QS)) == 0 and l_lat % (TILE * KEY_TILES * KEY_UNROLL) == 0
    ltot = l_lat + l_ctx
    n_t = ltot // TILE
    ctok, stok, cosT, sinT = _rope_tables(l_lat, l_ctx)
    c_all = jnp.zeros((16, D_MODEL), F32).at[:n_b].set(c).at[n_b:n_b + TOK_BATCH].set(c_ctx)
    mod = _modulation(c_all, w_mod, b_mod)
    ii = lax.broadcasted_iota(jnp.int32, (TILE, TILE), 0)
    jj = lax.broadcasted_iota(jnp.int32, (TILE, TILE), 1)
    same = (ii // GLA_CHUNK) == (jj // GLA_CHUNK)
    tri_lo = (same & (jj <= ii)).astype(BF)
    tri_up = (same & (jj >= ii)).astype(BF)
    xs = (x, ctx, 0)
    for l in range(N_LAYERS):
        last = l == N_LAYERS - 1
        n_q = n_t - 1 if last else n_t
        lam_init = 0.8 - 0.6 * math.exp(-0.3 * l)
        p = _prep_layer(l, w_in, gla_w_a2, gla_b_a, mla_q_norm_g, mla_kv_norm_g, mla_w_uq, mla_w_ukv, w_branch,
                        w_out, ln1_g, ln1_b, ffn_w_in, ffn_w_out, ln2_g, ln2_b)
        mod_l = mod[l].reshape(16, 1, 6 * D_MODEL)
        gq, gk, gv, gr, dk, la, mk, cq, ckv = _proj_tok(xs, mod_l, ctok, stok, p, n_b, n_t, ltot)
        dqT, dvT, gatesT, mqT, mvT = _proj_feat(xs, mod_l, cosT, sinT, cq, ckv, p, n_b, n_t)
        ya = _gla(gq, gk, gv, la, gr, tri_lo, tri_up, gla_norm_g[l].reshape(1, GLA_DV), n_b, n_t)
        gcol = jnp.broadcast_to(diff_norm_g[l].reshape(128, 1), (128, TILE))
        ybT = _diff_attn(dqT, dk, dvT, diff_lam[l], gcol, n_b, n_t, lam_init)
        ycT = _mla_attn(mqT, mk, mvT, n_b, n_t)
        if not last:
            ybT = _diff_attn(dqT, dk, dvT, diff_lam[l], gcol, n_b, n_t, lam_init, ctx_into=ybT)
            ycT = _mla_attn(mqT, mk, mvT, n_b, n_t, ctx_into=ycT)
        x1 = _merge(xs, ya, ybT, ycT, gatesT, mod_l, p, n_b, n_t, n_q)
        x2 = _ffn(x1, mod_l, p, n_b, n_t, n_q)
        xs = (x2, x2, n_t - 1)
    return x2
```

```python
import functools
import math

import jax
import jax.numpy as jnp
from jax import lax
from jax.experimental import pallas as pl
from jax.experimental.pallas import tpu as pltpu

BF = jnp.bfloat16
F32 = jnp.float32

D_MODEL = 1024
N_LAYERS = 2
GRID_W = 64
TILE = 256
GLA_H, GLA_DK, GLA_DV, GLA_RANK, GLA_TAU = 4, 128, 256, 16, 16.0
GLA_CHUNK, GLA_SUB = 64, 16
GLA_BATCH = 4
TOK_BATCH = 2
DIFF_H, DIFF_DH = 8, 64
MLA_H, MLA_QR, MLA_KVR, MLA_NOPE, MLA_ROPE, MLA_DV = 8, 256, 128, 128, 64, 128
MLA_SCALE = (MLA_NOPE + MLA_ROPE) ** -0.5
ROPE_DIM, ROPE_BASE = 64, 10000.0
FFN_H = 2816
FFN_CHUNKS = ((0, 1536), (1536, 2816))
DN_ALPHA = (2 * N_LAYERS) ** 0.25
EPS = 1e-6
DIFF_QS, MLA_QS = 8, 16
COL_BLOCK = 512
KEY_TILES = 2
KEY_UNROLL = 4
LOG2E = math.log2(math.e)
ONES_ROWS = 16
VMEM_LIMIT = 56 * 1024 * 1024

_SIZES = (512, 512, 1024, 1024, 32, 1024, 1024, 1024, 256, 128, 64, 3072)
_OFF = [0]
for _s in _SIZES:
    _OFF.append(_OFF[-1] + _s)
(O_GQ, O_GK, O_GV, O_GR, O_GA, O_DQ, O_DK, O_DV, O_MQ, O_MKV, O_MKR, O_GATES, O_END) = _OFF

T_GQ, T_GK, T_GV, T_GR, T_DK, T_MQ, T_MKV, T_SMA, T_END = (0, 512, 1024, 2048, 3072, 4096, 4352, 4480, 4608)
F_DQ, F_DV, F_GATES, F_END = 0, 1024, 2048, 5120


def _cparams(sem):
    return pltpu.CompilerParams(dimension_semantics=sem, vmem_limit_bytes=VMEM_LIMIT)


def _resident(shape):
    nd = len(shape)
    return pl.BlockSpec(shape, lambda *_: (0,) * nd, pipeline_mode=pl.Buffered(1))


def _dot(a, b):
    return jnp.dot(a, b, preferred_element_type=F32)


def _dot_nt(a, b):
    return lax.dot_general(a, b, (((1,), (1,)), ((), ())), preferred_element_type=F32)


def _dot_tn(a, b):
    return lax.dot_general(a, b, (((0,), (0,)), ((), ())), preferred_element_type=F32)


def _ln_rows(x):
    mu = jnp.mean(x, axis=-1, keepdims=True)
    xc = x - mu
    var = jnp.mean(xc * xc, axis=-1, keepdims=True)
    return xc * lax.rsqrt(var + EPS)


def _sigmoid(x):
    return 1.0 / (1.0 + jnp.exp(-x))


def _silu(x):
    return x * _sigmoid(x)


def _mod_kernel(c_ref, w_ref, b_ref, o_ref):
    ca = _silu(c_ref[...]).astype(BF)
    o_ref[0] = _dot(ca, w_ref[0].astype(BF)) + b_ref[0]


def _modulation(c_all, w_mod, b_mod):
    nl, d, n6 = w_mod.shape
    r = c_all.shape[0]
    nblk = 1536
    return pl.pallas_call(
        _mod_kernel,
        grid=(nl, n6 // nblk),
        in_specs=[pl.BlockSpec((r, d), lambda l, j: (0, 0)),
                  pl.BlockSpec((1, d, nblk), lambda l, j: (l, 0, j)),
                  pl.BlockSpec((1, 1, nblk), lambda l, j: (l, 0, j))],
        out_specs=pl.BlockSpec((1, r, nblk), lambda l, j: (l, 0, j)),
        out_shape=jax.ShapeDtypeStruct((nl, r, n6), F32),
        compiler_params=_cparams(("arbitrary", "arbitrary")),
        name="modulation",
    )(c_all, w_mod, b_mod.reshape(nl, 1, n6))


def _batch_rows(n_b):
    return TOK_BATCH if n_b % TOK_BATCH == 0 else 1


_DONE = object()


def _per_batch_row(body, n_rowed, n_shared):
    def kern(*refs, **kw):
        rowed, shared, outs = refs[:n_rowed], refs[n_rowed:n_rowed + n_shared], refs[n_rowed + n_shared:]
        pending = []
        for bi in range(rowed[0].shape[0]):
            row = lambda ref, bi=bi: ref.at[pl.ds(bi, 1)]
            pending.append(body(*map(row, rowed), *shared, *map(row, outs), **kw))
        pending = [g for g in pending if g is not None]
        while pending:
            pending = [g for g in pending if next(g, _DONE) is not _DONE]
    return kern


def _x_specs(rows, n_lat, ctx_blk):
    return [pl.BlockSpec((rows, TILE, D_MODEL), lambda b, t: (b, jnp.minimum(t, n_lat - 1), 0)),
            pl.BlockSpec((rows, TILE, D_MODEL), lambda b, t: (b, ctx_blk, 0))]


def _x_tile(x_ref, c_ref, n_lat):
    return jnp.where(pl.program_id(1) == n_lat, c_ref[0], x_ref[0])


def _proj_tok_kernel(x_ref, c_ref, sh_ref, sc_ref, ct_ref, st_ref, w_ref, wa2_ref, ba_ref, qg_ref, kvg_ref, wk_ref,
                     gq_ref, gk_ref, gv_ref, gr_ref, dk_ref, la_ref, mk_ref, cq_ref, ckv_ref, *, n_lat):
    h = (_ln_rows(_x_tile(x_ref, c_ref, n_lat)) * (1.0 + sc_ref[0]) + sh_ref[0]).astype(BF)

    def proj(a, b):
        return _dot(h, w_ref[:, a:b])

    ct = ct_ref[...]
    st = st_ref[...]
    first_half = lax.broadcasted_iota(jnp.int32, (TILE, 128), 1) % ROPE_DIM < ROPE_DIM // 2

    def rope(x):
        partner = jnp.where(first_half, pltpu.roll(x, 128 - ROPE_DIM // 2, 1), pltpu.roll(x, ROPE_DIM // 2, 1))
        return x * ct + partner * st

    mm = proj(T_MKV, T_END)
    mkv = mm[:, 0:128]
    sma = mm[:, 128:256]
    ckv = (mkv * lax.rsqrt(jnp.mean(mkv * mkv, axis=-1, keepdims=True) + EPS) * kvg_ref[...]).astype(BF)
    ckv_ref[0] = ckv
    mq = proj(T_MQ, T_MKV)
    cq = mq * lax.rsqrt(jnp.mean(mq * mq, axis=-1, keepdims=True) + EPS) * qg_ref[...]
    cq_ref[0] = cq.astype(BF)
    gq_ref[0] = (proj(T_GQ, T_GK) * GLA_DK ** -0.5).astype(BF)
    gk_ref[0] = proj(T_GK, T_GV).astype(BF)
    gv_ref[0] = proj(T_GV, T_GR).astype(BF)
    gr_ref[0] = proj(T_GR, T_DK).astype(BF)
    dkx = proj(T_DK, T_MQ)
    for hh in range(DIFF_H):
        sl = slice(128 * hh, 128 * (hh + 1))
        dk_ref[0, :, sl] = rope(dkx[:, sl]).astype(BF)
    yield
    kn = _dot(ckv, wk_ref[...]).astype(BF)
    kr = rope(sma)[:, 0:64].astype(BF)
    for hh in range(MLA_H):
        mk_ref[0, hh, :, 0:128] = kn[:, 128 * hh:128 * (hh + 1)]
        mk_ref[0, hh, :, 128:192] = kr
    z = _dot(sma.astype(BF), wa2_ref[...]) + ba_ref[...]
    la_ref[0] = (jnp.minimum(z, 0.0) - jnp.log(1.0 + jnp.exp(-jnp.abs(z)))) * (1.0 / GLA_TAU)


def _proj_tok(xs, mod_l, ctok, stok, p, n_b, n_t, ltot):
    x_lat, x_ctx, ctx_blk = xs
    rows = _batch_rows(n_b)
    tile_spec = lambda c: pl.BlockSpec((rows, TILE, c), lambda b, t: (b, t, 0))
    mrow = lambda b, t: jnp.where(t == n_t - 1, n_b // rows, b)
    bf_out = lambda c: jax.ShapeDtypeStruct((n_b, ltot, c), BF)
    return pl.pallas_call(
        functools.partial(_per_batch_row(_proj_tok_kernel, 4, 8), n_lat=n_t - 1),
        grid=(n_b // rows, n_t),
        in_specs=_x_specs(rows, n_t - 1, ctx_blk) + [
                  pl.BlockSpec((rows, 1, D_MODEL), lambda b, t: (mrow(b, t), 0, 0)),
                  pl.BlockSpec((rows, 1, D_MODEL), lambda b, t: (mrow(b, t), 0, 1)),
                  pl.BlockSpec((TILE, 128), lambda b, t: (t, 0)),
                  pl.BlockSpec((TILE, 128), lambda b, t: (t, 0)),
                  _resident((D_MODEL, T_END)), _resident((128, 1024)), _resident((1, 1024)),
                  _resident((1, MLA_QR)), _resident((1, MLA_KVR)), _resident((MLA_KVR, 1024))],
        out_specs=[tile_spec(512), tile_spec(512), tile_spec(1024), tile_spec(1024), tile_spec(1024),
                   tile_spec(1024),
                   pl.BlockSpec((rows, MLA_H, TILE, 192), lambda b, t: (b, 0, t, 0)),
                   tile_spec(MLA_QR), tile_spec(MLA_KVR)],
        out_shape=[bf_out(512), bf_out(512), bf_out(1024), bf_out(1024), bf_out(1024),
                   jax.ShapeDtypeStruct((n_b, ltot, 1024), F32),
                   jax.ShapeDtypeStruct((n_b, MLA_H, ltot, 192), BF),
                   bf_out(MLA_QR), bf_out(MLA_KVR)],
        compiler_params=_cparams(("arbitrary", "arbitrary")),
        name="proj_tok",
    )(x_lat, x_ctx, mod_l, mod_l, ctok, stok, p["w_tok"], p["wa2"], p["ba"], p["qg"], p["kvg"], p["wukv_k"])


def _rope_rows(x1, x2, cos, sin):
    return x1 * cos - x2 * sin, x1 * sin + x2 * cos


def _proj_feat_kernel(x_ref, c_ref, sh_ref, sc_ref, cq_ref, ckv_ref, cos_ref, sin_ref, w_ref, wuq_ref, wv_ref,
                      dq_ref, dv_ref, g_ref, mq_ref, mv_ref, *, n_lat):
    h = (_ln_rows(_x_tile(x_ref, c_ref, n_lat)) * (1.0 + sc_ref[0]) + sh_ref[0]).astype(BF)
    cos = cos_ref[0]
    sin = sin_ref[0]
    dq = _dot_nt(w_ref[F_DQ:F_DV, :], h) * LOG2E
    for g in range(2 * DIFF_H):
        o1, o2 = _rope_rows(dq[64 * g:64 * g + 32], dq[64 * g + 32:64 * (g + 1)], cos, sin)
        dq_ref[0, 0, 64 * g:64 * g + 32, :] = o1.astype(BF)
        dq_ref[0, 0, 64 * g + 32:64 * (g + 1), :] = o2.astype(BF)
    dv_ref[0, 0] = _dot_nt(w_ref[F_DV:F_GATES, :], h).astype(BF)
    for j in range(3):
        a = F_GATES + 1024 * j
        g_ref[0, 0, 1024 * j:1024 * (j + 1), :] = _dot_nt(w_ref[a:a + 1024, :], h).astype(BF)
    cq = cq_ref[0]
    mq = _dot_nt(wuq_ref[...], cq) * (MLA_SCALE * LOG2E)
    for hh in range(MLA_H):
        a = 192 * hh
        o1, o2 = _rope_rows(mq[a + 128:a + 160], mq[a + 160:a + 192], cos, sin)
        mq_ref[0, 0, a:a + 128, :] = mq[a:a + 128].astype(BF)
        mq_ref[0, 0, a + 128:a + 160, :] = o1.astype(BF)
        mq_ref[0, 0, a + 160:a + 192, :] = o2.astype(BF)
    mv_ref[0, 0] = _dot_nt(wv_ref[...], ckv_ref[0]).astype(BF)


def _proj_feat(xs, mod_l, cosT, sinT, cq, ckv, p, n_b, n_t):
    x_lat, x_ctx, ctx_blk = xs
    rows = _batch_rows(n_b)
    tile_spec = lambda c: pl.BlockSpec((rows, TILE, c), lambda b, t: (b, t, 0))
    mrow = lambda b, t: jnp.where(t == n_t - 1, n_b // rows, b)
    fm_spec = lambda c: pl.BlockSpec((rows, 1, c, TILE), lambda b, t: (b, t, 0, 0))
    fm_out = lambda c: jax.ShapeDtypeStruct((n_b, n_t, c, TILE), BF)
    return pl.pallas_call(
        functools.partial(_per_batch_row(_proj_feat_kernel, 6, 5), n_lat=n_t - 1),
        grid=(n_b // rows, n_t),
        in_specs=_x_specs(rows, n_t - 1, ctx_blk) + [
                  pl.BlockSpec((rows, 1, D_MODEL), lambda b, t: (mrow(b, t), 0, 0)),
                  pl.BlockSpec((rows, 1, D_MODEL), lambda b, t: (mrow(b, t), 0, 1)),
                  tile_spec(MLA_QR), tile_spec(MLA_KVR),
                  pl.BlockSpec((1, 32, TILE), lambda b, t: (t, 0, 0)),
                  pl.BlockSpec((1, 32, TILE), lambda b, t: (t, 0, 0)),
                  _resident((F_END, D_MODEL)), _resident((MLA_H * 192, MLA_QR)), _resident((1024, MLA_KVR))],
        out_specs=[fm_spec(1024), fm_spec(1024), fm_spec(3072), fm_spec(MLA_H * 192), fm_spec(1024)],
        out_shape=[fm_out(1024), fm_out(1024), fm_out(3072), fm_out(MLA_H * 192), fm_out(1024)],
        compiler_params=_cparams(("arbitrary", "arbitrary")),
        name="proj_feat",
    )(x_lat, x_ctx, mod_l, mod_l, cq, ckv, cosT, sinT, p["w_featT"], p["wuqT"], p["wukv_vT"])


def _flash_T(k_ref, v_ref, q2_ref, m_ref, acc_ref, s_ref, n_lat, latent_keys):
    dv = v_ref.shape[2]
    nq = q2_ref.shape[1]
    cb = min(COL_BLOCK, nq)
    blocks = [slice(c, c + cb) for c in range(0, nq, cb)]
    m_ref[2] = jnp.full(m_ref.shape[1:], -jnp.inf, F32)
    acc_ref[...] = jnp.zeros(acc_ref.shape, F32)
    rows = KEY_TILES * TILE

    def scores(slot, tile0, cs):
        s = _dot(k_ref[0, pl.ds(pl.multiple_of(tile0 * TILE, TILE), rows), :], q2_ref[:, cs])
        s_ref[slot, :, cs] = s
        m_ref[slot, :, cs] = jnp.max(s, axis=0, keepdims=True)

    def consume(s, s_max, tile0, ntiles, cs):
        m_old = m_ref[2, :, cs]
        m_new = jnp.maximum(m_old, s_max)
        alpha = jnp.exp2(m_old - m_new)
        p = jnp.exp2(s - m_new).astype(BF)
        v = jnp.concatenate([v_ref[0, tile0 + i] for i in range(ntiles)], axis=1)
        vext = jnp.concatenate([v, jnp.ones((ONES_ROWS, ntiles * TILE), BF)], axis=0)
        acc_ref[:, cs] = acc_ref[:, cs] * alpha + _dot(vext, p)
        m_ref[2, :, cs] = m_new

    def ctx_step(tile, cs):
        s = _dot(k_ref[0, pl.ds(tile * TILE, TILE), :], q2_ref[:, cs])
        consume(s, jnp.max(s, axis=0, keepdims=True), tile, 1, cs)

    if latent_keys:
        n_steps = n_lat // KEY_TILES
        for cs in blocks:
            scores(0, 0, cs)

        def body(jj, carry):
            for u in range(KEY_UNROLL):
                j = KEY_UNROLL * jj + u
                for cs in blocks:
                    scores((u + 1) % 2, (j + 1) * KEY_TILES, cs)
                    consume(s_ref[u % 2, :, cs], m_ref[u % 2, :, cs], j * KEY_TILES, KEY_TILES, cs)
            return carry
        lax.fori_loop(0, n_steps // KEY_UNROLL - 1, body, 0)
        for u in range(KEY_UNROLL):
            j = n_steps - KEY_UNROLL + u
            for cs in blocks:
                if u < KEY_UNROLL - 1:
                    scores((u + 1) % 2, (j + 1) * KEY_TILES, cs)
                consume(s_ref[u % 2, :, cs], m_ref[u % 2, :, cs], j * KEY_TILES, KEY_TILES, cs)
        for cs in blocks:
            ctx_step(n_lat, cs)
    else:
        for cs in blocks:
            ctx_step(0, cs)
    acc = acc_ref[...]
    return acc[0:dv] * (1.0 / acc[dv:dv + 1])


def _attn_call(kern, name, q_rows, k_spec_fn, n_b, n_heads, n_t, qs, scratch, qT, k, vT, extra, extra_specs,
               ctx_into=None):
    n_lat = n_t - 1
    if ctx_into is None:
        grid = (n_b, n_heads, n_lat // qs)
        q_spec = pl.BlockSpec((1, qs, q_rows, TILE), lambda b, h, q: (b, q, h, 0))
        k_spec = k_spec_fn(n_t * TILE, 0)
        v_spec = pl.BlockSpec((1, n_t, 128, TILE), lambda b, h, q: (b, 0, h, 0))
        o_spec = pl.BlockSpec((1, qs, 128, TILE), lambda b, h, q: (b, q, h, 0))
        alias_in, alias_specs, aliases = [], [], {}
    else:
        grid = (n_b, n_heads, 1)
        q_spec = pl.BlockSpec((1, 1, q_rows, TILE), lambda b, h, q: (b, n_lat, h, 0))
        k_spec = k_spec_fn(TILE, n_lat)
        v_spec = pl.BlockSpec((1, 1, 128, TILE), lambda b, h, q: (b, n_lat, h, 0))
        o_spec = pl.BlockSpec((1, 1, 128, TILE), lambda b, h, q: (b, n_lat, h, 0))
        alias_in, alias_specs = [ctx_into], [pl.BlockSpec(memory_space=pl.ANY)]
        aliases = {3 + len(extra): 0}
    return pl.pallas_call(
        kern,
        grid=grid,
        in_specs=[q_spec, k_spec, v_spec] + extra_specs + alias_specs,
        out_specs=o_spec,
        out_shape=jax.ShapeDtypeStruct((n_b, n_t, 1024, TILE), BF),
        scratch_shapes=scratch,
        input_output_aliases=aliases,
        compiler_params=_cparams(("arbitrary", "arbitrary", "arbitrary")),
        name=name,
    )(qT, k, vT, *extra, *alias_in)


def _diff_attn_kernel(q_ref, k_ref, v_ref, lam_ref, g_ref, *rest, n_lat, lam_init, qs, latent):
    o_ref, q2_ref, m_ref, acc_ref, s_ref = rest[-5:]
    nq = qs * TILE
    zeros = jnp.zeros((DIFF_DH, TILE), BF)
    for i in range(qs):
        q = q_ref[0, i]
        q2_ref[0:64, i * TILE:(i + 1) * TILE] = q[0:64]
        q2_ref[64:128, i * TILE:(i + 1) * TILE] = zeros
        q2_ref[0:64, nq + i * TILE:nq + (i + 1) * TILE] = zeros
        q2_ref[64:128, nq + i * TILE:nq + (i + 1) * TILE] = q[64:128]
    o = _flash_T(k_ref, v_ref, q2_ref, m_ref, acc_ref, s_ref, n_lat, latent)
    dl = lam_ref[...]
    lam = (jnp.exp(jnp.sum(dl[0:1] * dl[1:2], axis=1, keepdims=True))
           - jnp.exp(jnp.sum(dl[2:3] * dl[3:4], axis=1, keepdims=True)) + lam_init)
    od = o[:, 0:nq] - lam * o[:, nq:2 * nq]
    y = od * lax.rsqrt(jnp.mean(od * od, axis=0, keepdims=True) + EPS) * (1.0 - lam_init)
    for i in range(qs):
        o_ref[0, i] = (y[:, i * TILE:(i + 1) * TILE] * g_ref[...]).astype(BF)


def _diff_attn(dqT, dk, dvT, dlam, gcol, n_b, n_t, lam_init, ctx_into=None):
    qs = DIFF_QS if ctx_into is None else 1
    kern = functools.partial(_diff_attn_kernel, n_lat=n_t - 1, lam_init=lam_init, qs=qs, latent=ctx_into is None)
    k_spec_fn = lambda rows, blk: pl.BlockSpec((1, rows, 128), lambda b, h, q: (b, blk, h))
    scratch = [pltpu.VMEM((128, 2 * qs * TILE), BF), pltpu.VMEM((3, 1, 2 * qs * TILE), F32),
               pltpu.VMEM((128 + ONES_ROWS, 2 * qs * TILE), F32),
               pltpu.VMEM((2, KEY_TILES * TILE, 2 * qs * TILE), F32)]
    extra_specs = [pl.BlockSpec((4, DIFF_DH), lambda b, h, q: (0, 0)),
                   pl.BlockSpec((128, TILE), lambda b, h, q: (0, 0))]
    return _attn_call(kern, "diff_attn" if ctx_into is None else "diff_attn_ctx", 128, k_spec_fn, n_b, DIFF_H, n_t,
                      qs, scratch, dqT, dk, dvT, [dlam, gcol], extra_specs, ctx_into)


def _mla_attn_kernel(q_ref, k_ref, v_ref, *rest, n_lat, qs, latent):
    o_ref, q2_ref, m_ref, acc_ref, s_ref = rest[-5:]
    for i in range(qs):
        q2_ref[:, i * TILE:(i + 1) * TILE] = q_ref[0, i]
    o = _flash_T(k_ref.at[0], v_ref, q2_ref, m_ref, acc_ref, s_ref, n_lat, latent)
    for i in range(qs):
        o_ref[0, i] = o[:, i * TILE:(i + 1) * TILE].astype(BF)


def _mla_attn(mqT, mk, mvT, n_b, n_t, ctx_into=None):
    qs = MLA_QS if ctx_into is None else 1
    kern = functools.partial(_mla_attn_kernel, n_lat=n_t - 1, qs=qs, latent=ctx_into is None)
    k_spec_fn = lambda rows, blk: pl.BlockSpec((1, 1, rows, 192), lambda b, h, q: (b, h, blk, 0))
    scratch = [pltpu.VMEM((192, qs * TILE), BF), pltpu.VMEM((3, 1, qs * TILE), F32),
               pltpu.VMEM((128 + ONES_ROWS, qs * TILE), F32),
               pltpu.VMEM((2, KEY_TILES * TILE, qs * TILE), F32)]
    return _attn_call(kern, "mla_attn" if ctx_into is None else "mla_attn_ctx", 192, k_spec_fn, n_b, MLA_H, n_t,
                      qs, scratch, mqT, mk, mvT, [], [], ctx_into)


def _gla_chunk_order(reverse):
    n_chunk = TILE // GLA_CHUNK
    return range(n_chunk - 1, -1, -1) if reverse else range(n_chunk)


def _gla_tile(q_ref, k_ref, v_ref, la_ref, tri_ref, bi, reverse, work):
    g = la_ref[bi]
    g1 = g.astype(BF)
    r1 = g - g1.astype(F32)
    g2 = r1.astype(BF)
    tri = tri_ref[...]
    bcum = _dot(tri, g1) + _dot(tri, g2)
    n_sub = GLA_CHUNK // GLA_SUB
    r16 = lax.broadcasted_iota(jnp.int32, (GLA_SUB, GLA_CHUNK), 0)
    c16 = lax.broadcasted_iota(jnp.int32, (GLA_SUB, GLA_CHUNK), 1)
    for hh in range(GLA_H):
        kc0, kc1 = GLA_DK * hh, GLA_DK * (hh + 1)
        for c in _gla_chunk_order(reverse):
            r0 = GLA_CHUNK * c
            bc = bcum[r0:r0 + GLA_CHUNK, kc0:kc1]
            qc = q_ref[bi, r0:r0 + GLA_CHUNK, kc0:kc1].astype(F32)
            kc = k_ref[bi, r0:r0 + GLA_CHUNK, kc0:kc1].astype(F32)
            vc = v_ref[bi, r0:r0 + GLA_CHUNK, GLA_DV * hh:GLA_DV * (hh + 1)]
            b_tot = bc[0:1] if reverse else bc[GLA_CHUNK - 1:GLA_CHUNK]
            a_rows = []
            for i in range(n_sub):
                s0 = GLA_SUB * i
                ref_row = bc[s0 + GLA_SUB - 1:s0 + GLA_SUB] if reverse else bc[s0:s0 + 1]
                qs = (qc[s0:s0 + GLA_SUB] * jnp.exp(bc[s0:s0 + GLA_SUB] - ref_row)).astype(BF)
                lo, hi = (s0, GLA_CHUNK) if reverse else (0, s0 + GLA_SUB)
                ks = (kc[lo:hi] * jnp.exp(ref_row - bc[lo:hi])).astype(BF)
                pad = jnp.zeros((GLA_CHUNK - (hi - lo), GLA_DK), BF)
                ks = jnp.concatenate([pad, ks] if reverse else [ks, pad], axis=0) if hi - lo < GLA_CHUNK else ks
                a = _dot_nt(qs, ks)
                keep = (c16 > r16 + s0) if reverse else (c16 <= r16 + s0)
                a_rows.append(jnp.where(keep, a, 0.0))
            q_in = (qc * jnp.exp(bc)).astype(BF)
            k_d = (kc * jnp.exp(b_tot - bc)).astype(BF)
            work[hh, c] = (a_rows, q_in, vc, jnp.exp(b_tot), _dot_tn(vc, k_d))
            yield


def _gla_finish(work, reverse, st_ref, bi, outs):
    heads = []
    for hh in range(GLA_H):
        o_chunks = [None] * (TILE // GLA_CHUNK)
        st = st_ref[bi, hh]
        for c in _gla_chunk_order(reverse):
            a_rows, q_in, vc, decay, st_inc = work[hh, c]
            a_mat = jnp.concatenate(a_rows, axis=0).astype(BF)
            o_chunks[c] = _dot(a_mat, vc) + _dot_nt(q_in, st.astype(BF))
            st = st * decay + st_inc
            yield
        st_ref[bi, hh] = st
        heads.append(jnp.concatenate(o_chunks, axis=0))
    outs[bi] = jnp.concatenate(heads, axis=1)


def _gla_rows(q_ref, k_ref, v_ref, la_ref, tri_ref, st_ref, reverse):
    n = q_ref.shape[0]
    works = [{} for _ in range(n)]
    outs = [None] * n
    free = [_gla_tile(q_ref, k_ref, v_ref, la_ref, tri_ref, bi, reverse, works[bi]) for bi in range(n)]
    dep = [_gla_finish(works[bi], reverse, st_ref, bi, outs) for bi in range(n)]
    for _ in free[0]:
        pass
    for bi in range(1, n):
        pending = [free[bi], dep[bi - 1]]
        while pending:
            pending = [g for g in pending if next(g, _DONE) is not _DONE]
    for _ in dep[n - 1]:
        pass
    return outs


def _gla_fwd_kernel(q_ref, k_ref, v_ref, la_ref, tri_ref, o_ref, st_ref):
    @pl.when(pl.program_id(1) == 0)
    def _():
        st_ref[...] = jnp.zeros(st_ref.shape, F32)
    for bi, o in enumerate(_gla_rows(q_ref, k_ref, v_ref, la_ref, tri_ref, st_ref, False)):
        o_ref[bi] = o


def _gla_bwd_kernel(q_ref, k_ref, v_ref, la_ref, tri_ref, of_ref, r_ref, g_ref, y_ref, st_ref):
    @pl.when(pl.program_id(1) == 0)
    def _():
        st_ref[...] = jnp.zeros(st_ref.shape, F32)
    gn = g_ref[...]
    for bi, o_b in enumerate(_gla_rows(q_ref, k_ref, v_ref, la_ref, tri_ref, st_ref, True)):
        o = of_ref[bi] + o_b
        for hh in range(GLA_H):
            sl = slice(GLA_DV * hh, GLA_DV * (hh + 1))
            oh = o[:, sl]
            yh = oh * lax.rsqrt(jnp.mean(oh * oh, axis=-1, keepdims=True) + EPS) * gn
            y_ref[bi, :, sl] = (yh * _silu(r_ref[bi, :, sl].astype(F32))).astype(BF)


def _gla(gq, gk, gv, la, gr, tri_lo, tri_up, gnorm, n_b, n_t):
    ltot = n_t * TILE
    rows = GLA_BATCH if n_b % GLA_BATCH == 0 else 1
    t_fwd = lambda s: (s + n_t - 1) % n_t
    t_bwd = lambda s: jnp.where(s == 0, n_t - 1, n_t - 1 - s)

    def specs(tmap, dirn):
        ts = lambda c, cb=0: pl.BlockSpec((rows, TILE, c), lambda b, s: (b, tmap(s), cb))
        return [ts(512), ts(512), ts(1024), ts(512, dirn), pl.BlockSpec((TILE, TILE), lambda b, s: (0, 0))]

    o_f = pl.pallas_call(
        _gla_fwd_kernel,
        grid=(n_b // rows, n_t),
        in_specs=specs(t_fwd, 0),
        out_specs=pl.BlockSpec((rows, TILE, 1024), lambda b, s: (b, t_fwd(s), 0)),
        out_shape=jax.ShapeDtypeStruct((n_b, ltot, 1024), F32),
        scratch_shapes=[pltpu.VMEM((rows, GLA_H, GLA_DV, GLA_DK), F32)],
        compiler_params=_cparams(("arbitrary", "arbitrary")),
        name="gla_fwd",
    )(gq, gk, gv, la, tri_lo)
    tsb = lambda c: pl.BlockSpec((rows, TILE, c), lambda b, s: (b, t_bwd(s), 0))
    return pl.pallas_call(
        _gla_bwd_kernel,
        grid=(n_b // rows, n_t),
        in_specs=specs(t_bwd, 1) + [tsb(1024), tsb(1024), pl.BlockSpec((1, GLA_DV), lambda b, s: (0, 0))],
        out_specs=tsb(1024),
        out_shape=jax.ShapeDtypeStruct((n_b, ltot, 1024), BF),
        scratch_shapes=[pltpu.VMEM((rows, GLA_H, GLA_DV, GLA_DK), F32)],
        compiler_params=_cparams(("arbitrary", "arbitrary")),
        name="gla_bwd",
    )(gq, gk, gv, la, tri_up, o_f, gr, gnorm)


def _merge_kernel(x_ref, c_ref, ya_ref, yb_ref, yc_ref, g_ref, g1_ref, wb_ref, wo_ref, lg_ref, lb_ref, o_ref, *,
                  n_lat):
    zt = (_sigmoid(g_ref[0, 0, 0:1024, :].astype(F32)) * _dot_nt(wb_ref[0], ya_ref[0])
          + _sigmoid(g_ref[0, 0, 1024:2048, :].astype(F32)) * _dot(wb_ref[1], yb_ref[0, 0])
          + _sigmoid(g_ref[0, 0, 2048:3072, :].astype(F32)) * _dot(wb_ref[2], yc_ref[0, 0]))
    yield
    u = _dot(wo_ref[...], zt.astype(BF)).T
    y = _ln_rows(DN_ALPHA * _x_tile(x_ref, c_ref, n_lat) + g1_ref[0] * u)
    o_ref[0] = y * lg_ref[...] + lb_ref[...]


def _merge(xs, ya, ybT, ycT, gatesT, mod_l, p, n_b, n_t, n_q):
    x_lat, x_ctx, ctx_blk = xs
    rows = _batch_rows(n_b)
    tile_spec = lambda c: pl.BlockSpec((rows, TILE, c), lambda b, t: (b, t, 0))
    fm_spec = lambda c: pl.BlockSpec((rows, 1, c, TILE), lambda b, t: (b, t, 0, 0))
    mrow = lambda b, t: jnp.where(t == n_t - 1, n_b // rows, b)
    return pl.pallas_call(
        functools.partial(_per_batch_row(_merge_kernel, 7, 4), n_lat=n_t - 1),
        grid=(n_b // rows, n_q),
        in_specs=_x_specs(rows, n_t - 1, ctx_blk) + [
                  tile_spec(1024), fm_spec(1024), fm_spec(1024), fm_spec(3072),
                  pl.BlockSpec((rows, 1, D_MODEL), lambda b, t: (mrow(b, t), 0, 2)),
                  _resident((3, D_MODEL, D_MODEL)), _resident((D_MODEL, D_MODEL)),
                  _resident((1, D_MODEL)), _resident((1, D_MODEL))],
        out_specs=tile_spec(D_MODEL),
        out_shape=jax.ShapeDtypeStruct((n_b, n_q * TILE, D_MODEL), F32),
        compiler_params=_cparams(("arbitrary", "arbitrary")),
        name="merge",
    )(x_lat, x_ctx, ya, ybT, ycT, gatesT, mod_l, p["wbT"], p["woT"], p["ln1_g"], p["ln1_b"])


def _ffn_kernel(x_ref, sh_ref, sc_ref, g2_ref, wi_ref, wo_ref, lg_ref, lb_ref, o_ref):
    x = x_ref[0]
    h = (_ln_rows(x) * (1.0 + sc_ref[0]) + sh_ref[0]).astype(BF)
    acts = []
    for a, b in FFN_CHUNKS:
        gate = _dot(h, wi_ref[:, a:b])
        up = _dot(h, wi_ref[:, FFN_H + a:FFN_H + b])
        acts.append((_silu(gate) * up).astype(BF))
    yield
    acc = jnp.zeros((TILE, D_MODEL), F32)
    for (a, b), act in zip(FFN_CHUNKS, acts):
        acc = acc + _dot(act, wo_ref[a:b, :])
    y = _ln_rows(DN_ALPHA * x + g2_ref[0] * acc)
    o_ref[0] = y * lg_ref[...] + lb_ref[...]


def _ffn(x1, mod_l, p, n_b, n_t, n_q):
    rows = _batch_rows(n_b)
    tile_spec = lambda c: pl.BlockSpec((rows, TILE, c), lambda b, t: (b, t, 0))
    mrow = lambda b, t: jnp.where(t == n_t - 1, n_b // rows, b)
    mspec = lambda j: pl.BlockSpec((rows, 1, D_MODEL), lambda b, t: (mrow(b, t), 0, j))
    return pl.pallas_call(
        _per_batch_row(_ffn_kernel, 4, 4),
        grid=(n_b // rows, n_q),
        in_specs=[tile_spec(D_MODEL), mspec(3), mspec(4), mspec(5),
                  _resident((D_MODEL, 2 * FFN_H)), _resident((FFN_H, D_MODEL)),
                  _resident((1, D_MODEL)), _resident((1, D_MODEL))],
        out_specs=tile_spec(D_MODEL),
        out_shape=jax.ShapeDtypeStruct((n_b, n_q * TILE, D_MODEL), F32),
        compiler_params=_cparams(("arbitrary", "arbitrary")),
        name="ffn",
    )(x1, mod_l, mod_l, mod_l, p["ffn_wi"], p["ffn_wo"], p["ln2_g"], p["ln2_b"])


def _merge_ffn_kernel(x_ref, c_ref, ya_ref, yb_ref, yc_ref, g_ref, g1_ref, sh_ref, sc_ref, g2_ref,
                      wb_ref, wo_ref, lg1_ref, lb1_ref, wi_ref, wo2_ref, lg2_ref, lb2_ref, o_ref, *, n_lat):
    zt = (_sigmoid(g_ref[0, 0, 0:1024, :].astype(F32)) * _dot_nt(wb_ref[0], ya_ref[0])
          + _sigmoid(g_ref[0, 0, 1024:2048, :].astype(F32)) * _dot(wb_ref[1], yb_ref[0, 0])
          + _sigmoid(g_ref[0, 0, 2048:3072, :].astype(F32)) * _dot(wb_ref[2], yc_ref[0, 0]))
    yield
    u = _dot(wo_ref[...], zt.astype(BF)).T
    x1 = _ln_rows(DN_ALPHA * _x_tile(x_ref, c_ref, n_lat) + g1_ref[0] * u) * lg1_ref[...] + lb1_ref[...]
    h = (_ln_rows(x1) * (1.0 + sc_ref[0]) + sh_ref[0]).astype(BF)
    acts = []
    for a, b in FFN_CHUNKS:
        gate = _dot(h, wi_ref[:, a:b])
        up = _dot(h, wi_ref[:, FFN_H + a:FFN_H + b])
        acts.append((_silu(gate) * up).astype(BF))
    yield
    acc = jnp.zeros((TILE, D_MODEL), F32)
    for (a, b), act in zip(FFN_CHUNKS, acts):
        acc = acc + _dot(act, wo2_ref[a:b, :])
    y = _ln_rows(DN_ALPHA * x1 + g2_ref[0] * acc)
    o_ref[0] = y * lg2_ref[...] + lb2_ref[...]


def _merge_ffn(xs, ya, ybT, ycT, gatesT, mod_l, p, n_b, n_t, n_q):
    x_lat, x_ctx, ctx_blk = xs
    rows = _batch_rows(n_b)
    tile_spec = lambda c: pl.BlockSpec((rows, TILE, c), lambda b, t: (b, t, 0))
    fm_spec = lambda c: pl.BlockSpec((rows, 1, c, TILE), lambda b, t: (b, t, 0, 0))
    mrow = lambda b, t: jnp.where(t == n_t - 1, n_b // rows, b)
    mspec = lambda j: pl.BlockSpec((rows, 1, D_MODEL), lambda b, t: (mrow(b, t), 0, j))
    return pl.pallas_call(
        functools.partial(_per_batch_row(_merge_ffn_kernel, 10, 8), n_lat=n_t - 1),
        grid=(n_b // rows, n_q),
        in_specs=_x_specs(rows, n_t - 1, ctx_blk) + [
                  tile_spec(1024), fm_spec(1024), fm_spec(1024), fm_spec(3072),
                  mspec(2), mspec(3), mspec(4), mspec(5),
                  _resident((3, D_MODEL, D_MODEL)), _resident((D_MODEL, D_MODEL)),
                  _resident((1, D_MODEL)), _resident((1, D_MODEL)),
                  _resident((D_MODEL, 2 * FFN_H)), _resident((FFN_H, D_MODEL)),
                  _resident((1, D_MODEL)), _resident((1, D_MODEL))],
        out_specs=tile_spec(D_MODEL),
        out_shape=jax.ShapeDtypeStruct((n_b, n_q * TILE, D_MODEL), F32),
        compiler_params=_cparams(("arbitrary", "arbitrary")),
        name="merge_ffn",
    )(x_lat, x_ctx, ya, ybT, ycT, gatesT, mod_l, mod_l, mod_l, mod_l, p["wbT"], p["woT"], p["ln1_g"], p["ln1_b"],
      p["ffn_wi"], p["ffn_wo"], p["ln2_g"], p["ln2_b"])


def _prep_layer(l, w_in, gla_w_a2, gla_b_a, mla_q_norm_g, mla_kv_norm_g, mla_w_uq, mla_w_ukv, w_branch, w_out,
                ln1_g, ln1_b, ffn_w_in, ffn_w_out, ln2_g, ln2_b):
    w = w_in[l]
    seg = lambda a, b: w[:, a:b]
    w_tok = jnp.concatenate([
        seg(O_GQ, O_GK), seg(O_GK, O_GV), seg(O_GV, O_GR), seg(O_GR, O_GA), seg(O_DK, O_DV),
        seg(O_MQ, O_MKV), seg(O_MKV, O_MKR),
        seg(O_MKR, O_GATES), seg(O_GA, O_DQ), jnp.zeros((D_MODEL, 32), F32)], axis=1).astype(BF)
    w_featT = jnp.concatenate([seg(O_DQ, O_DK) * DIFF_DH ** -0.5, seg(O_DV, O_MQ), seg(O_GATES, O_END)],
                              axis=1).T.astype(BF)
    wa2 = jnp.zeros((128, 1024), F32)
    wa2 = wa2.at[64:80, 0:512].set(gla_w_a2[l, 0]).at[80:96, 512:1024].set(gla_w_a2[l, 1]).astype(BF)
    ukv = mla_w_ukv[l].reshape(MLA_KVR, MLA_H, MLA_NOPE + MLA_DV)
    return dict(
        w_tok=w_tok, w_featT=w_featT, wa2=wa2, ba=gla_b_a[l].reshape(1, 1024),
        qg=mla_q_norm_g[l].reshape(1, MLA_QR), kvg=mla_kv_norm_g[l].reshape(1, MLA_KVR),
        wukv_k=ukv[:, :, :MLA_NOPE].reshape(MLA_KVR, 1024).astype(BF),
        wukv_vT=ukv[:, :, MLA_NOPE:].reshape(MLA_KVR, 1024).T.astype(BF),
        wuqT=mla_w_uq[l].T.astype(BF),
        wbT=jnp.swapaxes(w_branch[l], 1, 2).astype(BF), woT=w_out[l].T.astype(BF),
        ln1_g=ln1_g[l].reshape(1, D_MODEL), ln1_b=ln1_b[l].reshape(1, D_MODEL),
        ffn_wi=ffn_w_in[l].astype(BF), ffn_wo=ffn_w_out[l].astype(BF),
        ln2_g=ln2_g[l].reshape(1, D_MODEL), ln2_b=ln2_b[l].reshape(1, D_MODEL))


def _rope_tables(l_lat, l_ctx):
    rows = l_lat // GRID_W
    pos_row = jnp.broadcast_to(jnp.arange(rows, dtype=F32)[:, None], (rows, GRID_W)).reshape(l_lat)
    pos_col = jnp.broadcast_to(jnp.arange(GRID_W, dtype=F32)[None, :], (rows, GRID_W)).reshape(l_lat)
    d_axis = ROPE_DIM // 2
    inv = ROPE_BASE ** (-jnp.arange(0, d_axis, 2, dtype=F32) / d_axis)
    ang = jnp.concatenate([pos_row[:, None] * inv, pos_col[:, None] * inv], axis=-1)
    cos = jnp.concatenate([jnp.cos(ang), jnp.ones((l_ctx, 32), F32)], axis=0)
    sin = jnp.concatenate([jnp.sin(ang), jnp.zeros((l_ctx, 32), F32)], axis=0)
    ctok = jnp.tile(cos, (1, 4))
    stok = jnp.tile(jnp.concatenate([-sin, sin], axis=1), (1, 2))
    n_t = (l_lat + l_ctx) // TILE
    to_fm = lambda a: a.T.reshape(32, n_t, TILE).transpose(1, 0, 2)
    return ctok, stok, to_fm(cos), to_fm(sin)


def kernel(x, c, ctx, c_ctx, w_mod, b_mod, w_in, gla_w_a2, gla_b_a, gla_norm_g, diff_lam, diff_norm_g,
           mla_q_norm_g, mla_kv_norm_g, mla_w_uq, mla_w_ukv, w_branch, w_out, ln1_g, ln1_b, ffn_w_in, ffn_w_out,
           ln2_g, ln2_b):
    n_b, l_lat, _ = x.shape
    l_ctx = ctx.shape[1]
    assert l_ctx == TILE and n_b + TOK_BATCH <= 16
    assert l_lat % (TILE * max(DIFF_QS, MLA_QS)) == 0 and l_lat % (TILE * KEY_TILES * KEY_UNROLL) == 0
    ltot = l_lat + l_ctx
    n_t = ltot // TILE
    ctok, stok, cosT, sinT = _rope_tables(l_lat, l_ctx)
    c_all = jnp.zeros((16, D_MODEL), F32).at[:n_b].set(c).at[n_b:n_b + TOK_BATCH].set(c_ctx)
    mod = _modulation(c_all, w_mod, b_mod)
    ii = lax.broadcasted_iota(jnp.int32, (TILE, TILE), 0)
    jj = lax.broadcasted_iota(jnp.int32, (TILE, TILE), 1)
    same = (ii // GLA_CHUNK) == (jj // GLA_CHUNK)
    tri_lo = (same & (jj <= ii)).astype(BF)
    tri_up = (same & (jj >= ii)).astype(BF)
    xs = (x, ctx, 0)
    for l in range(N_LAYERS):
        last = l == N_LAYERS - 1
        n_q = n_t - 1 if last else n_t
        lam_init = 0.8 - 0.6 * math.exp(-0.3 * l)
        p = _prep_layer(l, w_in, gla_w_a2, gla_b_a, mla_q_norm_g, mla_kv_norm_g, mla_w_uq, mla_w_ukv, w_branch,
                        w_out, ln1_g, ln1_b, ffn_w_in, ffn_w_out, ln2_g, ln2_b)
        mod_l = mod[l].reshape(16, 1, 6 * D_MODEL)
        gq, gk, gv, gr, dk, la, mk, cq, ckv = _proj_tok(xs, mod_l, ctok, stok, p, n_b, n_t, ltot)
        dqT, dvT, gatesT, mqT, mvT = _proj_feat(xs, mod_l, cosT, sinT, cq, ckv, p, n_b, n_t)
        ya = _gla(gq, gk, gv, la, gr, tri_lo, tri_up, gla_norm_g[l].reshape(1, GLA_DV), n_b, n_t)
        gcol = jnp.broadcast_to(diff_norm_g[l].reshape(128, 1), (128, TILE))
        ybT = _diff_attn(dqT, dk, dvT, diff_lam[l], gcol, n_b, n_t, lam_init)
        ycT = _mla_attn(mqT, mk, mvT, n_b, n_t)
        if not last:
            ybT = _diff_attn(dqT, dk, dvT, diff_lam[l], gcol, n_b, n_t, lam_init, ctx_into=ybT)
            ycT = _mla_attn(mqT, mk, mvT, n_b, n_t, ctx_into=ycT)
        x2 = _merge_ffn(xs, ya, ybT, ycT, gatesT, mod_l, p, n_b, n_t, n_q)
        xs = (x2, x2, n_t - 1)
    return x2
```

```python
import functools
import math

import jax
import jax.numpy as jnp
from jax import lax
from jax.experimental import pallas as pl
from jax.experimental.pallas import tpu as pltpu

BF = jnp.bfloat16
F32 = jnp.float32

D_MODEL = 1024
N_LAYERS = 2
GRID_W = 64
TILE = 256
GLA_H, GLA_DK, GLA_DV, GLA_RANK, GLA_TAU = 4, 128, 256, 16, 16.0
GLA_CHUNK, GLA_SUB = 64, 16
GLA_BATCH = 2
TOK_BATCH = 2
DIFF_H, DIFF_DH = 8, 64
MLA_H, MLA_QR, MLA_KVR, MLA_NOPE, MLA_ROPE, MLA_DV = 8, 256, 128, 128, 64, 128
MLA_SCALE = (MLA_NOPE + MLA_ROPE) ** -0.5
ROPE_DIM, ROPE_BASE = 64, 10000.0
FFN_H = 2816
FFN_CHUNKS = ((0, 1536), (1536, 2816))
DN_ALPHA = (2 * N_LAYERS) ** 0.25
EPS = 1e-6
DIFF_QS, MLA_QS = 8, 16
COL_BLOCK = 512
KEY_TILES = 2
KEY_UNROLL = 4
LOG2E = math.log2(math.e)
ONES_ROWS = 16
VMEM_LIMIT = 56 * 1024 * 1024

_SIZES = (512, 512, 1024, 1024, 32, 1024, 1024, 1024, 256, 128, 64, 3072)
_OFF = [0]
for _s in _SIZES:
    _OFF.append(_OFF[-1] + _s)
(O_GQ, O_GK, O_GV, O_GR, O_GA, O_DQ, O_DK, O_DV, O_MQ, O_MKV, O_MKR, O_GATES, O_END) = _OFF

T_GQ, T_GK, T_GV, T_GR, T_DK, T_MQ, T_MKV, T_SMA, T_END = (0, 512, 1024, 2048, 3072, 4096, 4352, 4480, 4608)
F_DQ, F_DV, F_GATES, F_END = 0, 1024, 2048, 5120


def _cparams(sem):
    return pltpu.CompilerParams(dimension_semantics=sem, vmem_limit_bytes=VMEM_LIMIT)


def _resident(shape):
    nd = len(shape)
    return pl.BlockSpec(shape, lambda *_: (0,) * nd, pipeline_mode=pl.Buffered(1))


def _dot(a, b):
    return jnp.dot(a, b, preferred_element_type=F32)


def _dot_nt(a, b):
    return lax.dot_general(a, b, (((1,), (1,)), ((), ())), preferred_element_type=F32)


def _dot_tn(a, b):
    return lax.dot_general(a, b, (((0,), (0,)), ((), ())), preferred_element_type=F32)


def _ln_rows(x):
    mu = jnp.mean(x, axis=-1, keepdims=True)
    xc = x - mu
    var = jnp.mean(xc * xc, axis=-1, keepdims=True)
    return xc * lax.rsqrt(var + EPS)


def _sigmoid(x):
    return 1.0 / (1.0 + jnp.exp(-x))


def _silu(x):
    return x * _sigmoid(x)


def _mod_kernel(c_ref, w_ref, b_ref, o_ref):
    ca = _silu(c_ref[...]).astype(BF)
    o_ref[0] = _dot(ca, w_ref[0].astype(BF)) + b_ref[0]


def _modulation(c_all, w_mod, b_mod):
    nl, d, n6 = w_mod.shape
    r = c_all.shape[0]
    nblk = 1536
    return pl.pallas_call(
        _mod_kernel,
        grid=(nl, n6 // nblk),
        in_specs=[pl.BlockSpec((r, d), lambda l, j: (0, 0)),
                  pl.BlockSpec((1, d, nblk), lambda l, j: (l, 0, j)),
                  pl.BlockSpec((1, 1, nblk), lambda l, j: (l, 0, j))],
        out_specs=pl.BlockSpec((1, r, nblk), lambda l, j: (l, 0, j)),
        out_shape=jax.ShapeDtypeStruct((nl, r, n6), F32),
        compiler_params=_cparams(("arbitrary", "arbitrary")),
        name="modulation",
    )(c_all, w_mod, b_mod.reshape(nl, 1, n6))


def _batch_rows(n_b):
    return TOK_BATCH if n_b % TOK_BATCH == 0 else 1


_DONE = object()


def _per_batch_row(body, n_rowed, n_shared):
    def kern(*refs, **kw):
        rowed, shared, outs = refs[:n_rowed], refs[n_rowed:n_rowed + n_shared], refs[n_rowed + n_shared:]
        pending = []
        for bi in range(rowed[0].shape[0]):
            row = lambda ref, bi=bi: ref.at[pl.ds(bi, 1)]
            pending.append(body(*map(row, rowed), *shared, *map(row, outs), **kw))
        pending = [g for g in pending if g is not None]
        while pending:
            pending = [g for g in pending if next(g, _DONE) is not _DONE]
    return kern


def _x_specs(rows, n_lat, ctx_blk):
    return [pl.BlockSpec((rows, TILE, D_MODEL), lambda b, t: (b, jnp.minimum(t, n_lat - 1), 0)),
            pl.BlockSpec((rows, TILE, D_MODEL), lambda b, t: (b, ctx_blk, 0))]


def _x_tile(x_ref, c_ref, n_lat):
    return jnp.where(pl.program_id(1) == n_lat, c_ref[0], x_ref[0])


def _proj_tok_kernel(x_ref, c_ref, sh_ref, sc_ref, ct_ref, st_ref, w_ref, wa2_ref, ba_ref, qg_ref, kvg_ref, wk_ref,
                     gq_ref, gk_ref, gv_ref, gr_ref, dk_ref, la_ref, mk_ref, cq_ref, ckv_ref, *, n_lat):
    h = (_ln_rows(_x_tile(x_ref, c_ref, n_lat)) * (1.0 + sc_ref[0]) + sh_ref[0]).astype(BF)

    def proj(a, b):
        return _dot(h, w_ref[:, a:b])

    ct = ct_ref[...]
    st = st_ref[...]
    first_half = lax.broadcasted_iota(jnp.int32, (TILE, 128), 1) % ROPE_DIM < ROPE_DIM // 2

    def rope(x):
        partner = jnp.where(first_half, pltpu.roll(x, 128 - ROPE_DIM // 2, 1), pltpu.roll(x, ROPE_DIM // 2, 1))
        return x * ct + partner * st

    mm = proj(T_MKV, T_END)
    mkv = mm[:, 0:128]
    sma = mm[:, 128:256]
    ckv = (mkv * lax.rsqrt(jnp.mean(mkv * mkv, axis=-1, keepdims=True) + EPS) * kvg_ref[...]).astype(BF)
    ckv_ref[0] = ckv
    mq = proj(T_MQ, T_MKV)
    cq = mq * lax.rsqrt(jnp.mean(mq * mq, axis=-1, keepdims=True) + EPS) * qg_ref[...]
    cq_ref[0] = cq.astype(BF)
    gq_ref[0] = (proj(T_GQ, T_GK) * GLA_DK ** -0.5).astype(BF)
    gk_ref[0] = proj(T_GK, T_GV).astype(BF)
    gv_ref[0] = proj(T_GV, T_GR).astype(BF)
    gr_ref[0] = proj(T_GR, T_DK).astype(BF)
    dkx = proj(T_DK, T_MQ)
    for hh in range(DIFF_H):
        sl = slice(128 * hh, 128 * (hh + 1))
        dk_ref[0, :, sl] = rope(dkx[:, sl]).astype(BF)
    yield
    kn = _dot(ckv, wk_ref[...]).astype(BF)
    kr = rope(sma)[:, 0:64].astype(BF)
    for hh in range(MLA_H):
        mk_ref[0, hh, :, 0:128] = kn[:, 128 * hh:128 * (hh + 1)]
        mk_ref[0, hh, :, 128:192] = kr
    z = _dot(sma.astype(BF), wa2_ref[...]) + ba_ref[...]
    la_ref[0] = (jnp.minimum(z, 0.0) - jnp.log(1.0 + jnp.exp(-jnp.abs(z)))) * (1.0 / GLA_TAU)


def _proj_tok(xs, mod_l, ctok, stok, p, n_b, n_t, ltot):
    x_lat, x_ctx, ctx_blk = xs
    rows = _batch_rows(n_b)
    tile_spec = lambda c: pl.BlockSpec((rows, TILE, c), lambda b, t: (b, t, 0))
    mrow = lambda b, t: jnp.where(t == n_t - 1, n_b // rows, b)
    bf_out = lambda c: jax.ShapeDtypeStruct((n_b, ltot, c), BF)
    return pl.pallas_call(
        functools.partial(_per_batch_row(_proj_tok_kernel, 4, 8), n_lat=n_t - 1),
        grid=(n_b // rows, n_t),
        in_specs=_x_specs(rows, n_t - 1, ctx_blk) + [
                  pl.BlockSpec((rows, 1, D_MODEL), lambda b, t: (mrow(b, t), 0, 0)),
                  pl.BlockSpec((rows, 1, D_MODEL), lambda b, t: (mrow(b, t), 0, 1)),
                  pl.BlockSpec((TILE, 128), lambda b, t: (t, 0)),
                  pl.BlockSpec((TILE, 128), lambda b, t: (t, 0)),
                  _resident((D_MODEL, T_END)), _resident((128, 1024)), _resident((1, 1024)),
                  _resident((1, MLA_QR)), _resident((1, MLA_KVR)), _resident((MLA_KVR, 1024))],
        out_specs=[tile_spec(512), tile_spec(512), tile_spec(1024), tile_spec(1024), tile_spec(1024),
                   tile_spec(1024),
                   pl.BlockSpec((rows, MLA_H, TILE, 192), lambda b, t: (b, 0, t, 0)),
                   tile_spec(MLA_QR), tile_spec(MLA_KVR)],
        out_shape=[bf_out(512), bf_out(512), bf_out(1024), bf_out(1024), bf_out(1024),
                   jax.ShapeDtypeStruct((n_b, ltot, 1024), F32),
                   jax.ShapeDtypeStruct((n_b, MLA_H, ltot, 192), BF),
                   bf_out(MLA_QR), bf_out(MLA_KVR)],
        compiler_params=_cparams(("arbitrary", "arbitrary")),
        name="proj_tok",
    )(x_lat, x_ctx, mod_l, mod_l, ctok, stok, p["w_tok"], p["wa2"], p["ba"], p["qg"], p["kvg"], p["wukv_k"])


def _rope_rows(x1, x2, cos, sin):
    return x1 * cos - x2 * sin, x1 * sin + x2 * cos


def _proj_feat_kernel(x_ref, c_ref, sh_ref, sc_ref, cq_ref, ckv_ref, cos_ref, sin_ref, w_ref, wuq_ref, wv_ref,
                      dq_ref, dv_ref, g_ref, mq_ref, mv_ref, *, n_lat):
    h = (_ln_rows(_x_tile(x_ref, c_ref, n_lat)) * (1.0 + sc_ref[0]) + sh_ref[0]).astype(BF)
    cos = cos_ref[0]
    sin = sin_ref[0]
    dq = _dot_nt(w_ref[F_DQ:F_DV, :], h) * LOG2E
    for g in range(2 * DIFF_H):
        o1, o2 = _rope_rows(dq[64 * g:64 * g + 32], dq[64 * g + 32:64 * (g + 1)], cos, sin)
        dq_ref[0, 0, 64 * g:64 * g + 32, :] = o1.astype(BF)
        dq_ref[0, 0, 64 * g + 32:64 * (g + 1), :] = o2.astype(BF)
    dv_ref[0, 0] = _dot_nt(w_ref[F_DV:F_GATES, :], h).astype(BF)
    for j in range(3):
        a = F_GATES + 1024 * j
        g_ref[0, 0, 1024 * j:1024 * (j + 1), :] = _dot_nt(w_ref[a:a + 1024, :], h).astype(BF)
    cq = cq_ref[0]
    mq = _dot_nt(wuq_ref[...], cq) * (MLA_SCALE * LOG2E)
    for hh in range(MLA_H):
        a = 192 * hh
        o1, o2 = _rope_rows(mq[a + 128:a + 160], mq[a + 160:a + 192], cos, sin)
        mq_ref[0, 0, a:a + 128, :] = mq[a:a + 128].astype(BF)
        mq_ref[0, 0, a + 128:a + 160, :] = o1.astype(BF)
        mq_ref[0, 0, a + 160:a + 192, :] = o2.astype(BF)
    mv_ref[0, 0] = _dot_nt(wv_ref[...], ckv_ref[0]).astype(BF)


def _proj_feat(xs, mod_l, cosT, sinT, cq, ckv, p, n_b, n_t):
    x_lat, x_ctx, ctx_blk = xs
    rows = _batch_rows(n_b)
    tile_spec = lambda c: pl.BlockSpec((rows, TILE, c), lambda b, t: (b, t, 0))
    mrow = lambda b, t: jnp.where(t == n_t - 1, n_b // rows, b)
    fm_spec = lambda c: pl.BlockSpec((rows, 1, c, TILE), lambda b, t: (b, t, 0, 0))
    fm_out = lambda c: jax.ShapeDtypeStruct((n_b, n_t, c, TILE), BF)
    return pl.pallas_call(
        functools.partial(_per_batch_row(_proj_feat_kernel, 6, 5), n_lat=n_t - 1),
        grid=(n_b // rows, n_t),
        in_specs=_x_specs(rows, n_t - 1, ctx_blk) + [
                  pl.BlockSpec((rows, 1, D_MODEL), lambda b, t: (mrow(b, t), 0, 0)),
                  pl.BlockSpec((rows, 1, D_MODEL), lambda b, t: (mrow(b, t), 0, 1)),
                  tile_spec(MLA_QR), tile_spec(MLA_KVR),
                  pl.BlockSpec((1, 32, TILE), lambda b, t: (t, 0, 0)),
                  pl.BlockSpec((1, 32, TILE), lambda b, t: (t, 0, 0)),
                  _resident((F_END, D_MODEL)), _resident((MLA_H * 192, MLA_QR)), _resident((1024, MLA_KVR))],
        out_specs=[fm_spec(1024), fm_spec(1024), fm_spec(3072), fm_spec(MLA_H * 192), fm_spec(1024)],
        out_shape=[fm_out(1024), fm_out(1024), fm_out(3072), fm_out(MLA_H * 192), fm_out(1024)],
        compiler_params=_cparams(("arbitrary", "arbitrary")),
        name="proj_feat",
    )(x_lat, x_ctx, mod_l, mod_l, cq, ckv, cosT, sinT, p["w_featT"], p["wuqT"], p["wukv_vT"])


def _flash_T(k_ref, v_ref, q2_ref, m_ref, acc_ref, s_ref, n_lat, latent_keys):
    dv = v_ref.shape[2]
    nq = q2_ref.shape[1]
    cb = min(COL_BLOCK, nq)
    blocks = [slice(c, c + cb) for c in range(0, nq, cb)]
    m_ref[2] = jnp.full(m_ref.shape[1:], -jnp.inf, F32)
    acc_ref[...] = jnp.zeros(acc_ref.shape, F32)
    rows = KEY_TILES * TILE

    def scores(slot, tile0, cs):
        s = _dot(k_ref[0, pl.ds(pl.multiple_of(tile0 * TILE, TILE), rows), :], q2_ref[:, cs])
        s_ref[slot, :, cs] = s
        m_ref[slot, :, cs] = jnp.max(s, axis=0, keepdims=True)

    def consume(s, s_max, tile0, ntiles, cs):
        m_old = m_ref[2, :, cs]
        m_new = jnp.maximum(m_old, s_max)
        alpha = jnp.exp2(m_old - m_new)
        p = jnp.exp2(s - m_new).astype(BF)
        v = jnp.concatenate([v_ref[0, tile0 + i] for i in range(ntiles)], axis=1)
        vext = jnp.concatenate([v, jnp.ones((ONES_ROWS, ntiles * TILE), BF)], axis=0)
        acc_ref[:, cs] = acc_ref[:, cs] * alpha + _dot(vext, p)
        m_ref[2, :, cs] = m_new

    def ctx_step(tile, cs):
        s = _dot(k_ref[0, pl.ds(tile * TILE, TILE), :], q2_ref[:, cs])
        consume(s, jnp.max(s, axis=0, keepdims=True), tile, 1, cs)

    if latent_keys:
        n_steps = n_lat // KEY_TILES
        for cs in blocks:
            scores(0, 0, cs)

        def body(jj, carry):
            for u in range(KEY_UNROLL):
                j = KEY_UNROLL * jj + u
                for cs in blocks:
                    scores((u + 1) % 2, (j + 1) * KEY_TILES, cs)
                    consume(s_ref[u % 2, :, cs], m_ref[u % 2, :, cs], j * KEY_TILES, KEY_TILES, cs)
            return carry
        lax.fori_loop(0, n_steps // KEY_UNROLL - 1, body, 0)
        for u in range(KEY_UNROLL):
            j = n_steps - KEY_UNROLL + u
            for cs in blocks:
                if u < KEY_UNROLL - 1:
                    scores((u + 1) % 2, (j + 1) * KEY_TILES, cs)
                consume(s_ref[u % 2, :, cs], m_ref[u % 2, :, cs], j * KEY_TILES, KEY_TILES, cs)
        for cs in blocks:
            ctx_step(n_lat, cs)
    else:
        for cs in blocks:
            ctx_step(0, cs)
    acc = acc_ref[...]
    return acc[0:dv] * (1.0 / acc[dv:dv + 1])


def _attn_call(kern, name, q_rows, k_spec_fn, n_b, n_heads, n_t, qs, scratch, qT, k, vT, extra, extra_specs,
               ctx_into=None):
    n_lat = n_t - 1
    if ctx_into is None:
        grid = (n_b, n_heads, n_lat // qs)
        q_spec = pl.BlockSpec((1, qs, q_rows, TILE), lambda b, h, q: (b, q, h, 0))
        k_spec = k_spec_fn(n_t * TILE, 0)
        v_spec = pl.BlockSpec((1, n_t, 128, TILE), lambda b, h, q: (b, 0, h, 0), pipeline_mode=pl.Buffered(1))
        o_spec = pl.BlockSpec((1, qs, 128, TILE), lambda b, h, q: (b, q, h, 0))
        alias_in, alias_specs, aliases = [], [], {}
    else:
        grid = (n_b, n_heads, 1)
        q_spec = pl.BlockSpec((1, 1, q_rows, TILE), lambda b, h, q: (b, n_lat, h, 0))
        k_spec = k_spec_fn(TILE, n_lat)
        v_spec = pl.BlockSpec((1, 1, 128, TILE), lambda b, h, q: (b, n_lat, h, 0))
        o_spec = pl.BlockSpec((1, 1, 128, TILE), lambda b, h, q: (b, n_lat, h, 0))
        alias_in, alias_specs = [ctx_into], [pl.BlockSpec(memory_space=pl.ANY)]
        aliases = {3 + len(extra): 0}
    return pl.pallas_call(
        kern,
        grid=grid,
        in_specs=[q_spec, k_spec, v_spec] + extra_specs + alias_specs,
        out_specs=o_spec,
        out_shape=jax.ShapeDtypeStruct((n_b, n_t, 1024, TILE), BF),
        scratch_shapes=scratch,
        input_output_aliases=aliases,
        compiler_params=_cparams(("arbitrary", "arbitrary", "arbitrary")),
        name=name,
    )(qT, k, vT, *extra, *alias_in)


def _diff_attn_kernel(q_ref, k_ref, v_ref, lam_ref, g_ref, *rest, n_lat, lam_init, qs, latent):
    o_ref, q2_ref, m_ref, acc_ref, s_ref = rest[-5:]
    nq = qs * TILE
    zeros = jnp.zeros((DIFF_DH, TILE), BF)
    for i in range(qs):
        q = q_ref[0, i]
        q2_ref[0:64, i * TILE:(i + 1) * TILE] = q[0:64]
        q2_ref[64:128, i * TILE:(i + 1) * TILE] = zeros
        q2_ref[0:64, nq + i * TILE:nq + (i + 1) * TILE] = zeros
        q2_ref[64:128, nq + i * TILE:nq + (i + 1) * TILE] = q[64:128]
    o = _flash_T(k_ref, v_ref, q2_ref, m_ref, acc_ref, s_ref, n_lat, latent)
    dl = lam_ref[...]
    lam = (jnp.exp(jnp.sum(dl[0:1] * dl[1:2], axis=1, keepdims=True))
           - jnp.exp(jnp.sum(dl[2:3] * dl[3:4], axis=1, keepdims=True)) + lam_init)
    od = o[:, 0:nq] - lam * o[:, nq:2 * nq]
    y = od * lax.rsqrt(jnp.mean(od * od, axis=0, keepdims=True) + EPS) * (1.0 - lam_init)
    for i in range(qs):
        o_ref[0, i] = (y[:, i * TILE:(i + 1) * TILE] * g_ref[...]).astype(BF)


def _diff_attn(dqT, dk, dvT, dlam, gcol, n_b, n_t, lam_init, ctx_into=None):
    qs = DIFF_QS if ctx_into is None else 1
    kern = functools.partial(_diff_attn_kernel, n_lat=n_t - 1, lam_init=lam_init, qs=qs, latent=ctx_into is None)
    k_spec_fn = lambda rows, blk: pl.BlockSpec((1, rows, 128), lambda b, h, q: (b, blk, h),
                                               pipeline_mode=pl.Buffered(1))
    scratch = [pltpu.VMEM((128, 2 * qs * TILE), BF), pltpu.VMEM((3, 1, 2 * qs * TILE), F32),
               pltpu.VMEM((128 + ONES_ROWS, 2 * qs * TILE), F32),
               pltpu.VMEM((2, KEY_TILES * TILE, 2 * qs * TILE), F32)]
    extra_specs = [pl.BlockSpec((4, DIFF_DH), lambda b, h, q: (0, 0)),
                   pl.BlockSpec((128, TILE), lambda b, h, q: (0, 0))]
    return _attn_call(kern, "diff_attn" if ctx_into is None else "diff_attn_ctx", 128, k_spec_fn, n_b, DIFF_H, n_t,
                      qs, scratch, dqT, dk, dvT, [dlam, gcol], extra_specs, ctx_into)


def _mla_attn_kernel(q_ref, k_ref, v_ref, *rest, n_lat, qs, latent):
    o_ref, q2_ref, m_ref, acc_ref, s_ref = rest[-5:]
    for i in range(qs):
        q2_ref[:, i * TILE:(i + 1) * TILE] = q_ref[0, i]
    o = _flash_T(k_ref.at[0], v_ref, q2_ref, m_ref, acc_ref, s_ref, n_lat, latent)
    for i in range(qs):
        o_ref[0, i] = o[:, i * TILE:(i + 1) * TILE].astype(BF)


def _mla_attn(mqT, mk, mvT, n_b, n_t, ctx_into=None):
    qs = MLA_QS if ctx_into is None else 1
    kern = functools.partial(_mla_attn_kernel, n_lat=n_t - 1, qs=qs, latent=ctx_into is None)
    k_spec_fn = lambda rows, blk: pl.BlockSpec((1, 1, rows, 192), lambda b, h, q: (b, h, blk, 0),
                                               pipeline_mode=pl.Buffered(1))
    scratch = [pltpu.VMEM((192, qs * TILE), BF), pltpu.VMEM((3, 1, qs * TILE), F32),
               pltpu.VMEM((128 + ONES_ROWS, qs * TILE), F32),
               pltpu.VMEM((2, KEY_TILES * TILE, qs * TILE), F32)]
    return _attn_call(kern, "mla_attn" if ctx_into is None else "mla_attn_ctx", 192, k_spec_fn, n_b, MLA_H, n_t,
                      qs, scratch, mqT, mk, mvT, [], [], ctx_into)


def _gla_chunk_order(reverse):
    n_chunk = TILE // GLA_CHUNK
    return range(n_chunk - 1, -1, -1) if reverse else range(n_chunk)


def _gla_tile(q_ref, k_ref, v_ref, la_ref, tri_ref, bi, reverse, work):
    g = la_ref[bi]
    g1 = g.astype(BF)
    r1 = g - g1.astype(F32)
    g2 = r1.astype(BF)
    tri = tri_ref[...]
    bcum = _dot(tri, g1) + _dot(tri, g2)
    n_sub = GLA_CHUNK // GLA_SUB
    r16 = lax.broadcasted_iota(jnp.int32, (GLA_SUB, GLA_CHUNK), 0)
    c16 = lax.broadcasted_iota(jnp.int32, (GLA_SUB, GLA_CHUNK), 1)
    for hh in range(GLA_H):
        kc0, kc1 = GLA_DK * hh, GLA_DK * (hh + 1)
        for c in _gla_chunk_order(reverse):
            r0 = GLA_CHUNK * c
            bc = bcum[r0:r0 + GLA_CHUNK, kc0:kc1]
            qc = q_ref[bi, r0:r0 + GLA_CHUNK, kc0:kc1].astype(F32)
            kc = k_ref[bi, r0:r0 + GLA_CHUNK, kc0:kc1].astype(F32)
            vc = v_ref[bi, r0:r0 + GLA_CHUNK, GLA_DV * hh:GLA_DV * (hh + 1)]
            b_tot = bc[0:1] if reverse else bc[GLA_CHUNK - 1:GLA_CHUNK]
            a_rows = []
            for i in range(n_sub):
                s0 = GLA_SUB * i
                ref_row = bc[s0 + GLA_SUB - 1:s0 + GLA_SUB] if reverse else bc[s0:s0 + 1]
                qs = (qc[s0:s0 + GLA_SUB] * jnp.exp(bc[s0:s0 + GLA_SUB] - ref_row)).astype(BF)
                lo, hi = (s0, GLA_CHUNK) if reverse else (0, s0 + GLA_SUB)
                ks = (kc[lo:hi] * jnp.exp(ref_row - bc[lo:hi])).astype(BF)
                pad = jnp.zeros((GLA_CHUNK - (hi - lo), GLA_DK), BF)
                ks = jnp.concatenate([pad, ks] if reverse else [ks, pad], axis=0) if hi - lo < GLA_CHUNK else ks
                a = _dot_nt(qs, ks)
                keep = (c16 > r16 + s0) if reverse else (c16 <= r16 + s0)
                a_rows.append(jnp.where(keep, a, 0.0))
            q_in = (qc * jnp.exp(bc)).astype(BF)
            k_d = (kc * jnp.exp(b_tot - bc)).astype(BF)
            work[hh, c] = (a_rows, q_in, vc, jnp.exp(b_tot), _dot_tn(vc, k_d))
            yield


def _gla_finish(work, reverse, st_ref, bi, outs):
    heads = []
    for hh in range(GLA_H):
        o_chunks = [None] * (TILE // GLA_CHUNK)
        st = st_ref[bi, hh]
        for c in _gla_chunk_order(reverse):
            a_rows, q_in, vc, decay, st_inc = work[hh, c]
            a_mat = jnp.concatenate(a_rows, axis=0).astype(BF)
            o_chunks[c] = _dot(a_mat, vc) + _dot_nt(q_in, st.astype(BF))
            st = st * decay + st_inc
            yield
        st_ref[bi, hh] = st
        heads.append(jnp.concatenate(o_chunks, axis=0))
    outs[bi] = jnp.concatenate(heads, axis=1)


def _gla_rows(q_ref, k_ref, v_ref, la_ref, tri_ref, st_ref, reverse):
    n = q_ref.shape[0]
    works = [{} for _ in range(n)]
    outs = [None] * n
    free = [_gla_tile(q_ref, k_ref, v_ref, la_ref, tri_ref, bi, reverse, works[bi]) for bi in range(n)]
    dep = [_gla_finish(works[bi], reverse, st_ref, bi, outs) for bi in range(n)]
    for _ in free[0]:
        pass
    for bi in range(1, n):
        pending = [free[bi], dep[bi - 1]]
        while pending:
            pending = [g for g in pending if next(g, _DONE) is not _DONE]
    for _ in dep[n - 1]:
        pass
    return outs


def _gla_fwd_kernel(q_ref, k_ref, v_ref, la_ref, tri_ref, o_ref, st_ref):
    @pl.when(pl.program_id(1) == 0)
    def _():
        st_ref[...] = jnp.zeros(st_ref.shape, F32)
    for bi, o in enumerate(_gla_rows(q_ref, k_ref, v_ref, la_ref, tri_ref, st_ref, False)):
        o_ref[bi] = o


def _gla_bwd_kernel(q_ref, k_ref, v_ref, la_ref, tri_ref, of_ref, r_ref, g_ref, y_ref, st_ref):
    @pl.when(pl.program_id(1) == 0)
    def _():
        st_ref[...] = jnp.zeros(st_ref.shape, F32)
    gn = g_ref[...]
    for bi, o_b in enumerate(_gla_rows(q_ref, k_ref, v_ref, la_ref, tri_ref, st_ref, True)):
        o = of_ref[bi] + o_b
        for hh in range(GLA_H):
            sl = slice(GLA_DV * hh, GLA_DV * (hh + 1))
            oh = o[:, sl]
            yh = oh * lax.rsqrt(jnp.mean(oh * oh, axis=-1, keepdims=True) + EPS) * gn
            y_ref[bi, :, sl] = (yh * _silu(r_ref[bi, :, sl].astype(F32))).astype(BF)


def _gla(gq, gk, gv, la, gr, tri_lo, tri_up, gnorm, n_b, n_t):
    ltot = n_t * TILE
    rows = GLA_BATCH if n_b % GLA_BATCH == 0 else 1
    t_fwd = lambda s: (s + n_t - 1) % n_t
    t_bwd = lambda s: jnp.where(s == 0, n_t - 1, n_t - 1 - s)

    def specs(tmap, dirn):
        ts = lambda c, cb=0: pl.BlockSpec((rows, TILE, c), lambda b, s: (b, tmap(s), cb))
        return [ts(512), ts(512), ts(1024), ts(512, dirn), pl.BlockSpec((TILE, TILE), lambda b, s: (0, 0))]

    o_f = pl.pallas_call(
        _gla_fwd_kernel,
        grid=(n_b // rows, n_t),
        in_specs=specs(t_fwd, 0),
        out_specs=pl.BlockSpec((rows, TILE, 1024), lambda b, s: (b, t_fwd(s), 0)),
        out_shape=jax.ShapeDtypeStruct((n_b, ltot, 1024), F32),
        scratch_shapes=[pltpu.VMEM((rows, GLA_H, GLA_DV, GLA_DK), F32)],
        compiler_params=_cparams(("arbitrary", "arbitrary")),
        name="gla_fwd",
    )(gq, gk, gv, la, tri_lo)
    tsb = lambda c: pl.BlockSpec((rows, TILE, c), lambda b, s: (b, t_bwd(s), 0))
    return pl.pallas_call(
        _gla_bwd_kernel,
        grid=(n_b // rows, n_t),
        in_specs=specs(t_bwd, 1) + [tsb(1024), tsb(1024), pl.BlockSpec((1, GLA_DV), lambda b, s: (0, 0))],
        out_specs=tsb(1024),
        out_shape=jax.ShapeDtypeStruct((n_b, ltot, 1024), BF),
        scratch_shapes=[pltpu.VMEM((rows, GLA_H, GLA_DV, GLA_DK), F32)],
        compiler_params=_cparams(("arbitrary", "arbitrary")),
        name="gla_bwd",
    )(gq, gk, gv, la, tri_up, o_f, gr, gnorm)


def _merge_kernel(x_ref, c_ref, ya_ref, yb_ref, yc_ref, g_ref, g1_ref, wb_ref, wo_ref, lg_ref, lb_ref, o_ref, *,
                  n_lat):
    zt = (_sigmoid(g_ref[0, 0, 0:1024, :].astype(F32)) * _dot_nt(wb_ref[0], ya_ref[0])
          + _sigmoid(g_ref[0, 0, 1024:2048, :].astype(F32)) * _dot(wb_ref[1], yb_ref[0, 0])
          + _sigmoid(g_ref[0, 0, 2048:3072, :].astype(F32)) * _dot(wb_ref[2], yc_ref[0, 0]))
    yield
    u = _dot(wo_ref[...], zt.astype(BF)).T
    y = _ln_rows(DN_ALPHA * _x_tile(x_ref, c_ref, n_lat) + g1_ref[0] * u)
    o_ref[0] = y * lg_ref[...] + lb_ref[...]


def _merge(xs, ya, ybT, ycT, gatesT, mod_l, p, n_b, n_t, n_q):
    x_lat, x_ctx, ctx_blk = xs
    rows = _batch_rows(n_b)
    tile_spec = lambda c: pl.BlockSpec((rows, TILE, c), lambda b, t: (b, t, 0))
    fm_spec = lambda c: pl.BlockSpec((rows, 1, c, TILE), lambda b, t: (b, t, 0, 0))
    mrow = lambda b, t: jnp.where(t == n_t - 1, n_b // rows, b)
    return pl.pallas_call(
        functools.partial(_per_batch_row(_merge_kernel, 7, 4), n_lat=n_t - 1),
        grid=(n_b // rows, n_q),
        in_specs=_x_specs(rows, n_t - 1, ctx_blk) + [
                  tile_spec(1024), fm_spec(1024), fm_spec(1024), fm_spec(3072),
                  pl.BlockSpec((rows, 1, D_MODEL), lambda b, t: (mrow(b, t), 0, 2)),
                  _resident((3, D_MODEL, D_MODEL)), _resident((D_MODEL, D_MODEL)),
                  _resident((1, D_MODEL)), _resident((1, D_MODEL))],
        out_specs=tile_spec(D_MODEL),
        out_shape=jax.ShapeDtypeStruct((n_b, n_q * TILE, D_MODEL), F32),
        compiler_params=_cparams(("arbitrary", "arbitrary")),
        name="merge",
    )(x_lat, x_ctx, ya, ybT, ycT, gatesT, mod_l, p["wbT"], p["woT"], p["ln1_g"], p["ln1_b"])


def _ffn_kernel(x_ref, sh_ref, sc_ref, g2_ref, wi_ref, wo_ref, lg_ref, lb_ref, o_ref):
    x = x_ref[0]
    h = (_ln_rows(x) * (1.0 + sc_ref[0]) + sh_ref[0]).astype(BF)
    acts = []
    for a, b in FFN_CHUNKS:
        gate = _dot(h, wi_ref[:, a:b])
        up = _dot(h, wi_ref[:, FFN_H + a:FFN_H + b])
        acts.append((_silu(gate) * up).astype(BF))
    yield
    acc = jnp.zeros((TILE, D_MODEL), F32)
    for (a, b), act in zip(FFN_CHUNKS, acts):
        acc = acc + _dot(act, wo_ref[a:b, :])
    y = _ln_rows(DN_ALPHA * x + g2_ref[0] * acc)
    o_ref[0] = y * lg_ref[...] + lb_ref[...]


def _ffn(x1, mod_l, p, n_b, n_t, n_q):
    rows = _batch_rows(n_b)
    tile_spec = lambda c: pl.BlockSpec((rows, TILE, c), lambda b, t: (b, t, 0))
    mrow = lambda b, t: jnp.where(t == n_t - 1, n_b // rows, b)
    mspec = lambda j: pl.BlockSpec((rows, 1, D_MODEL), lambda b, t: (mrow(b, t), 0, j))
    return pl.pallas_call(
        _per_batch_row(_ffn_kernel, 4, 4),
        grid=(n_b // rows, n_q),
        in_specs=[tile_spec(D_MODEL), mspec(3), mspec(4), mspec(5),
                  _resident((D_MODEL, 2 * FFN_H)), _resident((FFN_H, D_MODEL)),
                  _resident((1, D_MODEL)), _resident((1, D_MODEL))],
        out_specs=tile_spec(D_MODEL),
        out_shape=jax.ShapeDtypeStruct((n_b, n_q * TILE, D_MODEL), F32),
        compiler_params=_cparams(("arbitrary", "arbitrary")),
        name="ffn",
    )(x1, mod_l, mod_l, mod_l, p["ffn_wi"], p["ffn_wo"], p["ln2_g"], p["ln2_b"])


def _prep_layer(l, w_in, gla_w_a2, gla_b_a, mla_q_norm_g, mla_kv_norm_g, mla_w_uq, mla_w_ukv, w_branch, w_out,
                ln1_g, ln1_b, ffn_w_in, ffn_w_out, ln2_g, ln2_b):
    w = w_in[l]
    seg = lambda a, b: w[:, a:b]
    w_tok = jnp.concatenate([
        seg(O_GQ, O_GK), seg(O_GK, O_GV), seg(O_GV, O_GR), seg(O_GR, O_GA), seg(O_DK, O_DV),
        seg(O_MQ, O_MKV), seg(O_MKV, O_MKR),
        seg(O_MKR, O_GATES), seg(O_GA, O_DQ), jnp.zeros((D_MODEL, 32), F32)], axis=1).astype(BF)
    w_featT = jnp.concatenate([seg(O_DQ, O_DK) * DIFF_DH ** -0.5, seg(O_DV, O_MQ), seg(O_GATES, O_END)],
                              axis=1).T.astype(BF)
    wa2 = jnp.zeros((128, 1024), F32)
    wa2 = wa2.at[64:80, 0:512].set(gla_w_a2[l, 0]).at[80:96, 512:1024].set(gla_w_a2[l, 1]).astype(BF)
    ukv = mla_w_ukv[l].reshape(MLA_KVR, MLA_H, MLA_NOPE + MLA_DV)
    return dict(
        w_tok=w_tok, w_featT=w_featT, wa2=wa2, ba=gla_b_a[l].reshape(1, 1024),
        qg=mla_q_norm_g[l].reshape(1, MLA_QR), kvg=mla_kv_norm_g[l].reshape(1, MLA_KVR),
        wukv_k=ukv[:, :, :MLA_NOPE].reshape(MLA_KVR, 1024).astype(BF),
        wukv_vT=ukv[:, :, MLA_NOPE:].reshape(MLA_KVR, 1024).T.astype(BF),
        wuqT=mla_w_uq[l].T.astype(BF),
        wbT=jnp.swapaxes(w_branch[l], 1, 2).astype(BF), woT=w_out[l].T.astype(BF),
        ln1_g=ln1_g[l].reshape(1, D_MODEL), ln1_b=ln1_b[l].reshape(1, D_MODEL),
        ffn_wi=ffn_w_in[l].astype(BF), ffn_wo=ffn_w_out[l].astype(BF),
        ln2_g=ln2_g[l].reshape(1, D_MODEL), ln2_b=ln2_b[l].reshape(1, D_MODEL))


def _rope_tables(l_lat, l_ctx):
    rows = l_lat // GRID_W
    pos_row = jnp.broadcast_to(jnp.arange(rows, dtype=F32)[:, None], (rows, GRID_W)).reshape(l_lat)
    pos_col = jnp.broadcast_to(jnp.arange(GRID_W, dtype=F32)[None, :], (rows, GRID_W)).reshape(l_lat)
    d_axis = ROPE_DIM // 2
    inv = ROPE_BASE ** (-jnp.arange(0, d_axis, 2, dtype=F32) / d_axis)
    ang = jnp.concatenate([pos_row[:, None] * inv, pos_col[:, None] * inv], axis=-1)
    cos = jnp.concatenate([jnp.cos(ang), jnp.ones((l_ctx, 32), F32)], axis=0)
    sin = jnp.concatenate([jnp.sin(ang), jnp.zeros((l_ctx, 32), F32)], axis=0)
    ctok = jnp.tile(cos, (1, 4))
    stok = jnp.tile(jnp.concatenate([-sin, sin], axis=1), (1, 2))
    n_t = (l_lat + l_ctx) // TILE
    to_fm = lambda a: a.T.reshape(32, n_t, TILE).transpose(1, 0, 2)
    return ctok, stok, to_fm(cos), to_fm(sin)


def kernel(x, c, ctx, c_ctx, w_mod, b_mod, w_in, gla_w_a2, gla_b_a, gla_norm_g, diff_lam, diff_norm_g,
           mla_q_norm_g, mla_kv_norm_g, mla_w_uq, mla_w_ukv, w_branch, w_out, ln1_g, ln1_b, ffn_w_in, ffn_w_out,
           ln2_g, ln2_b):
    n_b, l_lat, _ = x.shape
    l_ctx = ctx.shape[1]
    assert l_ctx == TILE and n_b + TOK_BATCH <= 16
    assert l_lat % (TILE * max(DIFF_QS, MLA_QS)) == 0 and l_lat % (TILE * KEY_TILES * KEY_UNROLL) == 0
    ltot = l_lat + l_ctx
    n_t = ltot // TILE
    ctok, stok, cosT, sinT = _rope_tables(l_lat, l_ctx)
    c_all = jnp.zeros((16, D_MODEL), F32).at[:n_b].set(c).at[n_b:n_b + TOK_BATCH].set(c_ctx)
    mod = _modulation(c_all, w_mod, b_mod)
    ii = lax.broadcasted_iota(jnp.int32, (TILE, TILE), 0)
    jj = lax.broadcasted_iota(jnp.int32, (TILE, TILE), 1)
    same = (ii // GLA_CHUNK) == (jj // GLA_CHUNK)
    tri_lo = (same & (jj <= ii)).astype(BF)
    tri_up = (same & (jj >= ii)).astype(BF)
    xs = (x, ctx, 0)
    for l in range(N_LAYERS):
        last = l == N_LAYERS - 1
        n_q = n_t - 1 if last else n_t
        lam_init = 0.8 - 0.6 * math.exp(-0.3 * l)
        p = _prep_layer(l, w_in, gla_w_a2, gla_b_a, mla_q_norm_g, mla_kv_norm_g, mla_w_uq, mla_w_ukv, w_branch,
                        w_out, ln1_g, ln1_b, ffn_w_in, ffn_w_out, ln2_g, ln2_b)
        mod_l = mod[l].reshape(16, 1, 6 * D_MODEL)
        gq, gk, gv, gr, dk, la, mk, cq, ckv = _proj_tok(xs, mod_l, ctok, stok, p, n_b, n_t, ltot)
        dqT, dvT, gatesT, mqT, mvT = _proj_feat(xs, mod_l, cosT, sinT, cq, ckv, p, n_b, n_t)
        ya = _gla(gq, gk, gv, la, gr, tri_lo, tri_up, gla_norm_g[l].reshape(1, GLA_DV), n_b, n_t)
        gcol = jnp.broadcast_to(diff_norm_g[l].reshape(128, 1), (128, TILE))
        ybT = _diff_attn(dqT, dk, dvT, diff_lam[l], gcol, n_b, n_t, lam_init)
        ycT = _mla_attn(mqT, mk, mvT, n_b, n_t)
        if not last:
            ybT = _diff_attn(dqT, dk, dvT, diff_lam[l], gcol, n_b, n_t, lam_init, ctx_into=ybT)
            ycT = _mla_attn(mqT, mk, mvT, n_b, n_t, ctx_into=ycT)
        x1 = _merge(xs, ya, ybT, ycT, gatesT, mod_l, p, n_b, n_t, n_q)
        x2 = _ffn(x1, mod_l, p, n_b, n_t, n_q)
        xs = (x2, x2, n_t - 1)
    return x2
```
